```python
import math
import jax, jax.numpy as jnp
from jax import lax
import numpy as np

D_MODEL = 2048
BATCH = 2
SEQ = 4096
DEPTH = 2
DEC_BATCH = 32
DEC_SEQ = 8
PAST_LEN = 16384
PAGE_SIZE = 128

N_HEADS = 32
N_KV_HEADS = 8
HEAD_DIM = 64
GROUP = N_HEADS // N_KV_HEADS
WINDOW = 128
ATTN_BLOCK = WINDOW
ROPE_THETA = 10000.0
LRU_WIDTH = D_MODEL
LRU_BLOCKS = 8
LRU_BLOCK_SIZE = LRU_WIDTH // LRU_BLOCKS
CONV_WIDTH = 4
LRU_C = 8.0
D_FF = 5632
N_MEM = 256
N_MEM_HEADS = 4
MEM_HEAD_DIM = 128
MEM_WIDTH = N_MEM_HEADS * MEM_HEAD_DIM
N_SWA_LAYERS = (DEPTH + 1) // 2
N_LRU_LAYERS = DEPTH // 2
NORM_EPS = 1e-6

kernel_name = 'hybrid_swa_sink_rglru_macaron_memxattn_step'


def rms_norm(x, g):
    xf = x.astype(jnp.float32)
    y = xf * lax.rsqrt(jnp.mean(xf * xf, axis=-1, keepdims=True) + NORM_EPS)
    return (y * g.astype(jnp.float32)).astype(x.dtype)


def rope(x, pos):
    half = HEAD_DIM // 2
    inv = ROPE_THETA ** (-jnp.arange(half, dtype=jnp.float32) * (2.0 / HEAD_DIM))
    ang = pos.astype(jnp.float32)[:, None] * inv[None, :]
    cos = jnp.cos(ang)[:, None, :]
    sin = jnp.sin(ang)[:, None, :]
    xf = x.astype(jnp.float32)
    x1, x2 = xf[..., :half], xf[..., half:]
    return jnp.concatenate([x1 * cos - x2 * sin, x2 * cos + x1 * sin], axis=-1).astype(x.dtype)


def swiglu(h, w_in, w_out):
    g, u = jnp.split(h @ w_in, 2, axis=-1)
    return (jax.nn.silu(g) * u) @ w_out


def window_mask(qp, kp):
    d = qp - kp
    return (d >= 0) & (d < WINDOW) & (kp >= 0)


def sink_attention(q, k, v, mask, sinks):
    s = jnp.einsum('...qkgd,...skd->...kgqs', q, k).astype(jnp.float32) * (HEAD_DIM ** -0.5)
    s = jnp.where(mask[..., None, None, :, :], s, -jnp.inf)
    sink = jnp.broadcast_to(sinks.astype(jnp.float32)[:, :, None, None], s.shape[:-1] + (1,))
    p = jax.nn.softmax(jnp.concatenate([s, sink], axis=-1), axis=-1)[..., :-1]
    return jnp.einsum('...kgqs,...skd->...qkgd', p.astype(v.dtype), v)


def swa_mixer(h, start, past_k, past_v, w_qkv, w_o, sinks):
    b, t, _ = h.shape
    qw, kw = N_HEADS * HEAD_DIM, N_KV_HEADS * HEAD_DIM
    qkv = h @ w_qkv
    q = qkv[..., :qw].reshape(b, t, N_HEADS, HEAD_DIM)
    k = qkv[..., qw:qw + kw].reshape(b, t, N_KV_HEADS, HEAD_DIM)
    v = qkv[..., qw + kw:].reshape(b, t, N_KV_HEADS, HEAD_DIM)
    pos = start + jnp.arange(t, dtype=jnp.int32)
    q = rope(q, pos)
    k = rope(k, pos)
    sk = sinks.reshape(N_KV_HEADS, GROUP)
    if past_k is None:
        nb = t // ATTN_BLOCK
        qb = q.reshape(b, nb, ATTN_BLOCK, N_KV_HEADS, GROUP, HEAD_DIM)

        def with_prev(xb):
            xb = xb.reshape(b, nb, ATTN_BLOCK, N_KV_HEADS, HEAD_DIM)
            prev = jnp.concatenate([jnp.zeros_like(xb[:, :1]), xb[:, :-1]], axis=1)
            return jnp.concatenate([prev, xb], axis=2)

        qpos = pos.reshape(nb, ATTN_BLOCK)
        kpos = qpos[:, :1] - ATTN_BLOCK + jnp.arange(2 * ATTN_BLOCK, dtype=jnp.int32)[None, :]
        mask = window_mask(qpos[:, :, None], kpos[:, None, :])
        o = sink_attention(qb, with_prev(k), with_prev(v), mask, sk)
        k_all, v_all = k, v
    else:
        k_all = jnp.concatenate([past_k.astype(k.dtype), k], axis=1)
        v_all = jnp.concatenate([past_v.astype(v.dtype), v], axis=1)
        kpos = start - WINDOW + jnp.arange(WINDOW + t, dtype=jnp.int32)
        mask = window_mask(pos[:, None], kpos[None, :])
        o = sink_attention(q.reshape(b, t, N_KV_HEADS, GROUP, HEAD_DIM), k_all, v_all, mask, sk)
    o = o.reshape(b, t, N_HEADS * HEAD_DIM) @ w_o
    return o, k_all[:, -WINDOW:], v_all[:, -WINDOW:]


def rglru_mixer(h, h0, conv_buf, w_in, w_conv, b_conv, w_gate_a, b_gate_a, w_gate_x, b_gate_x, lam, w_out):
    b, t, _ = h.shape
    xy = h @ w_in
    xb, yb = xy[..., :LRU_WIDTH], xy[..., LRU_WIDTH:]
    gate = jax.nn.gelu(yb, approximate=True)
    xp = jnp.concatenate([conv_buf.astype(xb.dtype), xb], axis=1)
    xc = b_conv + sum(w_conv[j] * xp[:, j:j + t] for j in range(CONV_WIDTH))
    xblk = xc.reshape(b, t, LRU_BLOCKS, LRU_BLOCK_SIZE)
    r = jax.nn.sigmoid(jnp.einsum('btnc,ncd->btnd', xblk, w_gate_a).reshape(b, t, LRU_WIDTH) + b_gate_a)
    ig = jax.nn.sigmoid(jnp.einsum('btnc,ncd->btnd', xblk, w_gate_x).reshape(b, t, LRU_WIDTH) + b_gate_x)
    log_a = LRU_C * r.astype(jnp.float32) * jax.nn.log_sigmoid(lam.astype(jnp.float32))
    a = jnp.exp(log_a)
    u = jnp.sqrt(-jnp.expm1(2.0 * log_a)) * (ig * xc).astype(jnp.float32)
    u = u.at[:, 0].add(a[:, 0] * h0.astype(jnp.float32))

    def combine(lhs, rhs):
        a1, b1 = lhs
        a2, b2 = rhs
        return a1 * a2, a2 * b1 + b2

    _, hs = lax.associative_scan(combine, (a, u), axis=1)
    y = (hs.astype(h.dtype) * gate) @ w_out
    return y, hs[:, -1].astype(h.dtype), xp[:, -(CONV_WIDTH - 1):]


def mem_kv(mem, g, w_kv):
    b, n, _ = mem.shape
    m = rms_norm(mem, g) @ w_kv
    k = m[..., :MEM_WIDTH].reshape(b, n, N_MEM_HEADS, MEM_HEAD_DIM)
    v = m[..., MEM_WIDTH:].reshape(b, n, N_MEM_HEADS, MEM_HEAD_DIM)
    return k, v


def cross_attn(h, mk, mv, w_q, w_o):
    b, t, _ = h.shape
    q = (h @ w_q).reshape(b, t, N_MEM_HEADS, MEM_HEAD_DIM)
    s = jnp.einsum('bthd,bmhd->bhtm', q, mk.astype(q.dtype)).astype(jnp.float32) * (MEM_HEAD_DIM ** -0.5)
    p = jax.nn.softmax(s, axis=-1).astype(h.dtype)
    o = jnp.einsum('bhtm,bmhd->bthd', p, mv.astype(h.dtype)).reshape(b, t, MEM_WIDTH)
    return o @ w_o


def run_trunk(x, start, swa_k, swa_v, lru_h, lru_conv, mem_k, mem_v, p):
    new_k, new_v, new_h, new_conv = [], [], [], []
    for layer in range(DEPTH):
        x = x + 0.5 * swiglu(rms_norm(x, p['ln_ffn1'][layer]), p['ffn1_w_in'][layer], p['ffn1_w_out'][layer])
        hn = rms_norm(x, p['ln_mix'][layer])
        j = layer // 2
        if layer % 2 == 0:
            o, kb, vb = swa_mixer(hn, start,
                                  None if swa_k is None else swa_k[j],
                                  None if swa_v is None else swa_v[j],
                                  p['swa_w_qkv'][j], p['swa_w_o'][j], p['swa_sinks'][j])
            new_k.append(kb)
            new_v.append(vb)
        else:
            o, hT, cb = rglru_mixer(hn, lru_h[j], lru_conv[j], p['lru_w_in'][j], p['lru_w_conv'][j],
                                    p['lru_b_conv'][j], p['lru_w_gate_a'][j], p['lru_b_gate_a'][j],
                                    p['lru_w_gate_x'][j], p['lru_b_gate_x'][j], p['lru_lambda'][j],
                                    p['lru_w_out'][j])
            new_h.append(hT)
            new_conv.append(cb)
        x = x + o
        x = x + cross_attn(rms_norm(x, p['ln_cross'][layer]), mem_k[layer], mem_v[layer],
                           p['cross_w_q'][layer], p['cross_w_o'][layer])
        x = x + 0.5 * swiglu(rms_norm(x, p['ln_ffn2'][layer]), p['ffn2_w_in'][layer], p['ffn2_w_out'][layer])
    y = rms_norm(x, p['ln_final'])
    return y, jnp.stack(new_k), jnp.stack(new_v), jnp.stack(new_h), jnp.stack(new_conv)


def setup_inputs(seed: int = 0) -> dict:
    key = jax.random.key(seed)
    ks = iter(jax.random.split(key, 48))
    f32 = jnp.float32

    def nrm(shape, scale=1.0):
        return jax.random.normal(next(ks), shape, f32) * scale

    def gain(shape):
        return 1.0 + 0.02 * jax.random.normal(next(ks), shape, f32)

    qkv_width = (N_HEADS + 2 * N_KV_HEADS) * HEAD_DIM
    pa = jax.random.uniform(next(ks), (N_LRU_LAYERS, LRU_WIDTH), f32, 0.9, 0.999) ** (1.0 / LRU_C)
    lam = jnp.log(pa) - jnp.log1p(-pa)
    return {
        'x_prompt': nrm((BATCH, SEQ, D_MODEL)),
        'x_sample': nrm((DEC_BATCH, DEC_SEQ, D_MODEL)),
        'cache_swa_k': nrm((N_SWA_LAYERS, DEC_BATCH, WINDOW, N_KV_HEADS, HEAD_DIM)),
        'cache_swa_v': nrm((N_SWA_LAYERS, DEC_BATCH, WINDOW, N_KV_HEADS, HEAD_DIM)),
        'state_lru_h': nrm((N_LRU_LAYERS, DEC_BATCH, LRU_WIDTH), 0.5),
        'state_lru_conv': nrm((N_LRU_LAYERS, DEC_BATCH, CONV_WIDTH - 1, LRU_WIDTH)),
        'cache_mem_k': nrm((DEPTH, DEC_BATCH, N_MEM, N_MEM_HEADS, MEM_HEAD_DIM)),
        'cache_mem_v': nrm((DEPTH, DEC_BATCH, N_MEM, N_MEM_HEADS, MEM_HEAD_DIM)),
        'mem_prompt': nrm((BATCH, N_MEM, D_MODEL)),
        'ln_ffn1': gain((DEPTH, D_MODEL)),
        'ffn1_w_in': nrm((DEPTH, D_MODEL, 2 * D_FF), D_MODEL ** -0.5),
        'ffn1_w_out': nrm((DEPTH, D_FF, D_MODEL), D_FF ** -0.5),
        'ln_mix': gain((DEPTH, D_MODEL)),
        'swa_w_qkv': nrm((N_SWA_LAYERS, D_MODEL, qkv_width), D_MODEL ** -0.5),
        'swa_w_o': nrm((N_SWA_LAYERS, N_HEADS * HEAD_DIM, D_MODEL), (N_HEADS * HEAD_DIM) ** -0.5),
        'swa_sinks': nrm((N_SWA_LAYERS, N_HEADS), 0.5),
        'lru_w_in': nrm((N_LRU_LAYERS, D_MODEL, 2 * LRU_WIDTH), D_MODEL ** -0.5),
        'lru_w_conv': nrm((N_LRU_LAYERS, CONV_WIDTH, LRU_WIDTH), CONV_WIDTH ** -0.5),
        'lru_b_conv': nrm((N_LRU_LAYERS, LRU_WIDTH), 0.02),
        'lru_w_gate_a': nrm((N_LRU_LAYERS, LRU_BLOCKS, LRU_BLOCK_SIZE, LRU_BLOCK_SIZE), LRU_BLOCK_SIZE ** -0.5),
        'lru_b_gate_a': nrm((N_LRU_LAYERS, LRU_WIDTH), 0.02),
        'lru_w_gate_x': nrm((N_LRU_LAYERS, LRU_BLOCKS, LRU_BLOCK_SIZE, LRU_BLOCK_SIZE), LRU_BLOCK_SIZE ** -0.5),
        'lru_b_gate_x': nrm((N_LRU_LAYERS, LRU_WIDTH), 0.02),
        'lru_lambda': lam,
        'lru_w_out': nrm((N_LRU_LAYERS, LRU_WIDTH, D_MODEL), LRU_WIDTH ** -0.5),
        'ln_cross': gain((DEPTH, D_MODEL)),
        'ln_mem': gain((DEPTH, D_MODEL)),
        'cross_w_q': nrm((DEPTH, D_MODEL, MEM_WIDTH), D_MODEL ** -0.5),
        'cross_w_kv': nrm((DEPTH, D_MODEL, 2 * MEM_WIDTH), D_MODEL ** -0.5),
        'cross_w_o': nrm((DEPTH, MEM_WIDTH, D_MODEL), MEM_WIDTH ** -0.5),
        'ln_ffn2': gain((DEPTH, D_MODEL)),
        'ffn2_w_in': nrm((DEPTH, D_MODEL, 2 * D_FF), D_MODEL ** -0.5),
        'ffn2_w_out': nrm((DEPTH, D_FF, D_MODEL), D_FF ** -0.5),
        'ln_final': gain((D_MODEL,)),
    }


def reference(x_prompt, x_sample, cache_swa_k, cache_swa_v, state_lru_h, state_lru_conv,
              cache_mem_k, cache_mem_v, mem_prompt,
              ln_ffn1, ffn1_w_in, ffn1_w_out, ln_mix, swa_w_qkv, swa_w_o, swa_sinks,
              lru_w_in, lru_w_conv, lru_b_conv, lru_w_gate_a, lru_b_gate_a, lru_w_gate_x, lru_b_gate_x,
              lru_lambda, lru_w_out, ln_cross, ln_mem, cross_w_q, cross_w_kv, cross_w_o,
              ln_ffn2, ffn2_w_in, ffn2_w_out, ln_final):
    p = dict(ln_ffn1=ln_ffn1, ffn1_w_in=ffn1_w_in, ffn1_w_out=ffn1_w_out, ln_mix=ln_mix,
             swa_w_qkv=swa_w_qkv, swa_w_o=swa_w_o, swa_sinks=swa_sinks,
             lru_w_in=lru_w_in, lru_w_conv=lru_w_conv, lru_b_conv=lru_b_conv,
             lru_w_gate_a=lru_w_gate_a, lru_b_gate_a=lru_b_gate_a,
             lru_w_gate_x=lru_w_gate_x, lru_b_gate_x=lru_b_gate_x,
             lru_lambda=lru_lambda, lru_w_out=lru_w_out,
             ln_cross=ln_cross, cross_w_q=cross_w_q, cross_w_o=cross_w_o,
             ln_ffn2=ln_ffn2, ffn2_w_in=ffn2_w_in, ffn2_w_out=ffn2_w_out, ln_final=ln_final)
    mkv = [mem_kv(mem_prompt, ln_mem[layer], cross_w_kv[layer]) for layer in range(DEPTH)]
    mem_k_prompt = jnp.stack([kv[0] for kv in mkv])
    mem_v_prompt = jnp.stack([kv[1] for kv in mkv])
    h0 = jnp.zeros((N_LRU_LAYERS, BATCH, LRU_WIDTH), x_prompt.dtype)
    c0 = jnp.zeros((N_LRU_LAYERS, BATCH, CONV_WIDTH - 1, LRU_WIDTH), x_prompt.dtype)
    y_prompt, swa_k_prompt, swa_v_prompt, lru_h_prompt, lru_conv_prompt = run_trunk(
        x_prompt, 0, None, None, h0, c0, mem_k_prompt, mem_v_prompt, p)
    y_sample, swa_k_sample, swa_v_sample, lru_h_sample, lru_conv_sample = run_trunk(
        x_sample, PAST_LEN, cache_swa_k, cache_swa_v, state_lru_h, state_lru_conv,
        cache_mem_k, cache_mem_v, p)
    return (y_prompt, y_sample, swa_k_prompt, swa_v_prompt, swa_k_sample, swa_v_sample,
            lru_h_prompt, lru_conv_prompt, lru_h_sample, lru_conv_sample, mem_k_prompt, mem_v_prompt)
```

```python
import functools
import math

import jax
import jax.numpy as jnp
from jax import lax
from jax.experimental import pallas as pl
from jax.experimental.pallas import tpu as pltpu

F32 = jnp.float32
BF16 = jnp.bfloat16

D_MODEL = 2048
BATCH = 2
SEQ = 4096
DEPTH = 2
DEC_BATCH = 32
DEC_SEQ = 8
PAST_LEN = 16384
N_HEADS = 32
N_KV_HEADS = 8
HEAD_DIM = 64
GROUP = N_HEADS // N_KV_HEADS
WINDOW = 128
ROPE_THETA = 10000.0
LRU_WIDTH = D_MODEL
LRU_BLOCKS = 8
LRU_BLOCK_SIZE = LRU_WIDTH // LRU_BLOCKS
CONV_WIDTH = 4
LRU_C = 8.0
D_FF = 5632
N_MEM = 256
N_MEM_HEADS = 4
MEM_HEAD_DIM = 128
MEM_WIDTH = N_MEM_HEADS * MEM_HEAD_DIM
NORM_EPS = 1e-6
Q_WIDTH = N_HEADS * HEAD_DIM
KV_WIDTH = N_KV_HEADS * HEAD_DIM

V7X_LANES = 128
V7X_SUBLANES = 8
V7X_VMEM_BYTES = 64 * 1024 * 1024
V7X_VMEM_USABLE_BYTES = V7X_VMEM_BYTES - 6 * 1024 * 1024

N_PROMPT_ROWS = BATCH * SEQ
N_SAMPLE_ROWS = DEC_BATCH * DEC_SEQ


def _params(semantics, vmem_bytes):
    return pltpu.CompilerParams(
        dimension_semantics=semantics,
        vmem_limit_bytes=int(min(max(vmem_bytes, 32 * 1024 * 1024), V7X_VMEM_USABLE_BYTES)),
    )


def _nbytes(shape, dtype):
    return math.prod(shape) * jnp.dtype(dtype).itemsize


def _rms(x, gain):
    ms = jnp.mean(x * x, axis=-1, keepdims=True)
    return x * lax.rsqrt(ms + NORM_EPS) * gain


def _row_param(p):
    return p.reshape(p.shape[0], 1, p.shape[-1])


def _row_param_spec(layer, width):
    return pl.BlockSpec((None, 1, width), lambda *_: (layer, 0, 0))


def _ffn_kernel(x_ref, gain_ref, wg_ref, wu_ref, wo_ref, *rest, row_chunk, final_norm):
    if final_norm:
        gfin_ref, o_ref, h_scr = rest
    else:
        o_ref, h_scr = rest
    f = pl.program_id(1)
    n_chunks = x_ref.shape[0] // row_chunk

    def rows_of(c):
        return pl.ds(pl.multiple_of(c * row_chunk, row_chunk), row_chunk)

    @pl.when(f == 0)
    def _():
        def body(c, carry):
            rows = rows_of(c)
            h_scr[rows, :] = _rms(x_ref[rows, :], gain_ref[...]).astype(BF16)
            o_ref[rows, :] = jnp.zeros((row_chunk, D_MODEL), F32)
            return carry

        lax.fori_loop(0, n_chunks, body, 0)

    def body(c, carry):
        rows = rows_of(c)
        h = h_scr[rows, :]
        g = jnp.dot(h, wg_ref[...], preferred_element_type=F32)
        u = jnp.dot(h, wu_ref[...], preferred_element_type=F32)
        a = (g * jax.nn.sigmoid(g) * u).astype(BF16)
        o_ref[rows, :] += jnp.dot(a, wo_ref[...], preferred_element_type=F32)
        return carry

    lax.fori_loop(0, n_chunks, body, 0)

    @pl.when(f == pl.num_programs(1) - 1)
    def _():
        def body(c, carry):
            rows = rows_of(c)
            y = x_ref[rows, :] + 0.5 * o_ref[rows, :]
            if final_norm:
                y = _rms(y, gfin_ref[...])
            o_ref[rows, :] = y
            return carry

        lax.fori_loop(0, n_chunks, body, 0)


def _ffn(x, gain, w_in, w_out, layer, *, tm, tf, row_chunk, final_gain=None):
    m = x.shape[0]
    nf = D_FF // tf
    final_norm = final_gain is not None
    in_specs = [
        pl.BlockSpec((tm, D_MODEL), lambda i, f: (i, 0)),
        _row_param_spec(layer, D_MODEL),
        pl.BlockSpec((None, D_MODEL, tf), lambda i, f: (layer, 0, f)),
        pl.BlockSpec((None, D_MODEL, tf), lambda i, f: (layer, 0, f + nf)),
        pl.BlockSpec((None, tf, D_MODEL), lambda i, f: (layer, f, 0)),
    ]
    args = [x, _row_param(gain), w_in, w_in, w_out]
    if final_norm:
        in_specs.append(_row_param_spec(0, D_MODEL))
        args.append(final_gain.reshape(1, 1, D_MODEL))
    vmem = (
        4 * _nbytes((tm, D_MODEL), F32)
        + _nbytes((tm, D_MODEL), BF16)
        + 2 * 3 * _nbytes((D_MODEL, tf), BF16)
        + 4 * _nbytes((row_chunk, tf), F32)
        + 2 * _nbytes((row_chunk, D_MODEL), F32)
    )
    return pl.pallas_call(
        functools.partial(_ffn_kernel, row_chunk=row_chunk, final_norm=final_norm),
        grid=(m // tm, nf),
        in_specs=in_specs,
        out_specs=pl.BlockSpec((tm, D_MODEL), lambda i, f: (i, 0)),
        out_shape=jax.ShapeDtypeStruct((m, D_MODEL), F32),
        scratch_shapes=[pltpu.VMEM((tm, D_MODEL), BF16)],
        compiler_params=_params(("parallel", "arbitrary"), vmem),
        name="ffn",
    )(*args)


def _gelu_tanh(y):
    return 0.5 * y * (1.0 + jnp.tanh(math.sqrt(2.0 / math.pi) * (y + 0.044715 * (y * y * y))))


def _norm_linear_kernel(x_ref, gain_ref, w_ref, o_ref, h_scr, *, gelu_from):
    j = pl.program_id(1)

    @pl.when(j == 0)
    def _():
        h_scr[...] = _rms(x_ref[...], gain_ref[...]).astype(BF16)

    y = jnp.dot(h_scr[...], w_ref[...], preferred_element_type=F32)
    if gelu_from is None:
        o_ref[...] = y.astype(o_ref.dtype)
    else:

        @pl.when(j < gelu_from)
        def _():
            o_ref[...] = y.astype(o_ref.dtype)

        @pl.when(j >= gelu_from)
        def _():
            o_ref[...] = _gelu_tanh(y).astype(o_ref.dtype)


def _norm_linear(x, gain, gain_layer, w, w_layer, *, tm, tn, out_dtype, gelu_from=None):
    m = x.shape[0]
    n = w.shape[-1]
    vmem = (
        2 * _nbytes((tm, D_MODEL), F32)
        + _nbytes((tm, D_MODEL), BF16)
        + 2 * _nbytes((D_MODEL, tn), BF16)
        + 4 * _nbytes((tm, tn), F32)
    )
    return pl.pallas_call(
        functools.partial(_norm_linear_kernel, gelu_from=gelu_from),
        grid=(m // tm, n // tn),
        in_specs=[
            pl.BlockSpec((tm, D_MODEL), lambda i, j: (i, 0)),
            _row_param_spec(gain_layer, D_MODEL),
            pl.BlockSpec((None, D_MODEL, tn), lambda i, j: (w_layer, 0, j)),
        ],
        out_specs=pl.BlockSpec((tm, tn), lambda i, j: (i, j)),
        out_shape=jax.ShapeDtypeStruct((m, n), out_dtype),
        scratch_shapes=[pltpu.VMEM((tm, D_MODEL), BF16)],
        compiler_params=_params(("parallel", "arbitrary"), vmem),
        name="norm_linear",
    )(x, _row_param(gain), w)


_QKV_TN = 1024


def _rope_tables(positions):
    half = HEAD_DIM // 2
    inv = ROPE_THETA ** (-jnp.arange(half, dtype=F32) * (2.0 / HEAD_DIM))
    ang = positions.astype(F32)[:, None] * inv[None, :]
    cos = jnp.cos(ang)
    sin = jnp.sin(ang)
    reps = V7X_LANES // HEAD_DIM
    cos_t = jnp.tile(jnp.concatenate([cos, cos], axis=-1), (1, reps))
    sin_t = jnp.tile(jnp.concatenate([-sin, sin], axis=-1), (1, reps))
    return cos_t, sin_t


def _qkv_kernel(x_ref, gain_ref, w_ref, cos_ref, sin_ref, q_ref, kv_ref, h_scr):
    j = pl.program_id(1)
    tm = x_ref.shape[0]

    @pl.when(j == 0)
    def _():
        h_scr[...] = _rms(x_ref[...], gain_ref[...]).astype(BF16)

    y = jnp.dot(h_scr[...], w_ref[...], preferred_element_type=F32)
    cos = cos_ref[...]
    sin = sin_ref[...]
    lane = lax.broadcasted_iota(jnp.int32, (tm, V7X_LANES), 1)
    first_half = (lane % HEAD_DIM) < (HEAD_DIM // 2)

    def rope(yc):
        rot = jnp.where(
            first_half,
            pltpu.roll(yc, V7X_LANES - HEAD_DIM // 2, 1),
            pltpu.roll(yc, HEAD_DIM // 2, 1),
        )
        return yc * cos + rot * sin

    def chunk(c):
        return slice(c * V7X_LANES, (c + 1) * V7X_LANES)

    @pl.when(j < Q_WIDTH // _QKV_TN)
    def _():
        for c in range(_QKV_TN // V7X_LANES):
            q_ref[:, chunk(c)] = rope(y[:, chunk(c)]).astype(q_ref.dtype)

    @pl.when(j == Q_WIDTH // _QKV_TN)
    def _():
        for c in range(KV_WIDTH // V7X_LANES):
            kv_ref[:, chunk(c)] = rope(y[:, chunk(c)])
        kv_ref[:, KV_WIDTH:] = y[:, KV_WIDTH:]


def _qkv(x, gain, gain_layer, w, w_layer, cos_t, sin_t, *, tm, q_dtype):
    m = x.shape[0]
    n_tab = cos_t.shape[0] // tm
    n_q_blocks = Q_WIDTH // _QKV_TN
    vmem = (
        2 * _nbytes((tm, D_MODEL), F32)
        + _nbytes((tm, D_MODEL), BF16)
        + 2 * _nbytes((D_MODEL, _QKV_TN), BF16)
        + 6 * _nbytes((tm, _QKV_TN), F32)
    )
    return pl.pallas_call(
        _qkv_kernel,
        grid=(m // tm, n_q_blocks + 1),
        in_specs=[
            pl.BlockSpec((tm, D_MODEL), lambda i, j: (i, 0)),
            _row_param_spec(gain_layer, D_MODEL),
            pl.BlockSpec((None, D_MODEL, _QKV_TN), lambda i, j: (w_layer, 0, j)),
            pl.BlockSpec((tm, V7X_LANES), lambda i, j: (i % n_tab, 0)),
            pl.BlockSpec((tm, V7X_LANES), lambda i, j: (i % n_tab, 0)),
        ],
        out_specs=[
            pl.BlockSpec((tm, _QKV_TN), lambda i, j: (i, jnp.minimum(j, n_q_blocks - 1))),
            pl.BlockSpec((tm, 2 * KV_WIDTH), lambda i, j: (i, 0)),
        ],
        out_shape=[
            jax.ShapeDtypeStruct((m, Q_WIDTH), q_dtype),
            jax.ShapeDtypeStruct((m, 2 * KV_WIDTH), F32),
        ],
        scratch_shapes=[pltpu.VMEM((tm, D_MODEL), BF16)],
        compiler_params=_params(("parallel", "arbitrary"), vmem),
        name="qkv_rope",
    )(x, _row_param(gain), w, cos_t, sin_t)


def _linear_residual_kernel(o_ref, w_ref, x_ref, out_ref):
    out_ref[...] = x_ref[...] + jnp.dot(o_ref[...].astype(BF16), w_ref[...], preferred_element_type=F32)


def _linear_residual(o, w, layer, x, *, tm):
    m, k = o.shape
    vmem = (
        2 * _nbytes((tm, k), o.dtype)
        + 2 * _nbytes((k, D_MODEL), BF16)
        + 6 * _nbytes((tm, D_MODEL), F32)
    )
    return pl.pallas_call(
        _linear_residual_kernel,
        grid=(m // tm,),
        in_specs=[
            pl.BlockSpec((tm, k), lambda i: (i, 0)),
            pl.BlockSpec((None, k, D_MODEL), lambda i: (layer, 0, 0)),
            pl.BlockSpec((tm, D_MODEL), lambda i: (i, 0)),
        ],
        out_specs=pl.BlockSpec((tm, D_MODEL), lambda i: (i, 0)),
        out_shape=jax.ShapeDtypeStruct((m, D_MODEL), F32),
        compiler_params=_params(("parallel",), vmem),
        name="linear_residual",
    )(o, w, x)


def _sink_softmax_pv(s, sink, v_bf16):
    m = jnp.maximum(jnp.max(s, axis=-1, keepdims=True), sink)
    e = jnp.exp(s - m)
    denom = jnp.sum(e, axis=-1, keepdims=True) + jnp.exp(sink - m)
    p = (e / denom).astype(BF16)
    return jnp.dot(p, v_bf16, preferred_element_type=F32)


def _swa_prompt_kernel(sinks_ref, q_ref, kv_prev_ref, kv_cur_ref, o_ref):
    n = pl.program_id(1)
    blk = WINDOW
    qi = lax.broadcasted_iota(jnp.int32, (blk, 2 * blk), 0)
    kj = lax.broadcasted_iota(jnp.int32, (blk, 2 * blk), 1)
    d = qi + blk - kj
    mask = (d >= 0) & (d < WINDOW) & ((kj >= blk) | (n > 0))
    scale = HEAD_DIM ** -0.5
    for kvh in range(N_KV_HEADS):
        ks = slice(kvh * HEAD_DIM, (kvh + 1) * HEAD_DIM)
        vs = slice(KV_WIDTH + kvh * HEAD_DIM, KV_WIDTH + (kvh + 1) * HEAD_DIM)
        k = jnp.concatenate([kv_prev_ref[:, ks], kv_cur_ref[:, ks]], axis=0).astype(BF16)
        v = jnp.concatenate([kv_prev_ref[:, vs], kv_cur_ref[:, vs]], axis=0).astype(BF16)
        for g in range(GROUP):
            h = kvh * GROUP + g
            hs = slice(h * HEAD_DIM, (h + 1) * HEAD_DIM)
            q = q_ref[:, hs]
            s = lax.dot_general(q, k, (((1,), (1,)), ((), ())), preferred_element_type=F32) * scale
            s = jnp.where(mask, s, -jnp.inf)
            o_ref[:, hs] = _sink_softmax_pv(s, sinks_ref[h], v).astype(o_ref.dtype)


def _swa_prompt(q, kv, sinks):
    nb = SEQ // WINDOW
    vmem = 2 * _nbytes((WINDOW, Q_WIDTH), BF16) * 2 + 4 * _nbytes((WINDOW, 2 * KV_WIDTH), F32) + 16 * 1024 * 1024
    return pl.pallas_call(
        _swa_prompt_kernel,
        grid=(BATCH, nb),
        in_specs=[
            pl.BlockSpec(memory_space=pltpu.SMEM),
            pl.BlockSpec((WINDOW, Q_WIDTH), lambda b, n: (b * nb + n, 0)),
            pl.BlockSpec((WINDOW, 2 * KV_WIDTH), lambda b, n: (b * nb + jnp.maximum(n - 1, 0), 0)),
            pl.BlockSpec((WINDOW, 2 * KV_WIDTH), lambda b, n: (b * nb + n, 0)),
        ],
        out_specs=pl.BlockSpec((WINDOW, Q_WIDTH), lambda b, n: (b * nb + n, 0)),
        out_shape=jax.ShapeDtypeStruct((BATCH * SEQ, Q_WIDTH), BF16),
        compiler_params=_params(("parallel", "arbitrary"), vmem),
        name="swa_prompt",
    )(sinks, q, kv, kv)


def _swa_sample_kernel(sinks_ref, q_ref, kv_ref, ck_ref, cv_ref, o_ref, nk_ref, nv_ref):
    t_new = DEC_SEQ
    s_len = WINDOW + t_new
    qi = lax.broadcasted_iota(jnp.int32, (t_new, s_len), 0)
    kj = lax.broadcasted_iota(jnp.int32, (t_new, s_len), 1)
    d = qi + WINDOW - kj
    mask = (d >= 0) & (d < WINDOW)
    scale = HEAD_DIM ** -0.5
    k_all = jnp.concatenate([ck_ref[...], kv_ref[:, :KV_WIDTH]], axis=0)
    v_all = jnp.concatenate([cv_ref[...], kv_ref[:, KV_WIDTH:]], axis=0)
    nk_ref[...] = k_all[t_new:, :]
    nv_ref[...] = v_all[t_new:, :]
    k_bf = k_all.astype(BF16)
    v_bf = v_all.astype(BF16)
    for kvh in range(N_KV_HEADS):
        ks = slice(kvh * HEAD_DIM, (kvh + 1) * HEAD_DIM)
        k = k_bf[:, ks]
        v = v_bf[:, ks]
        for g in range(GROUP):
            h = kvh * GROUP + g
            hs = slice(h * HEAD_DIM, (h + 1) * HEAD_DIM)
            q = q_ref[:, hs].astype(BF16)
            s = lax.dot_general(q, k, (((1,), (1,)), ((), ())), preferred_element_type=F32) * scale
            s = jnp.where(mask, s, -jnp.inf)
            o_ref[:, hs] = _sink_softmax_pv(s, sinks_ref[h], v)


def _swa_sample(q, kv, cache_k, cache_v, sinks):
    cache_spec = pl.BlockSpec((None, WINDOW, KV_WIDTH), lambda b: (b, 0, 0))
    return pl.pallas_call(
        _swa_sample_kernel,
        grid=(DEC_BATCH,),
        in_specs=[
            pl.BlockSpec(memory_space=pltpu.SMEM),
            pl.BlockSpec((DEC_SEQ, Q_WIDTH), lambda b: (b, 0)),
            pl.BlockSpec((DEC_SEQ, 2 * KV_WIDTH), lambda b: (b, 0)),
            cache_spec,
            cache_spec,
        ],
        out_specs=[pl.BlockSpec((DEC_SEQ, Q_WIDTH), lambda b: (b, 0)), cache_spec, cache_spec],
        out_shape=[
            jax.ShapeDtypeStruct((N_SAMPLE_ROWS, Q_WIDTH), F32),
            jax.ShapeDtypeStruct((DEC_BATCH, WINDOW, KV_WIDTH), F32),
            jax.ShapeDtypeStruct((DEC_BATCH, WINDOW, KV_WIDTH), F32),
        ],
        compiler_params=_params(("parallel",), 32 * 1024 * 1024),
        name="swa_sample",
    )(sinks, q, kv, cache_k, cache_v)


def _cross_attn_kernel(q_ref, mk_ref, mv_ref, o_ref):
    scale = MEM_HEAD_DIM ** -0.5
    for h in range(N_MEM_HEADS):
        hs = slice(h * MEM_HEAD_DIM, (h + 1) * MEM_HEAD_DIM)
        q = q_ref[:, hs].astype(BF16)
        k = mk_ref[:, hs].astype(BF16)
        v = mv_ref[:, hs].astype(BF16)
        s = lax.dot_general(q, k, (((1,), (1,)), ((), ())), preferred_element_type=F32) * scale
        m = jnp.max(s, axis=-1, keepdims=True)
        e = jnp.exp(s - m)
        p = (e / jnp.sum(e, axis=-1, keepdims=True)).astype(BF16)
        o_ref[:, hs] = jnp.dot(p, v, preferred_element_type=F32).astype(o_ref.dtype)


def _cross_attn(q, mem_k, mem_v, layer, *, n_batch, tq, out_dtype):
    rows = q.shape[0] // n_batch
    nt = rows // tq
    mem_spec = pl.BlockSpec((None, None, N_MEM, MEM_WIDTH), lambda b, i: (layer, b, 0, 0))
    vmem = 4 * _nbytes((N_MEM, MEM_WIDTH), F32) + 16 * _nbytes((tq, MEM_WIDTH), F32) + 8 * 1024 * 1024
    return pl.pallas_call(
        _cross_attn_kernel,
        grid=(n_batch, nt),
        in_specs=[pl.BlockSpec((tq, MEM_WIDTH), lambda b, i: (b * nt + i, 0)), mem_spec, mem_spec],
        out_specs=pl.BlockSpec((tq, MEM_WIDTH), lambda b, i: (b * nt + i, 0)),
        out_shape=jax.ShapeDtypeStruct(q.shape, out_dtype),
        compiler_params=_params(("parallel", "arbitrary"), vmem),
        name="cross_attn",
    )(q, mem_k, mem_v)


def _log_sigmoid(x):
    return jnp.minimum(x, 0.0) - jnp.log1p(jnp.exp(-jnp.abs(x)))


def _neg_expm1(x):
    series = x * (1.0 + x / 2.0 * (1.0 + x / 3.0 * (1.0 + x / 4.0 * (1.0 + x / 5.0 * (1.0 + x / 6.0 * (
        1.0 + x / 7.0 * (1.0 + x / 8.0 * (1.0 + x / 9.0))))))))
    return -jnp.where(x > -0.25, series, jnp.exp(x) - 1.0)


def _lru_gates(xc, wa_ref, ba_ref, wx_ref, bx_ref, lam_ref):
    r_parts = []
    i_parts = []
    for n in range(LRU_BLOCKS):
        cs = slice(n * LRU_BLOCK_SIZE, (n + 1) * LRU_BLOCK_SIZE)
        xb = xc[:, cs].astype(BF16)
        r_parts.append(jnp.dot(xb, wa_ref[n], preferred_element_type=F32))
        i_parts.append(jnp.dot(xb, wx_ref[n], preferred_element_type=F32))
    r = jax.nn.sigmoid(jnp.concatenate(r_parts, axis=-1) + ba_ref[...])
    ig = jax.nn.sigmoid(jnp.concatenate(i_parts, axis=-1) + bx_ref[...])
    log_a = LRU_C * r * _log_sigmoid(lam_ref[...])
    a = jnp.exp(log_a)
    u = jnp.sqrt(_neg_expm1(2.0 * log_a)) * (ig * xc)
    return a, u


def _lru_prompt_kernel(xb_ref, gate_ref, x_ref, wc_ref, bc_ref, wa_ref, ba_ref, wx_ref, bx_ref, lam_ref,
                       wo_ref, out_ref, hlast_ref, xext_scr, a_scr, u_scr, h_scr):
    i = pl.program_id(1)
    tm = xb_ref.shape[0]
    pad = V7X_SUBLANES

    @pl.when(i == 0)
    def _():
        xext_scr[0:pad, :] = jnp.zeros((pad, LRU_WIDTH), F32)
        h_scr[...] = jnp.zeros((1, LRU_WIDTH), F32)

    xext_scr[pad:pad + tm, :] = xb_ref[...]
    xc = bc_ref[...] + wc_ref[CONV_WIDTH - 1:CONV_WIDTH, :] * xb_ref[...]
    for k in range(1, CONV_WIDTH):
        xc = xc + wc_ref[CONV_WIDTH - 1 - k:CONV_WIDTH - k, :] * xext_scr[pad - k:pad - k + tm, :]
    xext_scr[0:pad, :] = xext_scr[tm:tm + pad, :]

    a, u = _lru_gates(xc, wa_ref, ba_ref, wx_ref, bx_ref, lam_ref)
    a_scr[...] = a
    u_scr[...] = u

    def step(t, h):
        h = a_scr[pl.ds(t, 1), :] * h + u_scr[pl.ds(t, 1), :]
        u_scr[pl.ds(t, 1), :] = h
        return h

    h_scr[...] = lax.fori_loop(0, tm, step, h_scr[...], unroll=8)
    hlast_ref[...] = u_scr[tm - pad:tm, :]
    y = (u_scr[...] * gate_ref[...]).astype(BF16)
    out_ref[...] = x_ref[...] + jnp.dot(y, wo_ref[...], preferred_element_type=F32)


def _lru_weight_specs(layer_idx):
    def const3(*_):
        return (layer_idx, 0, 0)

    def const4(*_):
        return (layer_idx, 0, 0, 0)

    gate_w = pl.BlockSpec((None, LRU_BLOCKS, LRU_BLOCK_SIZE, LRU_BLOCK_SIZE), const4)
    vec = _row_param_spec(layer_idx, LRU_WIDTH)
    return [
        pl.BlockSpec((None, CONV_WIDTH, LRU_WIDTH), const3),
        vec,
        gate_w,
        vec,
        gate_w,
        vec,
        vec,
        pl.BlockSpec((None, LRU_WIDTH, D_MODEL), const3),
    ]


def _lru_weight_args(w_conv, b_conv, w_gate_a, b_gate_a, w_gate_x, b_gate_x, lam, w_out):
    return (w_conv, _row_param(b_conv), w_gate_a, _row_param(b_gate_a), w_gate_x, _row_param(b_gate_x),
            _row_param(lam), w_out)


def _lru_prompt(xy, x, lw, j, *, tm):
    nt = SEQ // tm
    row_spec0 = pl.BlockSpec((tm, LRU_WIDTH), lambda b, i: (b * nt + i, 0))
    row_spec1 = pl.BlockSpec((tm, LRU_WIDTH), lambda b, i: (b * nt + i, 1))
    vmem = (
        8 * _nbytes((tm, LRU_WIDTH), F32)
        + 3 * _nbytes((tm + V7X_SUBLANES, LRU_WIDTH), F32)
        + 2 * _nbytes((LRU_WIDTH, D_MODEL), BF16)
        + 4 * _nbytes((LRU_BLOCKS, LRU_BLOCK_SIZE, LRU_BLOCK_SIZE), BF16)
        + 10 * _nbytes((tm, LRU_WIDTH), F32)
    )
    return pl.pallas_call(
        _lru_prompt_kernel,
        grid=(BATCH, nt),
        in_specs=[row_spec0, row_spec1, row_spec0] + _lru_weight_specs(j),
        out_specs=[
            row_spec0,
            pl.BlockSpec((None, V7X_SUBLANES, LRU_WIDTH), lambda b, i: (b, 0, 0)),
        ],
        out_shape=[
            jax.ShapeDtypeStruct((BATCH * SEQ, D_MODEL), F32),
            jax.ShapeDtypeStruct((BATCH, V7X_SUBLANES, LRU_WIDTH), F32),
        ],
        scratch_shapes=[
            pltpu.VMEM((tm + V7X_SUBLANES, LRU_WIDTH), F32),
            pltpu.VMEM((tm, LRU_WIDTH), F32),
            pltpu.VMEM((tm, LRU_WIDTH), F32),
            pltpu.VMEM((1, LRU_WIDTH), F32),
        ],
        compiler_params=_params(("arbitrary", "arbitrary"), vmem),
        name="lru_prompt",
    )(xy, xy, x, *lw)


def _lru_sample_kernel(xb_ref, gate_ref, x_ref, cbuf_ref, h0_ref, wc_ref, bc_ref, wa_ref, ba_ref, wx_ref, bx_ref,
                       lam_ref, wo_ref, out_ref, hs_ref):
    rows = xb_ref.shape[0]
    t = lax.broadcasted_iota(jnp.int32, (rows, LRU_WIDTH), 0) % DEC_SEQ
    xb = xb_ref[...]
    cbuf = cbuf_ref[...]
    xc = bc_ref[...] + wc_ref[CONV_WIDTH - 1:CONV_WIDTH, :] * xb
    for k in range(1, CONV_WIDTH):
        shift_c = CONV_WIDTH - 1 - k
        from_buf = cbuf if shift_c == 0 else pltpu.roll(cbuf, rows - shift_c, 0)
        xk = jnp.where(t >= k, pltpu.roll(xb, k, 0), from_buf)
        xc = xc + wc_ref[CONV_WIDTH - 1 - k:CONV_WIDTH - k, :] * xk

    a, u = _lru_gates(xc, wa_ref, ba_ref, wx_ref, bx_ref, lam_ref)
    k = 1
    while k < DEC_SEQ:
        m = t >= k
        u = jnp.where(m, a * pltpu.roll(u, k, 0) + u, u)
        a = jnp.where(m, a * pltpu.roll(a, k, 0), a)
        k *= 2
    hs = u + a * h0_ref[...]
    hs_ref[...] = hs
    y = (hs * gate_ref[...]).astype(BF16)
    out_ref[...] = x_ref[...] + jnp.dot(y, wo_ref[...], preferred_element_type=F32)


def _lru_sample(xy, x, cbuf_rows, h0_rows, lw, j):
    rows = N_SAMPLE_ROWS
    row_spec0 = pl.BlockSpec((rows, LRU_WIDTH), lambda i: (0, 0))
    row_spec1 = pl.BlockSpec((rows, LRU_WIDTH), lambda i: (0, 1))
    vmem = 40 * _nbytes((rows, LRU_WIDTH), F32) + 2 * _nbytes((LRU_WIDTH, D_MODEL), BF16)
    return pl.pallas_call(
        _lru_sample_kernel,
        grid=(1,),
        in_specs=[row_spec0, row_spec1, row_spec0, row_spec0, row_spec0] + _lru_weight_specs(j),
        out_specs=[row_spec0, row_spec0],
        out_shape=[
            jax.ShapeDtypeStruct((rows, D_MODEL), F32),
            jax.ShapeDtypeStruct((rows, LRU_WIDTH), F32),
        ],
        compiler_params=_params(("arbitrary",), vmem),
        name="lru_sample",
    )(xy, xy, x, cbuf_rows, h0_rows, *lw)


_PROMPT_TM = 1024
_SAMPLE_TM = N_SAMPLE_ROWS


def kernel(x_prompt, x_sample, cache_swa_k, cache_swa_v, state_lru_h, state_lru_conv, cache_mem_k, cache_mem_v,
           mem_prompt, ln_ffn1, ffn1_w_in, ffn1_w_out, ln_mix, swa_w_qkv, swa_w_o, swa_sinks, lru_w_in,
           lru_w_conv, lru_b_conv, lru_w_gate_a, lru_b_gate_a, lru_w_gate_x, lru_b_gate_x, lru_lambda,
           lru_w_out, ln_cross, ln_mem, cross_w_q, cross_w_kv, cross_w_o, ln_ffn2, ffn2_w_in, ffn2_w_out,
           ln_final):
    bf = lambda w: w.astype(BF16)
    ffn1_w_in, ffn1_w_out, ffn2_w_in, ffn2_w_out = bf(ffn1_w_in), bf(ffn1_w_out), bf(ffn2_w_in), bf(ffn2_w_out)
    swa_w_qkv, swa_w_o = bf(swa_w_qkv), bf(swa_w_o)
    lru_w_in, lru_w_out = bf(lru_w_in), bf(lru_w_out)
    lru_w_gate_a, lru_w_gate_x = bf(lru_w_gate_a), bf(lru_w_gate_x)
    cross_w_q, cross_w_kv, cross_w_o = bf(cross_w_q), bf(cross_w_kv), bf(cross_w_o)

    xp = x_prompt.reshape(N_PROMPT_ROWS, D_MODEL)
    xs = x_sample.reshape(N_SAMPLE_ROWS, D_MODEL)

    mem_rows = mem_prompt.reshape(BATCH * N_MEM, D_MODEL)
    mkv = [
        _norm_linear(mem_rows, ln_mem, layer, cross_w_kv, layer, tm=BATCH * N_MEM, tn=MEM_WIDTH, out_dtype=F32)
        for layer in range(DEPTH)
    ]
    mem_k_prompt = jnp.stack([m[:, :MEM_WIDTH] for m in mkv]).reshape(DEPTH, BATCH, N_MEM, MEM_WIDTH)
    mem_v_prompt = jnp.stack([m[:, MEM_WIDTH:] for m in mkv]).reshape(DEPTH, BATCH, N_MEM, MEM_WIDTH)
    mem_k_sample = cache_mem_k.reshape(DEPTH, DEC_BATCH, N_MEM, MEM_WIDTH)
    mem_v_sample = cache_mem_v.reshape(DEPTH, DEC_BATCH, N_MEM, MEM_WIDTH)

    cos_p, sin_p = _rope_tables(jnp.arange(SEQ, dtype=jnp.int32))
    cos_s, sin_s = _rope_tables(PAST_LEN + jnp.arange(N_SAMPLE_ROWS, dtype=jnp.int32) % DEC_SEQ)

    swa_k_p, swa_v_p, swa_k_s, swa_v_s = [], [], [], []
    lru_h_p, lru_conv_p, lru_h_s, lru_conv_s = [], [], [], []

    for layer in range(DEPTH):
        j = layer // 2
        xp = _ffn(xp, ln_ffn1, ffn1_w_in, ffn1_w_out, layer, tm=_PROMPT_TM, tf=512, row_chunk=512)
        xs = _ffn(xs, ln_ffn1, ffn1_w_in, ffn1_w_out, layer, tm=_SAMPLE_TM, tf=512, row_chunk=256)

        if layer % 2 == 0:
            q_p, kv_p = _qkv(xp, ln_mix, layer, swa_w_qkv, j, cos_p, sin_p, tm=512, q_dtype=BF16)
            o_p = _swa_prompt(q_p, kv_p, swa_sinks[j])
            xp = _linear_residual(o_p, swa_w_o, j, xp, tm=512)
            kv_tail = kv_p.reshape(BATCH, SEQ, 2 * KV_WIDTH)[:, SEQ - WINDOW:, :]
            swa_k_p.append(kv_tail[..., :KV_WIDTH].reshape(BATCH, WINDOW, N_KV_HEADS, HEAD_DIM))
            swa_v_p.append(kv_tail[..., KV_WIDTH:].reshape(BATCH, WINDOW, N_KV_HEADS, HEAD_DIM))

            q_s, kv_s = _qkv(xs, ln_mix, layer, swa_w_qkv, j, cos_s, sin_s, tm=N_SAMPLE_ROWS, q_dtype=F32)
            o_s, nk, nv = _swa_sample(
                q_s, kv_s,
                cache_swa_k[j].reshape(DEC_BATCH, WINDOW, KV_WIDTH),
                cache_swa_v[j].reshape(DEC_BATCH, WINDOW, KV_WIDTH),
                swa_sinks[j],
            )
            xs = _linear_residual(o_s, swa_w_o, j, xs, tm=N_SAMPLE_ROWS)
            swa_k_s.append(nk.reshape(DEC_BATCH, WINDOW, N_KV_HEADS, HEAD_DIM))
            swa_v_s.append(nv.reshape(DEC_BATCH, WINDOW, N_KV_HEADS, HEAD_DIM))
        else:
            lw = _lru_weight_args(lru_w_conv, lru_b_conv, lru_w_gate_a, lru_b_gate_a, lru_w_gate_x, lru_b_gate_x,
                                  lru_lambda, lru_w_out)
            n_cols = 2 * LRU_WIDTH // 1024
            xy_p = _norm_linear(xp, ln_mix, layer, lru_w_in, j, tm=512, tn=1024, out_dtype=F32,
                                gelu_from=n_cols // 2)
            xp, h_last = _lru_prompt(xy_p, xp, lw, j, tm=256)
            lru_h_p.append(h_last[:, V7X_SUBLANES - 1, :])
            lru_conv_p.append(
                xy_p.reshape(BATCH, SEQ, 2 * LRU_WIDTH)[:, SEQ - (CONV_WIDTH - 1):, :LRU_WIDTH])

            xy_s = _norm_linear(xs, ln_mix, layer, lru_w_in, j, tm=N_SAMPLE_ROWS, tn=1024, out_dtype=F32,
                                gelu_from=n_cols // 2)
            cbuf_rows = jnp.pad(
                state_lru_conv[j], ((0, 0), (0, DEC_SEQ - (CONV_WIDTH - 1)), (0, 0))
            ).reshape(N_SAMPLE_ROWS, LRU_WIDTH)
            h0_rows = jnp.repeat(state_lru_h[j], DEC_SEQ, axis=0)
            xs, hs_s = _lru_sample(xy_s, xs, cbuf_rows, h0_rows, lw, j)
            lru_h_s.append(hs_s.reshape(DEC_BATCH, DEC_SEQ, LRU_WIDTH)[:, DEC_SEQ - 1, :])
            xp_tail = jnp.concatenate(
                [state_lru_conv[j], xy_s.reshape(DEC_BATCH, DEC_SEQ, 2 * LRU_WIDTH)[:, :, :LRU_WIDTH]], axis=1)
            lru_conv_s.append(xp_tail[:, -(CONV_WIDTH - 1):, :])

        qc_p = _norm_linear(xp, ln_cross, layer, cross_w_q, layer, tm=1024, tn=MEM_WIDTH, out_dtype=BF16)
        oc_p = _cross_attn(qc_p, mem_k_prompt, mem_v_prompt, layer, n_batch=BATCH, tq=512, out_dtype=BF16)
        xp = _linear_residual(oc_p, cross_w_o, layer, xp, tm=1024)
        qc_s = _norm_linear(xs, ln_cross, layer, cross_w_q, layer, tm=N_SAMPLE_ROWS, tn=MEM_WIDTH, out_dtype=F32)
        oc_s = _cross_attn(qc_s, mem_k_sample, mem_v_sample, layer, n_batch=DEC_BATCH, tq=DEC_SEQ, out_dtype=F32)
        xs = _linear_residual(oc_s, cross_w_o, layer, xs, tm=N_SAMPLE_ROWS)

        fin = ln_final if layer == DEPTH - 1 else None
        xp = _ffn(xp, ln_ffn2, ffn2_w_in, ffn2_w_out, layer, tm=_PROMPT_TM, tf=512, row_chunk=512, final_gain=fin)
        xs = _ffn(xs, ln_ffn2, ffn2_w_in, ffn2_w_out, layer, tm=_SAMPLE_TM, tf=512, row_chunk=256, final_gain=fin)

    return (
        xp.reshape(BATCH, SEQ, D_MODEL),
        xs.reshape(DEC_BATCH, DEC_SEQ, D_MODEL),
        jnp.stack(swa_k_p), jnp.stack(swa_v_p), jnp.stack(swa_k_s), jnp.stack(swa_v_s),
        jnp.stack(lru_h_p), jnp.stack(lru_conv_p), jnp.stack(lru_h_s), jnp.stack(lru_conv_s),
        mem_k_prompt.reshape(DEPTH, BATCH, N_MEM, N_MEM_HEADS, MEM_HEAD_DIM),
        mem_v_prompt.reshape(DEPTH, BATCH, N_MEM, N_MEM_HEADS, MEM_HEAD_DIM),
    )
```

```python
import functools
import math

import jax
import jax.numpy as jnp
from jax import lax
from jax.experimental import pallas as pl
from jax.experimental.pallas import tpu as pltpu

F32 = jnp.float32
BF16 = jnp.bfloat16

D_MODEL = 2048
BATCH = 2
SEQ = 4096
DEPTH = 2
DEC_BATCH = 32
DEC_SEQ = 8
PAST_LEN = 16384
N_HEADS = 32
N_KV_HEADS = 8
HEAD_DIM = 64
GROUP = N_HEADS // N_KV_HEADS
WINDOW = 128
ROPE_THETA = 10000.0
LRU_WIDTH = D_MODEL
LRU_BLOCKS = 8
LRU_BLOCK_SIZE = LRU_WIDTH // LRU_BLOCKS
CONV_WIDTH = 4
LRU_C = 8.0
D_FF = 5632
N_MEM = 256
N_MEM_HEADS = 4
MEM_HEAD_DIM = 128
MEM_WIDTH = N_MEM_HEADS * MEM_HEAD_DIM
NORM_EPS = 1e-6
Q_WIDTH = N_HEADS * HEAD_DIM
KV_WIDTH = N_KV_HEADS * HEAD_DIM

V7X_LANES = 128
V7X_SUBLANES = 8
V7X_VMEM_BYTES = 64 * 1024 * 1024
V7X_VMEM_USABLE_BYTES = V7X_VMEM_BYTES - 6 * 1024 * 1024

N_PROMPT_ROWS = BATCH * SEQ
N_SAMPLE_ROWS = DEC_BATCH * DEC_SEQ


def _params(semantics, vmem_bytes):
    return pltpu.CompilerParams(
        dimension_semantics=semantics,
        vmem_limit_bytes=int(min(max(vmem_bytes, 32 * 1024 * 1024), V7X_VMEM_USABLE_BYTES)),
    )


def _nbytes(shape, dtype):
    return math.prod(shape) * jnp.dtype(dtype).itemsize


def _rms(x, gain):
    ms = jnp.mean(x * x, axis=-1, keepdims=True)
    return x * lax.rsqrt(ms + NORM_EPS) * gain


def _row_param(p):
    return p.reshape(p.shape[0], 1, p.shape[-1])


def _row_param_spec(layer, width):
    return pl.BlockSpec((None, 1, width), lambda *_: (layer, 0, 0))


def _for_chunks(n_rows, chunk, body):
    n_chunks = n_rows // chunk
    if n_chunks == 1:
        body(pl.ds(0, n_rows))
        return

    def step(c, carry):
        body(pl.ds(pl.multiple_of(c * chunk, chunk), chunk))
        return carry

    lax.fori_loop(0, n_chunks, step, 0)


def _ffn_kernel(x_ref, gain_ref, wg_ref, wu_ref, wo_ref, *rest, row_chunk, final_norm):
    gfin_ref = rest[0] if final_norm else None
    xe_ref, o_ref, oe_ref, h_scr, he_scr = rest[1:] if final_norm else rest
    f = pl.program_id(1)

    def run(x_ref, o_ref, h_scr, chunk):
        n_rows = x_ref.shape[0]

        @pl.when(f == 0)
        def _():
            def prologue(rows):
                h_scr[rows, :] = _rms(x_ref[rows, :], gain_ref[...]).astype(BF16)
                o_ref[rows, :] = jnp.zeros((chunk, D_MODEL), F32)

            _for_chunks(n_rows, chunk, prologue)

        def accumulate(rows):
            h = h_scr[rows, :]
            g = jnp.dot(h, wg_ref[...], preferred_element_type=F32)
            u = jnp.dot(h, wu_ref[...], preferred_element_type=F32)
            a = (g * jax.nn.sigmoid(g) * u).astype(BF16)
            o_ref[rows, :] += jnp.dot(a, wo_ref[...], preferred_element_type=F32)

        _for_chunks(n_rows, chunk, accumulate)

        @pl.when(f == pl.num_programs(1) - 1)
        def _():
            def epilogue(rows):
                y = x_ref[rows, :] + 0.5 * o_ref[rows, :]
                if final_norm:
                    y = _rms(y, gfin_ref[...])
                o_ref[rows, :] = y

            _for_chunks(n_rows, chunk, epilogue)

    run(x_ref, o_ref, h_scr, row_chunk)

    @pl.when(pl.program_id(0) == 0)
    def _():
        run(xe_ref, oe_ref, he_scr, xe_ref.shape[0])


def _ffn(x, x_extra, gain, w_in, w_out, layer, *, tm, tf, row_chunk, final_gain=None):
    m = x.shape[0]
    me = x_extra.shape[0]
    nf = D_FF // tf
    final_norm = final_gain is not None
    row_spec = pl.BlockSpec((tm, D_MODEL), lambda i, f: (i, 0))
    extra_spec = pl.BlockSpec((me, D_MODEL), lambda i, f: (0, 0))
    in_specs = [
        row_spec,
        _row_param_spec(layer, D_MODEL),
        pl.BlockSpec((None, D_MODEL, tf), lambda i, f: (layer, 0, f)),
        pl.BlockSpec((None, D_MODEL, tf), lambda i, f: (layer, 0, f + nf)),
        pl.BlockSpec((None, tf, D_MODEL), lambda i, f: (layer, f, 0)),
    ]
    args = [x, _row_param(gain), w_in, w_in, w_out]
    if final_norm:
        in_specs.append(_row_param_spec(0, D_MODEL))
        args.append(final_gain.reshape(1, 1, D_MODEL))
    in_specs.append(extra_spec)
    args.append(x_extra)
    vmem = (
        4 * _nbytes((tm, D_MODEL), F32)
        + _nbytes((tm, D_MODEL), BF16)
        + 4 * _nbytes((me, D_MODEL), F32)
        + _nbytes((me, D_MODEL), BF16)
        + 2 * 3 * _nbytes((D_MODEL, tf), BF16)
        + 4 * _nbytes((row_chunk, tf), F32)
        + 2 * _nbytes((row_chunk, D_MODEL), F32)
    )
    return pl.pallas_call(
        functools.partial(_ffn_kernel, row_chunk=row_chunk, final_norm=final_norm),
        grid=(m // tm, nf),
        in_specs=in_specs,
        out_specs=[row_spec, extra_spec],
        out_shape=[jax.ShapeDtypeStruct((m, D_MODEL), F32), jax.ShapeDtypeStruct((me, D_MODEL), F32)],
        scratch_shapes=[pltpu.VMEM((tm, D_MODEL), BF16), pltpu.VMEM((me, D_MODEL), BF16)],
        compiler_params=_params(("arbitrary", "arbitrary"), vmem),
        name="ffn",
    )(*args)


def _gelu_tanh(y):
    return 0.5 * y * (1.0 + jnp.tanh(math.sqrt(2.0 / math.pi) * (y + 0.044715 * (y * y * y))))


def _norm_linear_kernel(x_ref, gain_ref, w_ref, o_ref, h_scr, *, gelu_from):
    j = pl.program_id(1)

    @pl.when(j == 0)
    def _():
        h_scr[...] = _rms(x_ref[...], gain_ref[...]).astype(BF16)

    y = jnp.dot(h_scr[...], w_ref[...], preferred_element_type=F32)
    if gelu_from is None:
        o_ref[...] = y.astype(o_ref.dtype)
    else:

        @pl.when(j < gelu_from)
        def _():
            o_ref[...] = y.astype(o_ref.dtype)

        @pl.when(j >= gelu_from)
        def _():
            o_ref[...] = _gelu_tanh(y).astype(o_ref.dtype)


def _norm_linear(x, gain, gain_layer, w, w_layer, *, tm, tn, out_dtype, gelu_from=None):
    m = x.shape[0]
    n = w.shape[-1]
    vmem = (
        2 * _nbytes((tm, D_MODEL), F32)
        + _nbytes((tm, D_MODEL), BF16)
        + 2 * _nbytes((D_MODEL, tn), BF16)
        + 4 * _nbytes((tm, tn), F32)
    )
    return pl.pallas_call(
        functools.partial(_norm_linear_kernel, gelu_from=gelu_from),
        grid=(m // tm, n // tn),
        in_specs=[
            pl.BlockSpec((tm, D_MODEL), lambda i, j: (i, 0)),
            _row_param_spec(gain_layer, D_MODEL),
            pl.BlockSpec((None, D_MODEL, tn), lambda i, j: (w_layer, 0, j)),
        ],
        out_specs=pl.BlockSpec((tm, tn), lambda i, j: (i, j)),
        out_shape=jax.ShapeDtypeStruct((m, n), out_dtype),
        scratch_shapes=[pltpu.VMEM((tm, D_MODEL), BF16)],
        compiler_params=_params(("parallel", "arbitrary"), vmem),
        name="norm_linear",
    )(x, _row_param(gain), w)


_QKV_TN = 1024
_ATTN_SCALE = HEAD_DIM ** -0.5
assert math.frexp(_ATTN_SCALE)[0] == 0.5


def _rope_tables(positions):
    half = HEAD_DIM // 2
    inv = ROPE_THETA ** (-jnp.arange(half, dtype=F32) * (2.0 / HEAD_DIM))
    ang = positions.astype(F32)[:, None] * inv[None, :]
    cos = jnp.cos(ang)
    sin = jnp.sin(ang)
    reps = V7X_LANES // HEAD_DIM
    cos_t = jnp.tile(jnp.concatenate([cos, cos], axis=-1), (1, reps))
    sin_t = jnp.tile(jnp.concatenate([-sin, sin], axis=-1), (1, reps))
    return cos_t, sin_t


def _qkv_kernel(x_ref, gain_ref, w_ref, cos_ref, sin_ref, q_ref, kv_ref, h_scr):
    j = pl.program_id(1)
    tm = x_ref.shape[0]

    @pl.when(j == 0)
    def _():
        h_scr[...] = _rms(x_ref[...], gain_ref[...]).astype(BF16)

    y = jnp.dot(h_scr[...], w_ref[...], preferred_element_type=F32)
    cos = cos_ref[...]
    sin = sin_ref[...]
    lane = lax.broadcasted_iota(jnp.int32, (tm, V7X_LANES), 1)
    first_half = (lane % HEAD_DIM) < (HEAD_DIM // 2)

    def rope(yc):
        rot = jnp.where(
            first_half,
            pltpu.roll(yc, V7X_LANES - HEAD_DIM // 2, 1),
            pltpu.roll(yc, HEAD_DIM // 2, 1),
        )
        return yc * cos + rot * sin

    def chunk(c):
        return slice(c * V7X_LANES, (c + 1) * V7X_LANES)

    @pl.when(j < Q_WIDTH // _QKV_TN)
    def _():
        for c in range(_QKV_TN // V7X_LANES):
            q_ref[:, chunk(c)] = (rope(y[:, chunk(c)]) * _ATTN_SCALE).astype(q_ref.dtype)

    @pl.when(j == Q_WIDTH // _QKV_TN)
    def _():
        for c in range(KV_WIDTH // V7X_LANES):
            kv_ref[:, chunk(c)] = rope(y[:, chunk(c)])
        kv_ref[:, KV_WIDTH:] = y[:, KV_WIDTH:]


def _qkv(x, gain, gain_layer, w, w_layer, cos_t, sin_t, *, tm, q_dtype):
    m = x.shape[0]
    n_tab = cos_t.shape[0] // tm
    n_q_blocks = Q_WIDTH // _QKV_TN
    vmem = (
        2 * _nbytes((tm, D_MODEL), F32)
        + _nbytes((tm, D_MODEL), BF16)
        + 2 * _nbytes((D_MODEL, _QKV_TN), BF16)
        + 6 * _nbytes((tm, _QKV_TN), F32)
    )
    return pl.pallas_call(
        _qkv_kernel,
        grid=(m // tm, n_q_blocks + 1),
        in_specs=[
            pl.BlockSpec((tm, D_MODEL), lambda i, j: (i, 0)),
            _row_param_spec(gain_layer, D_MODEL),
            pl.BlockSpec((None, D_MODEL, _QKV_TN), lambda i, j: (w_layer, 0, j)),
            pl.BlockSpec((tm, V7X_LANES), lambda i, j: (i % n_tab, 0)),
            pl.BlockSpec((tm, V7X_LANES), lambda i, j: (i % n_tab, 0)),
        ],
        out_specs=[
            pl.BlockSpec((tm, _QKV_TN), lambda i, j: (i, jnp.minimum(j, n_q_blocks - 1))),
            pl.BlockSpec((tm, 2 * KV_WIDTH), lambda i, j: (i, 0)),
        ],
        out_shape=[
            jax.ShapeDtypeStruct((m, Q_WIDTH), q_dtype),
            jax.ShapeDtypeStruct((m, 2 * KV_WIDTH), F32),
        ],
        scratch_shapes=[pltpu.VMEM((tm, D_MODEL), BF16)],
        compiler_params=_params(("parallel", "arbitrary"), vmem),
        name="qkv_rope",
    )(x, _row_param(gain), w, cos_t, sin_t)


def _linear_residual_kernel(o_ref, w_ref, x_ref, out_ref):
    out_ref[...] = x_ref[...] + jnp.dot(o_ref[...].astype(BF16), w_ref[...], preferred_element_type=F32)


def _linear_residual(o, w, layer, x, *, tm):
    m, k = o.shape
    vmem = (
        2 * _nbytes((tm, k), o.dtype)
        + 2 * _nbytes((k, D_MODEL), BF16)
        + 6 * _nbytes((tm, D_MODEL), F32)
    )
    return pl.pallas_call(
        _linear_residual_kernel,
        grid=(m // tm,),
        in_specs=[
            pl.BlockSpec((tm, k), lambda i: (i, 0)),
            pl.BlockSpec((None, k, D_MODEL), lambda i: (layer, 0, 0)),
            pl.BlockSpec((tm, D_MODEL), lambda i: (i, 0)),
        ],
        out_specs=pl.BlockSpec((tm, D_MODEL), lambda i: (i, 0)),
        out_shape=jax.ShapeDtypeStruct((m, D_MODEL), F32),
        compiler_params=_params(("parallel",), vmem),
        name="linear_residual",
    )(o, w, x)


def _head_cols(h):
    return slice(h * HEAD_DIM, (h + 1) * HEAD_DIM)


def _group_sinks(sinks_ref, kvh, g_of_row):
    sink = jnp.full(g_of_row.shape, sinks_ref[kvh * GROUP], F32)
    for g in range(1, GROUP):
        sink = jnp.where(g_of_row == g, sinks_ref[kvh * GROUP + g], sink)
    return sink


def _sink_softmax_weights(s, sink):
    m = jnp.maximum(jnp.max(s, axis=-1, keepdims=True), sink)
    e = jnp.exp(s - m)
    denom = jnp.sum(e, axis=-1, keepdims=True) + jnp.exp(sink - m)
    return e, 1.0 / denom


def _swa_prompt_kernel(sinks_ref, q_ref, kv_prev_ref, kv_cur_ref, o_ref):
    n = pl.program_id(1)
    blk = WINDOW
    rows = GROUP * blk
    row = lax.broadcasted_iota(jnp.int32, (rows, 2 * blk), 0)
    kj = lax.broadcasted_iota(jnp.int32, (rows, 2 * blk), 1)
    d = row % blk + blk - kj
    mask = (d >= 0) & (d < WINDOW) & ((kj >= blk) | (n > 0))
    g_of_row = lax.broadcasted_iota(jnp.int32, (rows, 1), 0) // blk
    ones = jnp.ones((2 * blk, HEAD_DIM), BF16)
    for kvh in range(N_KV_HEADS):
        ks = _head_cols(kvh)
        vs = _head_cols(N_KV_HEADS + kvh)
        k = jnp.concatenate([kv_prev_ref[:, ks], kv_cur_ref[:, ks]], axis=0).astype(BF16)
        v = jnp.concatenate([kv_prev_ref[:, vs], kv_cur_ref[:, vs]], axis=0).astype(BF16)
        v_ones = jnp.concatenate([v, ones], axis=1)
        q = jnp.concatenate([q_ref[:, _head_cols(kvh * GROUP + g)] for g in range(GROUP)], axis=0)
        s = lax.dot_general(q, k, (((1,), (1,)), ((), ())), preferred_element_type=F32)
        s = jnp.where(mask, s, -jnp.inf)
        sink = _group_sinks(sinks_ref, kvh, g_of_row)
        m = jnp.maximum(jnp.max(s, axis=-1, keepdims=True), sink)
        e = jnp.exp(s - m).astype(BF16)
        o_sum = jnp.dot(e, v_ones, preferred_element_type=F32)
        denom = o_sum[:, HEAD_DIM:HEAD_DIM + 1] + jnp.exp(sink - m)
        o = o_sum[:, :HEAD_DIM] * (1.0 / denom)
        for g in range(GROUP):
            o_ref[:, _head_cols(kvh * GROUP + g)] = o[g * blk:(g + 1) * blk, :].astype(o_ref.dtype)


def _swa_prompt(q, kv, sinks):
    nb = SEQ // WINDOW
    vmem = 2 * _nbytes((WINDOW, Q_WIDTH), BF16) * 2 + 4 * _nbytes((WINDOW, 2 * KV_WIDTH), F32) + 16 * 1024 * 1024
    return pl.pallas_call(
        _swa_prompt_kernel,
        grid=(BATCH, nb),
        in_specs=[
            pl.BlockSpec(memory_space=pltpu.SMEM),
            pl.BlockSpec((WINDOW, Q_WIDTH), lambda b, n: (b * nb + n, 0)),
            pl.BlockSpec((WINDOW, 2 * KV_WIDTH), lambda b, n: (b * nb + jnp.maximum(n - 1, 0), 0)),
            pl.BlockSpec((WINDOW, 2 * KV_WIDTH), lambda b, n: (b * nb + n, 0)),
        ],
        out_specs=pl.BlockSpec((WINDOW, Q_WIDTH), lambda b, n: (b * nb + n, 0)),
        out_shape=jax.ShapeDtypeStruct((BATCH * SEQ, Q_WIDTH), BF16),
        compiler_params=_params(("parallel", "arbitrary"), vmem),
        name="swa_prompt",
    )(sinks, q, kv, kv)


def _swa_sample_kernel(sinks_ref, q_ref, kv_ref, ck_ref, cv_ref, o_ref, nk_ref, nv_ref, kall_scr, vall_scr):
    nb = ck_ref.shape[0]
    t_new = DEC_SEQ
    s_len = WINDOW + t_new
    rows = GROUP * t_new
    kall_scr[:, :WINDOW, :] = ck_ref[...]
    vall_scr[:, :WINDOW, :] = cv_ref[...]
    kall_scr[:, WINDOW:, :] = kv_ref[:, :KV_WIDTH].reshape(nb, t_new, KV_WIDTH)
    vall_scr[:, WINDOW:, :] = kv_ref[:, KV_WIDTH:].reshape(nb, t_new, KV_WIDTH)
    nk_ref[...] = kall_scr[:, t_new:, :]
    nv_ref[...] = vall_scr[:, t_new:, :]

    row = lax.broadcasted_iota(jnp.int32, (1, rows, s_len), 1)
    kj = lax.broadcasted_iota(jnp.int32, (1, rows, s_len), 2)
    d = row % t_new + WINDOW - kj
    mask = (d >= 0) & (d < WINDOW)
    g_of_row = lax.broadcasted_iota(jnp.int32, (1, rows, 1), 1) // t_new
    q3 = q_ref[...].reshape(nb, t_new, Q_WIDTH)
    for kvh in range(N_KV_HEADS):
        k = kall_scr[:, :, _head_cols(kvh)].astype(BF16)
        v = vall_scr[:, :, _head_cols(kvh)].astype(BF16)
        q = jnp.concatenate([q3[:, :, _head_cols(kvh * GROUP + g)] for g in range(GROUP)], axis=1).astype(BF16)
        s = jnp.einsum("bqd,bkd->bqk", q, k, preferred_element_type=F32)
        s = jnp.where(mask, s, -jnp.inf)
        e, inv = _sink_softmax_weights(s, _group_sinks(sinks_ref, kvh, g_of_row))
        o = jnp.einsum("bqk,bkd->bqd", e.astype(BF16), v, preferred_element_type=F32) * inv
        for g in range(GROUP):
            o_ref[:, _head_cols(kvh * GROUP + g)] = o[:, g * t_new:(g + 1) * t_new, :].reshape(nb * t_new, HEAD_DIM)


_SWA_SAMPLE_SEQS_PER_STEP = 8


def _swa_sample(q, kv, cache_k, cache_v, sinks):
    nb = _SWA_SAMPLE_SEQS_PER_STEP
    cache_spec = pl.BlockSpec((nb, WINDOW, KV_WIDTH), lambda b: (b, 0, 0))
    all_keys = (nb, WINDOW + DEC_SEQ, KV_WIDTH)
    vmem = 8 * _nbytes((nb, WINDOW, KV_WIDTH), F32) + 2 * _nbytes(all_keys, F32) + 16 * 1024 * 1024
    return pl.pallas_call(
        _swa_sample_kernel,
        grid=(DEC_BATCH // nb,),
        in_specs=[
            pl.BlockSpec(memory_space=pltpu.SMEM),
            pl.BlockSpec((nb * DEC_SEQ, Q_WIDTH), lambda b: (b, 0)),
            pl.BlockSpec((nb * DEC_SEQ, 2 * KV_WIDTH), lambda b: (b, 0)),
            cache_spec,
            cache_spec,
        ],
        out_specs=[pl.BlockSpec((nb * DEC_SEQ, Q_WIDTH), lambda b: (b, 0)), cache_spec, cache_spec],
        out_shape=[
            jax.ShapeDtypeStruct((N_SAMPLE_ROWS, Q_WIDTH), F32),
            jax.ShapeDtypeStruct((DEC_BATCH, WINDOW, KV_WIDTH), F32),
            jax.ShapeDtypeStruct((DEC_BATCH, WINDOW, KV_WIDTH), F32),
        ],
        scratch_shapes=[pltpu.VMEM(all_keys, F32), pltpu.VMEM(all_keys, F32)],
        compiler_params=_params(("parallel",), vmem),
        name="swa_sample",
    )(sinks, q, kv, cache_k, cache_v)


def _cross_kernel(x_ref, gain_ref, wq_ref, mk_ref, mv_ref, wo_ref, out_ref, *, seqs, rows):
    scale = MEM_HEAD_DIM ** -0.5
    x = x_ref[...]
    h = _rms(x, gain_ref[...]).astype(BF16)
    q = jnp.dot(h, wq_ref[...], preferred_element_type=F32)
    heads = []
    for hd in range(N_MEM_HEADS):
        cs = slice(hd * MEM_HEAD_DIM, (hd + 1) * MEM_HEAD_DIM)
        qh = q[:, cs].reshape(seqs, rows, MEM_HEAD_DIM).astype(BF16)
        if mk_ref.ndim == 4:
            k = mk_ref[:, :, hd, :].astype(BF16)
            v = mv_ref[:, :, hd, :].astype(BF16)
        else:
            k = mk_ref[:, :, cs].astype(BF16)
            v = mv_ref[:, :, cs].astype(BF16)
        s = jnp.einsum("bqd,bkd->bqk", qh, k, preferred_element_type=F32) * scale
        e = jnp.exp(s - jnp.max(s, axis=-1, keepdims=True))
        inv = 1.0 / jnp.sum(e, axis=-1, keepdims=True)
        o = jnp.einsum("bqk,bkd->bqd", e.astype(BF16), v, preferred_element_type=F32) * inv
        heads.append(o.reshape(seqs * rows, MEM_HEAD_DIM))
    o_all = jnp.concatenate(heads, axis=-1).astype(BF16)
    out_ref[...] = x + jnp.dot(o_all, wo_ref[...], preferred_element_type=F32)


def _cross_block(x, gain, wq, mem_k, mem_v, wo, layer, *, n_seq, seqs, rows):
    seq_len = x.shape[0] // n_seq
    nt = seq_len // rows if seqs == 1 else 1
    tile = seqs * rows
    x_spec = pl.BlockSpec((tile, D_MODEL), lambda g, i: (g * nt + i, 0))
    mem_tail = mem_k.shape[3:]
    mem_spec = pl.BlockSpec((None, seqs, N_MEM) + mem_tail, lambda g, i: (layer, g, 0) + (0,) * len(mem_tail))
    mem_pad = V7X_SUBLANES // N_MEM_HEADS if len(mem_tail) == 2 else 1
    vmem = (
        4 * _nbytes((tile, D_MODEL), F32)
        + 4 * mem_pad * _nbytes((seqs, N_MEM, MEM_WIDTH), F32)
        + 4 * _nbytes((D_MODEL, MEM_WIDTH), BF16)
        + 4 * _nbytes((tile, D_MODEL), F32)
        + 8 * _nbytes((tile, N_MEM), F32)
    )
    return pl.pallas_call(
        functools.partial(_cross_kernel, seqs=seqs, rows=rows),
        grid=(n_seq // seqs, nt),
        in_specs=[
            x_spec,
            _row_param_spec(layer, D_MODEL),
            pl.BlockSpec((None, D_MODEL, MEM_WIDTH), lambda g, i: (layer, 0, 0)),
            mem_spec,
            mem_spec,
            pl.BlockSpec((None, MEM_WIDTH, D_MODEL), lambda g, i: (layer, 0, 0)),
        ],
        out_specs=x_spec,
        out_shape=jax.ShapeDtypeStruct(x.shape, F32),
        compiler_params=_params(("parallel", "arbitrary"), vmem),
        name="cross_block",
    )(x, _row_param(gain), wq, mem_k, mem_v, wo)


def _log_sigmoid(x):
    return jnp.minimum(x, 0.0) - jnp.log1p(jnp.exp(-jnp.abs(x)))


def _one_minus_exp2(y, exp_y):
    return jnp.tanh(-y) * (1.0 + exp_y * exp_y)


def _lru_gates(xc, wa_ref, ba_ref, wx_ref, bx_ref, lam_ref):
    r_parts = []
    i_parts = []
    for n in range(LRU_BLOCKS):
        cs = slice(n * LRU_BLOCK_SIZE, (n + 1) * LRU_BLOCK_SIZE)
        xb = xc[:, cs].astype(BF16)
        r_parts.append(jnp.dot(xb, wa_ref[n], preferred_element_type=F32))
        i_parts.append(jnp.dot(xb, wx_ref[n], preferred_element_type=F32))
    r = jax.nn.sigmoid(jnp.concatenate(r_parts, axis=-1) + ba_ref[...])
    ig = jax.nn.sigmoid(jnp.concatenate(i_parts, axis=-1) + bx_ref[...])
    log_a = LRU_C * r * _log_sigmoid(lam_ref[...])
    a = jnp.exp(log_a)
    u = jnp.sqrt(_one_minus_exp2(log_a, a)) * (ig * xc)
    return a, u


def _lru_prompt_kernel(xb_ref, gate_ref, x_ref, wc_ref, bc_ref, wa_ref, ba_ref, wx_ref, bx_ref, lam_ref,
                       wo_ref, out_ref, hlast_ref, xext_scr, a_scr, u_scr, h_scr):
    i = pl.program_id(1)
    tm = xb_ref.shape[0]
    pad = V7X_SUBLANES

    @pl.when(i == 0)
    def _():
        xext_scr[0:pad, :] = jnp.zeros((pad, LRU_WIDTH), F32)
        h_scr[...] = jnp.zeros((1, LRU_WIDTH), F32)

    xext_scr[pad:pad + tm, :] = xb_ref[...]
    xc = bc_ref[...] + wc_ref[CONV_WIDTH - 1:CONV_WIDTH, :] * xb_ref[...]
    for k in range(1, CONV_WIDTH):
        xc = xc + wc_ref[CONV_WIDTH - 1 - k:CONV_WIDTH - k, :] * xext_scr[pad - k:pad - k + tm, :]
    xext_scr[0:pad, :] = xext_scr[tm:tm + pad, :]

    a, u = _lru_gates(xc, wa_ref, ba_ref, wx_ref, bx_ref, lam_ref)
    a_scr[...] = a
    u_scr[...] = u

    def step(t, h):
        h = a_scr[pl.ds(t, 1), :] * h + u_scr[pl.ds(t, 1), :]
        u_scr[pl.ds(t, 1), :] = h
        return h

    h_scr[...] = lax.fori_loop(0, tm, step, h_scr[...], unroll=8)
    hlast_ref[...] = u_scr[tm - pad:tm, :]
    y = (u_scr[...] * gate_ref[...]).astype(BF16)
    out_ref[...] = x_ref[...] + jnp.dot(y, wo_ref[...], preferred_element_type=F32)


def _lru_weight_specs(layer_idx):
    def const3(*_):
        return (layer_idx, 0, 0)

    def const4(*_):
        return (layer_idx, 0, 0, 0)

    gate_w = pl.BlockSpec((None, LRU_BLOCKS, LRU_BLOCK_SIZE, LRU_BLOCK_SIZE), const4)
    vec = _row_param_spec(layer_idx, LRU_WIDTH)
    return [
        pl.BlockSpec((None, CONV_WIDTH, LRU_WIDTH), const3),
        vec,
        gate_w,
        vec,
        gate_w,
        vec,
        vec,
        pl.BlockSpec((None, LRU_WIDTH, D_MODEL), const3),
    ]


def _lru_weight_args(w_conv, b_conv, w_gate_a, b_gate_a, w_gate_x, b_gate_x, lam, w_out):
    return (w_conv, _row_param(b_conv), w_gate_a, _row_param(b_gate_a), w_gate_x, _row_param(b_gate_x),
            _row_param(lam), w_out)


def _lru_prompt(xy, x, lw, j, *, tm):
    nt = SEQ // tm
    row_spec0 = pl.BlockSpec((tm, LRU_WIDTH), lambda b, i: (b * nt + i, 0))
    row_spec1 = pl.BlockSpec((tm, LRU_WIDTH), lambda b, i: (b * nt + i, 1))
    vmem = (
        8 * _nbytes((tm, LRU_WIDTH), F32)
        + 3 * _nbytes((tm + V7X_SUBLANES, LRU_WIDTH), F32)
        + 2 * _nbytes((LRU_WIDTH, D_MODEL), BF16)
        + 4 * _nbytes((LRU_BLOCKS, LRU_BLOCK_SIZE, LRU_BLOCK_SIZE), BF16)
        + 10 * _nbytes((tm, LRU_WIDTH), F32)
    )
    return pl.pallas_call(
        _lru_prompt_kernel,
        grid=(BATCH, nt),
        in_specs=[row_spec0, row_spec1, row_spec0] + _lru_weight_specs(j),
        out_specs=[
            row_spec0,
            pl.BlockSpec((None, V7X_SUBLANES, LRU_WIDTH), lambda b, i: (b, 0, 0)),
        ],
        out_shape=[
            jax.ShapeDtypeStruct((BATCH * SEQ, D_MODEL), F32),
            jax.ShapeDtypeStruct((BATCH, V7X_SUBLANES, LRU_WIDTH), F32),
        ],
        scratch_shapes=[
            pltpu.VMEM((tm + V7X_SUBLANES, LRU_WIDTH), F32),
            pltpu.VMEM((tm, LRU_WIDTH), F32),
            pltpu.VMEM((tm, LRU_WIDTH), F32),
            pltpu.VMEM((1, LRU_WIDTH), F32),
        ],
        compiler_params=_params(("arbitrary", "arbitrary"), vmem),
        name="lru_prompt",
    )(xy, xy, x, *lw)


def _lru_sample_kernel(xb_ref, gate_ref, x_ref, cbuf_ref, h0_ref, wc_ref, bc_ref, wa_ref, ba_ref, wx_ref, bx_ref,
                       lam_ref, wo_ref, out_ref, hs_ref):
    rows = xb_ref.shape[0]
    t = lax.broadcasted_iota(jnp.int32, (rows, LRU_WIDTH), 0) % DEC_SEQ
    xb = xb_ref[...]
    cbuf = cbuf_ref[...]
    xc = bc_ref[...] + wc_ref[CONV_WIDTH - 1:CONV_WIDTH, :] * xb
    for k in range(1, CONV_WIDTH):
        shift_c = CONV_WIDTH - 1 - k
        from_buf = cbuf if shift_c == 0 else pltpu.roll(cbuf, rows - shift_c, 0)
        xk = jnp.where(t >= k, pltpu.roll(xb, k, 0), from_buf)
        xc = xc + wc_ref[CONV_WIDTH - 1 - k:CONV_WIDTH - k, :] * xk

    a, u = _lru_gates(xc, wa_ref, ba_ref, wx_ref, bx_ref, lam_ref)
    k = 1
    while k < DEC_SEQ:
        m = t >= k
        u = jnp.where(m, a * pltpu.roll(u, k, 0) + u, u)
        a = jnp.where(m, a * pltpu.roll(a, k, 0), a)
        k *= 2
    hs = u + a * h0_ref[...]
    hs_ref[...] = hs
    y = (hs * gate_ref[...]).astype(BF16)
    out_ref[...] = x_ref[...] + jnp.dot(y, wo_ref[...], preferred_element_type=F32)


def _lru_sample(xy, x, cbuf_rows, h0_rows, lw, j):
    rows = N_SAMPLE_ROWS
    row_spec0 = pl.BlockSpec((rows, LRU_WIDTH), lambda i: (0, 0))
    row_spec1 = pl.BlockSpec((rows, LRU_WIDTH), lambda i: (0, 1))
    vmem = 40 * _nbytes((rows, LRU_WIDTH), F32) + 2 * _nbytes((LRU_WIDTH, D_MODEL), BF16)
    return pl.pallas_call(
        _lru_sample_kernel,
        grid=(1,),
        in_specs=[row_spec0, row_spec1, row_spec0, row_spec0, row_spec0] + _lru_weight_specs(j),
        out_specs=[row_spec0, row_spec0],
        out_shape=[
            jax.ShapeDtypeStruct((rows, D_MODEL), F32),
            jax.ShapeDtypeStruct((rows, LRU_WIDTH), F32),
        ],
        compiler_params=_params(("arbitrary",), vmem),
        name="lru_sample",
    )(xy, xy, x, cbuf_rows, h0_rows, *lw)


_FFN_TM = 1024
_FFN_TF = 512
_FFN_ROW_CHUNK = 512


def kernel(x_prompt, x_sample, cache_swa_k, cache_swa_v, state_lru_h, state_lru_conv, cache_mem_k, cache_mem_v,
           mem_prompt, ln_ffn1, ffn1_w_in, ffn1_w_out, ln_mix, swa_w_qkv, swa_w_o, swa_sinks, lru_w_in,
           lru_w_conv, lru_b_conv, lru_w_gate_a, lru_b_gate_a, lru_w_gate_x, lru_b_gate_x, lru_lambda,
           lru_w_out, ln_cross, ln_mem, cross_w_q, cross_w_kv, cross_w_o, ln_ffn2, ffn2_w_in, ffn2_w_out,
           ln_final):
    bf = lambda w: w.astype(BF16)
    ffn1_w_in, ffn1_w_out, ffn2_w_in, ffn2_w_out = bf(ffn1_w_in), bf(ffn1_w_out), bf(ffn2_w_in), bf(ffn2_w_out)
    swa_w_qkv, swa_w_o = bf(swa_w_qkv), bf(swa_w_o)
    lru_w_in, lru_w_out = bf(lru_w_in), bf(lru_w_out)
    lru_w_gate_a, lru_w_gate_x = bf(lru_w_gate_a), bf(lru_w_gate_x)
    cross_w_q, cross_w_kv, cross_w_o = bf(cross_w_q), bf(cross_w_kv), bf(cross_w_o)

    xp = x_prompt.reshape(N_PROMPT_ROWS, D_MODEL)
    xs = x_sample.reshape(N_SAMPLE_ROWS, D_MODEL)

    mem_rows = mem_prompt.reshape(BATCH * N_MEM, D_MODEL)
    mkv = [
        _norm_linear(mem_rows, ln_mem, layer, cross_w_kv, layer, tm=BATCH * N_MEM, tn=MEM_WIDTH, out_dtype=F32)
        for layer in range(DEPTH)
    ]
    mem_k_prompt = jnp.stack([m[:, :MEM_WIDTH] for m in mkv]).reshape(DEPTH, BATCH, N_MEM, MEM_WIDTH)
    mem_v_prompt = jnp.stack([m[:, MEM_WIDTH:] for m in mkv]).reshape(DEPTH, BATCH, N_MEM, MEM_WIDTH)

    cos_p, sin_p = _rope_tables(jnp.arange(SEQ, dtype=jnp.int32))
    cos_s, sin_s = _rope_tables(PAST_LEN + jnp.arange(N_SAMPLE_ROWS, dtype=jnp.int32) % DEC_SEQ)

    swa_k_p, swa_v_p, swa_k_s, swa_v_s = [], [], [], []
    lru_h_p, lru_conv_p, lru_h_s, lru_conv_s = [], [], [], []

    for layer in range(DEPTH):
        j = layer // 2
        xp, xs = _ffn(xp, xs, ln_ffn1, ffn1_w_in, ffn1_w_out, layer, tm=_FFN_TM, tf=_FFN_TF, row_chunk=_FFN_ROW_CHUNK)

        if layer % 2 == 0:
            q_p, kv_p = _qkv(xp, ln_mix, layer, swa_w_qkv, j, cos_p, sin_p, tm=512, q_dtype=BF16)
            o_p = _swa_prompt(q_p, kv_p, swa_sinks[j])
            xp = _linear_residual(o_p, swa_w_o, j, xp, tm=512)
            kv_tail = kv_p.reshape(BATCH, SEQ, 2 * KV_WIDTH)[:, SEQ - WINDOW:, :]
            swa_k_p.append(kv_tail[..., :KV_WIDTH].reshape(BATCH, WINDOW, N_KV_HEADS, HEAD_DIM))
            swa_v_p.append(kv_tail[..., KV_WIDTH:].reshape(BATCH, WINDOW, N_KV_HEADS, HEAD_DIM))

            q_s, kv_s = _qkv(xs, ln_mix, layer, swa_w_qkv, j, cos_s, sin_s, tm=N_SAMPLE_ROWS, q_dtype=F32)
            o_s, nk, nv = _swa_sample(
                q_s, kv_s,
                cache_swa_k[j].reshape(DEC_BATCH, WINDOW, KV_WIDTH),
                cache_swa_v[j].reshape(DEC_BATCH, WINDOW, KV_WIDTH),
                swa_sinks[j],
            )
            xs = _linear_residual(o_s, swa_w_o, j, xs, tm=N_SAMPLE_ROWS)
            swa_k_s.append(nk.reshape(DEC_BATCH, WINDOW, N_KV_HEADS, HEAD_DIM))
            swa_v_s.append(nv.reshape(DEC_BATCH, WINDOW, N_KV_HEADS, HEAD_DIM))
        else:
            lw = _lru_weight_args(lru_w_conv, lru_b_conv, lru_w_gate_a, lru_b_gate_a, lru_w_gate_x, lru_b_gate_x,
                                  lru_lambda, lru_w_out)
            n_cols = 2 * LRU_WIDTH // 1024
            xy_p = _norm_linear(xp, ln_mix, layer, lru_w_in, j, tm=512, tn=1024, out_dtype=F32,
                                gelu_from=n_cols // 2)
            xp, h_last = _lru_prompt(xy_p, xp, lw, j, tm=256)
            lru_h_p.append(h_last[:, V7X_SUBLANES - 1, :])
            lru_conv_p.append(
                xy_p.reshape(BATCH, SEQ, 2 * LRU_WIDTH)[:, SEQ - (CONV_WIDTH - 1):, :LRU_WIDTH])

            xy_s = _norm_linear(xs, ln_mix, layer, lru_w_in, j, tm=N_SAMPLE_ROWS, tn=1024, out_dtype=F32,
                                gelu_from=n_cols // 2)
            cbuf_rows = jnp.pad(
                state_lru_conv[j], ((0, 0), (0, DEC_SEQ - (CONV_WIDTH - 1)), (0, 0))
            ).reshape(N_SAMPLE_ROWS, LRU_WIDTH)
            h0_rows = jnp.repeat(state_lru_h[j], DEC_SEQ, axis=0)
            xs, hs_s = _lru_sample(xy_s, xs, cbuf_rows, h0_rows, lw, j)
            lru_h_s.append(hs_s.reshape(DEC_BATCH, DEC_SEQ, LRU_WIDTH)[:, DEC_SEQ - 1, :])
            xp_tail = jnp.concatenate(
                [state_lru_conv[j], xy_s.reshape(DEC_BATCH, DEC_SEQ, 2 * LRU_WIDTH)[:, :, :LRU_WIDTH]], axis=1)
            lru_conv_s.append(xp_tail[:, -(CONV_WIDTH - 1):, :])

        xp = _cross_block(xp, ln_cross, cross_w_q, mem_k_prompt, mem_v_prompt, cross_w_o, layer,
                          n_seq=BATCH, seqs=1, rows=512)
        xs = _cross_block(xs, ln_cross, cross_w_q, cache_mem_k, cache_mem_v, cross_w_o, layer,
                          n_seq=DEC_BATCH, seqs=8, rows=DEC_SEQ)

        fin = ln_final if layer == DEPTH - 1 else None
        xp, xs = _ffn(xp, xs, ln_ffn2, ffn2_w_in, ffn2_w_out, layer, tm=_FFN_TM, tf=_FFN_TF, row_chunk=_FFN_ROW_CHUNK,
                      final_gain=fin)

    return (
        xp.reshape(BATCH, SEQ, D_MODEL),
        xs.reshape(DEC_BATCH, DEC_SEQ, D_MODEL),
        jnp.stack(swa_k_p), jnp.stack(swa_v_p), jnp.stack(swa_k_s), jnp.stack(swa_v_s),
        jnp.stack(lru_h_p), jnp.stack(lru_conv_p), jnp.stack(lru_h_s), jnp.stack(lru_conv_s),
        mem_k_prompt.reshape(DEPTH, BATCH, N_MEM, N_MEM_HEADS, MEM_HEAD_DIM),
        mem_v_prompt.reshape(DEPTH, BATCH, N_MEM, N_MEM_HEADS, MEM_HEAD_DIM),
    )
```

```python
import functools
import math

import jax
import jax.numpy as jnp
from jax import lax
from jax.experimental import pallas as pl
from jax.experimental.pallas import tpu as pltpu

F32 = jnp.float32
BF16 = jnp.bfloat16

D_MODEL = 2048
BATCH = 2
SEQ = 4096
DEPTH = 2
DEC_BATCH = 32
DEC_SEQ = 8
PAST_LEN = 16384
N_HEADS = 32
N_KV_HEADS = 8
HEAD_DIM = 64
GROUP = N_HEADS // N_KV_HEADS
WINDOW = 128
ROPE_THETA = 10000.0
LRU_WIDTH = D_MODEL
LRU_BLOCKS = 8
LRU_BLOCK_SIZE = LRU_WIDTH // LRU_BLOCKS
CONV_WIDTH = 4
LRU_C = 8.0
D_FF = 5632
N_MEM = 256
N_MEM_HEADS = 4
MEM_HEAD_DIM = 128
MEM_WIDTH = N_MEM_HEADS * MEM_HEAD_DIM
NORM_EPS = 1e-6
Q_WIDTH = N_HEADS * HEAD_DIM
KV_WIDTH = N_KV_HEADS * HEAD_DIM

V7X_LANES = 128
V7X_SUBLANES = 8
V7X_VMEM_BYTES = 64 * 1024 * 1024
V7X_VMEM_USABLE_BYTES = V7X_VMEM_BYTES - 2 * 1024 * 1024

N_PROMPT_ROWS = BATCH * SEQ
N_SAMPLE_ROWS = DEC_BATCH * DEC_SEQ


def _params(semantics, vmem_bytes):
    return pltpu.CompilerParams(
        dimension_semantics=semantics,
        vmem_limit_bytes=int(min(max(vmem_bytes, 32 * 1024 * 1024), V7X_VMEM_USABLE_BYTES)),
    )


def _nbytes(shape, dtype):
    return math.prod(shape) * jnp.dtype(dtype).itemsize


def _rms(x, gain):
    ms = jnp.mean(x * x, axis=-1, keepdims=True)
    return x * lax.rsqrt(ms + NORM_EPS) * gain


def _row_param(p):
    return p.reshape(p.shape[0], 1, p.shape[-1])


def _row_param_spec(layer, width):
    return pl.BlockSpec((None, 1, width), lambda *_: (layer, 0, 0))


def _for_chunks(n_rows, chunk, body):
    n_chunks = n_rows // chunk
    if n_chunks == 1:
        body(pl.ds(0, n_rows))
        return

    def step(c, carry):
        body(pl.ds(pl.multiple_of(c * chunk, chunk), chunk))
        return carry

    lax.fori_loop(0, n_chunks, step, 0)


def _ffn_kernel(x_ref, gain_ref, wg_ref, wu_ref, wo_ref, xe_ref, *rest, row_chunk, final_norm, cast_next):
    rest = list(rest)
    gfin_ref = rest.pop(0) if final_norm else None
    next_in_ref, next_out_ref = (rest.pop(0), rest.pop(0)) if cast_next else (None, None)
    o_ref, oe_ref = rest.pop(0), rest.pop(0)
    next_in_bf_ref, next_out_bf_ref = (rest.pop(0), rest.pop(0)) if cast_next else (None, None)
    h_scr, he_scr = rest
    f = pl.program_id(1)

    if cast_next:
        next_in_bf_ref[...] = next_in_ref[...].astype(BF16)
        next_out_bf_ref[...] = next_out_ref[...].astype(BF16)

    def run(x_ref, o_ref, h_scr, chunk):
        n_rows = x_ref.shape[0]

        @pl.when(f == 0)
        def _():
            def prologue(rows):
                h_scr[rows, :] = _rms(x_ref[rows, :], gain_ref[...]).astype(BF16)
                o_ref[rows, :] = jnp.zeros((chunk, D_MODEL), F32)

            _for_chunks(n_rows, chunk, prologue)

        def accumulate(rows):
            h = h_scr[rows, :]
            g = jnp.dot(h, wg_ref[...], preferred_element_type=F32)
            u = jnp.dot(h, wu_ref[...], preferred_element_type=F32)
            a = (g * jax.nn.sigmoid(g) * u).astype(BF16)
            o_ref[rows, :] += jnp.dot(a, wo_ref[...], preferred_element_type=F32)

        _for_chunks(n_rows, chunk, accumulate)

        @pl.when(f == pl.num_programs(1) - 1)
        def _():
            def epilogue(rows):
                y = x_ref[rows, :] + 0.5 * o_ref[rows, :]
                if final_norm:
                    y = _rms(y, gfin_ref[...])
                o_ref[rows, :] = y

            _for_chunks(n_rows, chunk, epilogue)

    run(x_ref, o_ref, h_scr, row_chunk)

    @pl.when(pl.program_id(0) == 0)
    def _():
        run(xe_ref, oe_ref, he_scr, xe_ref.shape[0])


def _ffn(x, x_extra, gain, layer, w_in, w_out, *, tm, tf, row_chunk, final_gain=None, cast_next=None):
    m = x.shape[0]
    me = x_extra.shape[0]
    n_tiles = m // tm
    nf = D_FF // tf
    final_norm = final_gain is not None
    row_spec = pl.BlockSpec((tm, D_MODEL), lambda i, f: (i, 0))
    extra_spec = pl.BlockSpec((me, D_MODEL), lambda i, f: (0, 0))
    in_specs = [
        row_spec,
        _row_param_spec(layer, D_MODEL),
        pl.BlockSpec((D_MODEL, tf), lambda i, f: (0, f)),
        pl.BlockSpec((D_MODEL, tf), lambda i, f: (0, f + nf)),
        pl.BlockSpec((tf, D_MODEL), lambda i, f: (f, 0)),
        pl.BlockSpec((me, D_MODEL), lambda i, f: (0, 0), pipeline_mode=pl.Buffered(1)),
    ]
    args = [x, _row_param(gain), w_in, w_in, w_out, x_extra]
    out_specs = [row_spec, extra_spec]
    out_shape = [jax.ShapeDtypeStruct((m, D_MODEL), F32), jax.ShapeDtypeStruct((me, D_MODEL), F32)]
    if final_norm:
        in_specs.append(_row_param_spec(0, D_MODEL))
        args.append(final_gain.reshape(1, 1, D_MODEL))
    vmem = (
        4 * _nbytes((tm, D_MODEL), F32)
        + _nbytes((tm, D_MODEL), BF16)
        + 4 * _nbytes((me, D_MODEL), F32)
        + _nbytes((me, D_MODEL), BF16)
        + 2 * 3 * _nbytes((D_MODEL, tf), BF16)
        + 4 * _nbytes((row_chunk, tf), F32)
        + 2 * _nbytes((row_chunk, D_MODEL), F32)
    )
    if cast_next is not None:
        nw_in, nw_out, nl = cast_next
        in_blk = (D_MODEL // n_tiles, 2 * D_FF // nf)
        out_blk = (D_FF // nf, D_MODEL // n_tiles)
        in_specs += [
            pl.BlockSpec((None,) + in_blk, lambda i, f: (nl, i, f)),
            pl.BlockSpec((None,) + out_blk, lambda i, f: (nl, f, i)),
        ]
        args += [nw_in, nw_out]
        out_specs += [pl.BlockSpec(in_blk, lambda i, f: (i, f)), pl.BlockSpec(out_blk, lambda i, f: (f, i))]
        out_shape += [jax.ShapeDtypeStruct(nw_in.shape[1:], BF16), jax.ShapeDtypeStruct(nw_out.shape[1:], BF16)]
        vmem += 2 * (_nbytes(in_blk, F32) + _nbytes(in_blk, BF16) + _nbytes(out_blk, F32) + _nbytes(out_blk, BF16))
    return pl.pallas_call(
        functools.partial(_ffn_kernel, row_chunk=row_chunk, final_norm=final_norm, cast_next=cast_next is not None),
        grid=(n_tiles, nf),
        in_specs=in_specs,
        out_specs=out_specs,
        out_shape=out_shape,
        scratch_shapes=[pltpu.VMEM((tm, D_MODEL), BF16), pltpu.VMEM((me, D_MODEL), BF16)],
        compiler_params=_params(("arbitrary", "arbitrary"), vmem),
        name="ffn",
    )(*args)


def _gelu_tanh(y):
    return 0.5 * y * (1.0 + jnp.tanh(math.sqrt(2.0 / math.pi) * (y + 0.044715 * (y * y * y))))


def _norm_linear_kernel(x_ref, gain_ref, w_ref, o_ref, h_scr, *, gelu_from):
    j = pl.program_id(1)

    @pl.when(j == 0)
    def _():
        h_scr[...] = _rms(x_ref[...], gain_ref[...]).astype(BF16)

    y = jnp.dot(h_scr[...], w_ref[...], preferred_element_type=F32)
    if gelu_from is None:
        o_ref[...] = y.astype(o_ref.dtype)
    else:

        @pl.when(j < gelu_from)
        def _():
            o_ref[...] = y.astype(o_ref.dtype)

        @pl.when(j >= gelu_from)
        def _():
            o_ref[...] = _gelu_tanh(y).astype(o_ref.dtype)


def _norm_linear(x, gain, gain_layer, w, w_layer, *, tm, tn, out_dtype, gelu_from=None):
    m = x.shape[0]
    n = w.shape[-1]
    vmem = (
        2 * _nbytes((tm, D_MODEL), F32)
        + _nbytes((tm, D_MODEL), BF16)
        + 2 * _nbytes((D_MODEL, tn), BF16)
        + 4 * _nbytes((tm, tn), F32)
    )
    return pl.pallas_call(
        functools.partial(_norm_linear_kernel, gelu_from=gelu_from),
        grid=(m // tm, n // tn),
        in_specs=[
            pl.BlockSpec((tm, D_MODEL), lambda i, j: (i, 0)),
            _row_param_spec(gain_layer, D_MODEL),
            pl.BlockSpec((None, D_MODEL, tn), lambda i, j: (w_layer, 0, j)),
        ],
        out_specs=pl.BlockSpec((tm, tn), lambda i, j: (i, j)),
        out_shape=jax.ShapeDtypeStruct((m, n), out_dtype),
        scratch_shapes=[pltpu.VMEM((tm, D_MODEL), BF16)],
        compiler_params=_params(("parallel", "arbitrary"), vmem),
        name="norm_linear",
    )(x, _row_param(gain), w)


_QKV_TN = 1024
_ATTN_SCALE = HEAD_DIM ** -0.5
assert math.frexp(_ATTN_SCALE)[0] == 0.5


def _rope_tables(positions):
    half = HEAD_DIM // 2
    inv = ROPE_THETA ** (-jnp.arange(half, dtype=F32) * (2.0 / HEAD_DIM))
    ang = positions.astype(F32)[:, None] * inv[None, :]
    cos = jnp.cos(ang)
    sin = jnp.sin(ang)
    reps = V7X_LANES // HEAD_DIM
    cos_t = jnp.tile(jnp.concatenate([cos, cos], axis=-1), (1, reps))
    sin_t = jnp.tile(jnp.concatenate([-sin, sin], axis=-1), (1, reps))
    return cos_t, sin_t


def _qkv_kernel(x_ref, gain_ref, w_ref, cos_ref, sin_ref, q_ref, kv_ref, *rest, paired_kv):
    if paired_kv:
        kpair_ref, vpair_ref, h_scr = rest
    else:
        (h_scr,) = rest
    j = pl.program_id(1)
    tm = x_ref.shape[0]

    @pl.when(j == 0)
    def _():
        h_scr[...] = _rms(x_ref[...], gain_ref[...]).astype(BF16)

    y = jnp.dot(h_scr[...], w_ref[...], preferred_element_type=F32)
    cos = cos_ref[...]
    sin = sin_ref[...]
    lane = lax.broadcasted_iota(jnp.int32, (tm, V7X_LANES), 1)
    first_half = (lane % HEAD_DIM) < (HEAD_DIM // 2)

    def rope(yc):
        rot = jnp.where(
            first_half,
            pltpu.roll(yc, V7X_LANES - HEAD_DIM // 2, 1),
            pltpu.roll(yc, HEAD_DIM // 2, 1),
        )
        return yc * cos + rot * sin

    def chunk(c):
        return slice(c * V7X_LANES, (c + 1) * V7X_LANES)

    @pl.when(j < Q_WIDTH // _QKV_TN)
    def _():
        for c in range(_QKV_TN // V7X_LANES):
            q_ref[:, chunk(c)] = (rope(y[:, chunk(c)]) * _ATTN_SCALE).astype(q_ref.dtype)

    @pl.when(j == Q_WIDTH // _QKV_TN)
    def _():
        heads_per_chunk = V7X_LANES // HEAD_DIM
        low_half = lane < HEAD_DIM
        for c in range(2 * KV_WIDTH // V7X_LANES):
            r = rope(y[:, chunk(c)]) if c < KV_WIDTH // V7X_LANES else y[:, chunk(c)]
            kv_ref[:, chunk(c)] = r
            if paired_kv:
                dst, c_in = (kpair_ref, c) if c < KV_WIDTH // V7X_LANES else (vpair_ref, c - KV_WIDTH // V7X_LANES)
                swapped = pltpu.roll(r, HEAD_DIM, 1)
                dst[:, chunk(heads_per_chunk * c_in)] = jnp.where(low_half, r, swapped).astype(BF16)
                dst[:, chunk(heads_per_chunk * c_in + 1)] = jnp.where(low_half, swapped, r).astype(BF16)


def _qkv(x, gain, gain_layer, w, w_layer, cos_t, sin_t, *, tm, q_dtype, seq_len=None):
    m = x.shape[0]
    n_tab = cos_t.shape[0] // tm
    n_q_blocks = Q_WIDTH // _QKV_TN
    paired_kv = seq_len is not None
    vmem = (
        2 * _nbytes((tm, D_MODEL), F32)
        + _nbytes((tm, D_MODEL), BF16)
        + 2 * _nbytes((D_MODEL, _QKV_TN), BF16)
        + 8 * _nbytes((tm, _QKV_TN), F32)
    )
    out_specs = [pl.BlockSpec((tm, _QKV_TN), lambda i, j: (i, jnp.minimum(j, n_q_blocks - 1)))]
    out_shape = [jax.ShapeDtypeStruct((m, Q_WIDTH), q_dtype)]
    if paired_kv:
        tiles_per_seq = seq_len // tm
        pair_width = N_KV_HEADS * V7X_LANES
        out_specs.append(pl.BlockSpec((None, tm, 2 * KV_WIDTH), lambda i, j: (i // tiles_per_seq, 0, 0)))
        out_shape.append(jax.ShapeDtypeStruct((m // seq_len, tm, 2 * KV_WIDTH), F32))
        out_specs += [pl.BlockSpec((tm, pair_width), lambda i, j: (i, 0))] * 2
        out_shape += [jax.ShapeDtypeStruct((m, pair_width), BF16)] * 2
    else:
        out_specs.append(pl.BlockSpec((tm, 2 * KV_WIDTH), lambda i, j: (i, 0)))
        out_shape.append(jax.ShapeDtypeStruct((m, 2 * KV_WIDTH), F32))
    return pl.pallas_call(
        functools.partial(_qkv_kernel, paired_kv=paired_kv),
        grid=(m // tm, n_q_blocks + 1),
        in_specs=[
            pl.BlockSpec((tm, D_MODEL), lambda i, j: (i, 0)),
            _row_param_spec(gain_layer, D_MODEL),
            pl.BlockSpec((None, D_MODEL, _QKV_TN), lambda i, j: (w_layer, 0, j)),
            pl.BlockSpec((tm, V7X_LANES), lambda i, j: (i % n_tab, 0)),
            pl.BlockSpec((tm, V7X_LANES), lambda i, j: (i % n_tab, 0)),
        ],
        out_specs=out_specs,
        out_shape=out_shape,
        scratch_shapes=[pltpu.VMEM((tm, D_MODEL), BF16)],
        compiler_params=_params(("arbitrary", "arbitrary"), vmem),
        name="qkv_rope",
    )(x, _row_param(gain), w, cos_t, sin_t)


def _linear_residual_kernel(o_ref, w_ref, x_ref, out_ref):
    out_ref[...] = x_ref[...] + jnp.dot(o_ref[...].astype(BF16), w_ref[...], preferred_element_type=F32)


def _linear_residual(o, w, layer, x, *, tm):
    m, k = o.shape
    vmem = (
        2 * _nbytes((tm, k), o.dtype)
        + 2 * _nbytes((k, D_MODEL), BF16)
        + 6 * _nbytes((tm, D_MODEL), F32)
    )
    return pl.pallas_call(
        _linear_residual_kernel,
        grid=(m // tm,),
        in_specs=[
            pl.BlockSpec((tm, k), lambda i: (i, 0)),
            pl.BlockSpec((None, k, D_MODEL), lambda i: (layer, 0, 0)),
            pl.BlockSpec((tm, D_MODEL), lambda i: (i, 0)),
        ],
        out_specs=pl.BlockSpec((tm, D_MODEL), lambda i: (i, 0)),
        out_shape=jax.ShapeDtypeStruct((m, D_MODEL), F32),
        compiler_params=_params(("parallel",), vmem),
        name="linear_residual",
    )(o, w, x)


def _head_cols(h):
    return slice(h * HEAD_DIM, (h + 1) * HEAD_DIM)


def _group_sinks(sinks_ref, kvh, g_of_row):
    sink = jnp.full(g_of_row.shape, sinks_ref[kvh * GROUP], F32)
    for g in range(1, GROUP):
        sink = jnp.where(g_of_row == g, sinks_ref[kvh * GROUP + g], sink)
    return sink


def _sink_softmax_weights(s, sink):
    m = jnp.maximum(jnp.max(s, axis=-1, keepdims=True), sink)
    e = jnp.exp(s - m)
    denom = jnp.sum(e, axis=-1, keepdims=True) + jnp.exp(sink - m)
    return e, 1.0 / denom


def _lane_block(c):
    return slice(c * V7X_LANES, (c + 1) * V7X_LANES)


def _swa_prompt_kernel(sinks_ref, q_ref, k_prev_ref, k_cur_ref, v_prev_ref, v_cur_ref, o_ref):
    n = pl.program_id(1)
    blk = WINDOW
    rows = GROUP * blk
    qi = lax.broadcasted_iota(jnp.int32, (blk, 2 * blk), 0)
    kj = lax.broadcasted_iota(jnp.int32, (blk, 2 * blk), 1)
    d = qi + blk - kj
    mask = ((d >= 0) & (d < WINDOW) & ((kj >= blk) | (n > 0))) | (kj == 0)
    bias = jnp.where(mask, 0.0, -jnp.inf)[None]
    low_half = lax.broadcasted_iota(jnp.int32, (blk, V7X_LANES), 1) < HEAD_DIM
    g_of_row = lax.broadcasted_iota(jnp.int32, (rows, V7X_LANES), 0) // blk
    col = lax.broadcasted_iota(jnp.int32, (rows, V7X_LANES), 1)
    q_onehot = jnp.where((col == g_of_row) | (col == g_of_row + GROUP), 1.0, 0.0).astype(BF16)
    keep_low = jnp.where(low_half, 1.0, 0.0).astype(BF16)
    keep_high = jnp.where(low_half, 0.0, 1.0).astype(BF16)
    head_rows = 2 * V7X_SUBLANES
    is_slot0 = lax.broadcasted_iota(jnp.int32, (head_rows, V7X_LANES), 0) == 0
    kcol = lax.broadcasted_iota(jnp.int32, (1, V7X_LANES), 1)
    ones = jnp.ones((2 * blk, V7X_LANES), BF16)
    no_feat = jnp.zeros((2 * blk - head_rows, V7X_LANES), BF16)

    def without_slot0(prev_ref, cur_ref, kvh):
        head = jnp.where(is_slot0, 0.0, prev_ref[:head_rows, _lane_block(kvh)].astype(F32)).astype(BF16)
        return jnp.concatenate([head, prev_ref[head_rows:, _lane_block(kvh)], cur_ref[:, _lane_block(kvh)]], axis=0)

    for kvh in range(N_KV_HEADS):
        sink_row = jnp.zeros((1, V7X_LANES), F32)
        for g in range(GROUP):
            sink_row = jnp.where((kcol == g) | (kcol == g + GROUP), sinks_ref[kvh * GROUP + g], sink_row)
        sink_hi = sink_row.astype(BF16).astype(F32)
        sink_feat = jnp.where(kcol < GROUP, sink_hi, sink_row - sink_hi)
        feat = jnp.concatenate([jnp.where(is_slot0, sink_feat, 0.0).astype(BF16), no_feat], axis=0)
        k_ext = jnp.concatenate([without_slot0(k_prev_ref, k_cur_ref, kvh), feat], axis=1)
        v_ones = jnp.concatenate([without_slot0(v_prev_ref, v_cur_ref, kvh), ones], axis=1)
        q_heads = []
        for pair in range(GROUP // 2):
            q_pair = q_ref[:, _lane_block(kvh * (GROUP // 2) + pair)]
            q_heads += [q_pair * keep_low, q_pair * keep_high]
        q_ext = jnp.concatenate([jnp.concatenate(q_heads, axis=0), q_onehot], axis=1)
        s = lax.dot_general(q_ext, k_ext, (((1,), (1,)), ((), ())), preferred_element_type=F32)
        s = (s.reshape(GROUP, blk, 2 * blk) + bias).reshape(rows, 2 * blk)
        e = jnp.exp(s - jnp.max(s, axis=-1, keepdims=True)).astype(BF16)
        o_sum = jnp.dot(e, v_ones, preferred_element_type=F32)
        o = o_sum[:, :V7X_LANES] * (1.0 / o_sum[:, V7X_LANES:])
        for pair in range(GROUP // 2):
            even = o[(2 * pair) * blk:(2 * pair + 1) * blk, :]
            odd = o[(2 * pair + 1) * blk:(2 * pair + 2) * blk, :]
            o_ref[:, _lane_block(kvh * (GROUP // 2) + pair)] = jnp.where(low_half, even, odd).astype(o_ref.dtype)


def _swa_prompt(q, k_pair, v_pair, sinks):
    nb = SEQ // WINDOW
    pair_width = N_KV_HEADS * V7X_LANES
    cur = pl.BlockSpec((WINDOW, pair_width), lambda b, n: (b * nb + n, 0))
    prev = pl.BlockSpec((WINDOW, pair_width), lambda b, n: (b * nb + jnp.maximum(n - 1, 0), 0))
    vmem = (
        4 * _nbytes((WINDOW, Q_WIDTH), BF16)
        + 8 * _nbytes((WINDOW, pair_width), BF16)
        + 24 * _nbytes((GROUP * WINDOW, 2 * WINDOW), F32)
    )
    return pl.pallas_call(
        _swa_prompt_kernel,
        grid=(BATCH, nb),
        in_specs=[
            pl.BlockSpec(memory_space=pltpu.SMEM),
            pl.BlockSpec((WINDOW, Q_WIDTH), lambda b, n: (b * nb + n, 0)),
            prev, cur, prev, cur,
        ],
        out_specs=pl.BlockSpec((WINDOW, Q_WIDTH), lambda b, n: (b * nb + n, 0)),
        out_shape=jax.ShapeDtypeStruct((BATCH * SEQ, Q_WIDTH), BF16),
        compiler_params=_params(("parallel", "arbitrary"), vmem),
        name="swa_prompt",
    )(sinks, q, k_pair, k_pair, v_pair, v_pair)


def _swa_sample_kernel(sinks_ref, q_ref, kv_ref, ck_ref, cv_ref, o_ref, nk_ref, nv_ref, kall_scr, vall_scr):
    nb = ck_ref.shape[0]
    t_new = DEC_SEQ
    s_len = WINDOW + t_new
    rows = GROUP * t_new
    kall_scr[:, :WINDOW, :] = ck_ref[...]
    vall_scr[:, :WINDOW, :] = cv_ref[...]
    kall_scr[:, WINDOW:, :] = kv_ref[:, :KV_WIDTH].reshape(nb, t_new, KV_WIDTH)
    vall_scr[:, WINDOW:, :] = kv_ref[:, KV_WIDTH:].reshape(nb, t_new, KV_WIDTH)
    nk_ref[...] = kall_scr[:, t_new:, :]
    nv_ref[...] = vall_scr[:, t_new:, :]

    row = lax.broadcasted_iota(jnp.int32, (1, rows, s_len), 1)
    kj = lax.broadcasted_iota(jnp.int32, (1, rows, s_len), 2)
    d = row % t_new + WINDOW - kj
    mask = (d >= 0) & (d < WINDOW)
    g_of_row = lax.broadcasted_iota(jnp.int32, (1, rows, 1), 1) // t_new
    q3 = q_ref[...].reshape(nb, t_new, Q_WIDTH)
    for kvh in range(N_KV_HEADS):
        k = kall_scr[:, :, _head_cols(kvh)].astype(BF16)
        v = vall_scr[:, :, _head_cols(kvh)].astype(BF16)
        q = jnp.concatenate([q3[:, :, _head_cols(kvh * GROUP + g)] for g in range(GROUP)], axis=1).astype(BF16)
        s = jnp.einsum("bqd,bkd->bqk", q, k, preferred_element_type=F32)
        s = jnp.where(mask, s, -jnp.inf)
        e, inv = _sink_softmax_weights(s, _group_sinks(sinks_ref, kvh, g_of_row))
        o = jnp.einsum("bqk,bkd->bqd", e.astype(BF16), v, preferred_element_type=F32) * inv
        for g in range(GROUP):
            o_ref[:, _head_cols(kvh * GROUP + g)] = o[:, g * t_new:(g + 1) * t_new, :].reshape(nb * t_new, HEAD_DIM)


_SWA_SAMPLE_SEQS_PER_STEP = 8


def _swa_sample(q, kv, cache_k, cache_v, sinks):
    nb = _SWA_SAMPLE_SEQS_PER_STEP
    cache_spec = pl.BlockSpec((nb, WINDOW, KV_WIDTH), lambda b: (b, 0, 0))
    all_keys = (nb, WINDOW + DEC_SEQ, KV_WIDTH)
    vmem = 8 * _nbytes((nb, WINDOW, KV_WIDTH), F32) + 2 * _nbytes(all_keys, F32) + 16 * 1024 * 1024
    return pl.pallas_call(
        _swa_sample_kernel,
        grid=(DEC_BATCH // nb,),
        in_specs=[
            pl.BlockSpec(memory_space=pltpu.SMEM),
            pl.BlockSpec((nb * DEC_SEQ, Q_WIDTH), lambda b: (b, 0)),
            pl.BlockSpec((nb * DEC_SEQ, 2 * KV_WIDTH), lambda b: (b, 0)),
            cache_spec,
            cache_spec,
        ],
        out_specs=[pl.BlockSpec((nb * DEC_SEQ, Q_WIDTH), lambda b: (b, 0)), cache_spec, cache_spec],
        out_shape=[
            jax.ShapeDtypeStruct((N_SAMPLE_ROWS, Q_WIDTH), F32),
            jax.ShapeDtypeStruct((DEC_BATCH, WINDOW, KV_WIDTH), F32),
            jax.ShapeDtypeStruct((DEC_BATCH, WINDOW, KV_WIDTH), F32),
        ],
        scratch_shapes=[pltpu.VMEM(all_keys, F32), pltpu.VMEM(all_keys, F32)],
        compiler_params=_params(("parallel",), vmem),
        name="swa_sample",
    )(sinks, q, kv, cache_k, cache_v)


def _cross_kernel(x_ref, gain_ref, wq_ref, mk_ref, mv_ref, wo_ref, out_ref, *, seqs, rows):
    scale = MEM_HEAD_DIM ** -0.5
    x = x_ref[...]
    h = _rms(x, gain_ref[...]).astype(BF16)
    q = jnp.dot(h, wq_ref[...], preferred_element_type=F32)
    heads = []
    for hd in range(N_MEM_HEADS):
        cs = slice(hd * MEM_HEAD_DIM, (hd + 1) * MEM_HEAD_DIM)
        qh = q[:, cs].reshape(seqs, rows, MEM_HEAD_DIM).astype(BF16)
        if mk_ref.ndim == 4:
            k = mk_ref[:, :, hd, :].astype(BF16)
            v = mv_ref[:, :, hd, :].astype(BF16)
        else:
            k = mk_ref[:, :, cs].astype(BF16)
            v = mv_ref[:, :, cs].astype(BF16)
        s = jnp.einsum("bqd,bkd->bqk", qh, k, preferred_element_type=F32) * scale
        e = jnp.exp(s - jnp.max(s, axis=-1, keepdims=True))
        inv = 1.0 / jnp.sum(e, axis=-1, keepdims=True)
        o = jnp.einsum("bqk,bkd->bqd", e.astype(BF16), v, preferred_element_type=F32) * inv
        heads.append(o.reshape(seqs * rows, MEM_HEAD_DIM))
    o_all = jnp.concatenate(heads, axis=-1).astype(BF16)
    out_ref[...] = x + jnp.dot(o_all, wo_ref[...], preferred_element_type=F32)


def _cross_block(x, gain, wq, mem_k, mem_v, wo, layer, *, n_seq, seqs, rows):
    seq_len = x.shape[0] // n_seq
    nt = seq_len // rows if seqs == 1 else 1
    tile = seqs * rows
    x_spec = pl.BlockSpec((tile, D_MODEL), lambda g, i: (g * nt + i, 0))
    mem_tail = mem_k.shape[3:]
    mem_spec = pl.BlockSpec((None, seqs, N_MEM) + mem_tail, lambda g, i: (layer, g, 0) + (0,) * len(mem_tail))
    mem_pad = V7X_SUBLANES // N_MEM_HEADS if len(mem_tail) == 2 else 1
    vmem = (
        4 * _nbytes((tile, D_MODEL), F32)
        + 4 * mem_pad * _nbytes((seqs, N_MEM, MEM_WIDTH), F32)
        + 4 * _nbytes((D_MODEL, MEM_WIDTH), BF16)
        + 4 * _nbytes((tile, D_MODEL), F32)
        + 8 * _nbytes((tile, N_MEM), F32)
    )
    return pl.pallas_call(
        functools.partial(_cross_kernel, seqs=seqs, rows=rows),
        grid=(n_seq // seqs, nt),
        in_specs=[
            x_spec,
            _row_param_spec(layer, D_MODEL),
            pl.BlockSpec((None, D_MODEL, MEM_WIDTH), lambda g, i: (layer, 0, 0)),
            mem_spec,
            mem_spec,
            pl.BlockSpec((None, MEM_WIDTH, D_MODEL), lambda g, i: (layer, 0, 0)),
        ],
        out_specs=x_spec,
        out_shape=jax.ShapeDtypeStruct(x.shape, F32),
        compiler_params=_params(("parallel", "arbitrary"), vmem),
        name="cross_block",
    )(x, _row_param(gain), wq, mem_k, mem_v, wo)


def _log_sigmoid(x):
    return jnp.minimum(x, 0.0) - jnp.log1p(jnp.exp(-jnp.abs(x)))


def _one_minus_exp2(y, exp_y):
    return jnp.tanh(-y) * (1.0 + exp_y * exp_y)


def _lru_gates(xc, wa_ref, ba_ref, wx_ref, bx_ref, lam_ref):
    r_parts = []
    i_parts = []
    for n in range(LRU_BLOCKS):
        cs = slice(n * LRU_BLOCK_SIZE, (n + 1) * LRU_BLOCK_SIZE)
        xb = xc[:, cs].astype(BF16)
        r_parts.append(jnp.dot(xb, wa_ref[n], preferred_element_type=F32))
        i_parts.append(jnp.dot(xb, wx_ref[n], preferred_element_type=F32))
    r = jax.nn.sigmoid(jnp.concatenate(r_parts, axis=-1) + ba_ref[...])
    ig = jax.nn.sigmoid(jnp.concatenate(i_parts, axis=-1) + bx_ref[...])
    log_a = LRU_C * r * _log_sigmoid(lam_ref[...])
    a = jnp.exp(log_a)
    u = jnp.sqrt(_one_minus_exp2(log_a, a)) * (ig * xc)
    return a, u


def _lru_prompt_kernel(xb_ref, gate_ref, x_ref, wc_ref, bc_ref, wa_ref, ba_ref, wx_ref, bx_ref, lam_ref,
                       wo_ref, out_ref, hlast_ref, xext_scr, a_scr, u_scr, h_scr):
    i = pl.program_id(1)
    tm = xb_ref.shape[0]
    pad = V7X_SUBLANES

    @pl.when(i == 0)
    def _():
        xext_scr[0:pad, :] = jnp.zeros((pad, LRU_WIDTH), F32)
        h_scr[...] = jnp.zeros((1, LRU_WIDTH), F32)

    xext_scr[pad:pad + tm, :] = xb_ref[...]
    xc = bc_ref[...] + wc_ref[CONV_WIDTH - 1:CONV_WIDTH, :] * xb_ref[...]
    for k in range(1, CONV_WIDTH):
        xc = xc + wc_ref[CONV_WIDTH - 1 - k:CONV_WIDTH - k, :] * xext_scr[pad - k:pad - k + tm, :]
    xext_scr[0:pad, :] = xext_scr[tm:tm + pad, :]

    a, u = _lru_gates(xc, wa_ref, ba_ref, wx_ref, bx_ref, lam_ref)
    a_scr[...] = a
    u_scr[...] = u

    def step(t, h):
        h = a_scr[pl.ds(t, 1), :] * h + u_scr[pl.ds(t, 1), :]
        u_scr[pl.ds(t, 1), :] = h
        return h

    h_scr[...] = lax.fori_loop(0, tm, step, h_scr[...], unroll=8)
    hlast_ref[...] = u_scr[tm - pad:tm, :]
    y = (u_scr[...] * gate_ref[...]).astype(BF16)
    out_ref[...] = x_ref[...] + jnp.dot(y, wo_ref[...], preferred_element_type=F32)


def _lru_weight_specs(layer_idx):
    def const3(*_):
        return (layer_idx, 0, 0)

    def const4(*_):
        return (layer_idx, 0, 0, 0)

    gate_w = pl.BlockSpec((None, LRU_BLOCKS, LRU_BLOCK_SIZE, LRU_BLOCK_SIZE), const4)
    vec = _row_param_spec(layer_idx, LRU_WIDTH)
    return [
        pl.BlockSpec((None, CONV_WIDTH, LRU_WIDTH), const3),
        vec,
        gate_w,
        vec,
        gate_w,
        vec,
        vec,
        pl.BlockSpec((None, LRU_WIDTH, D_MODEL), const3),
    ]


def _lru_weight_args(w_conv, b_conv, w_gate_a, b_gate_a, w_gate_x, b_gate_x, lam, w_out):
    return (w_conv, _row_param(b_conv), w_gate_a, _row_param(b_gate_a), w_gate_x, _row_param(b_gate_x),
            _row_param(lam), w_out)


def _lru_prompt(xy, x, lw, j, *, tm):
    nt = SEQ // tm
    row_spec0 = pl.BlockSpec((tm, LRU_WIDTH), lambda b, i: (b * nt + i, 0))
    row_spec1 = pl.BlockSpec((tm, LRU_WIDTH), lambda b, i: (b * nt + i, 1))
    vmem = (
        8 * _nbytes((tm, LRU_WIDTH), F32)
        + 3 * _nbytes((tm + V7X_SUBLANES, LRU_WIDTH), F32)
        + 2 * _nbytes((LRU_WIDTH, D_MODEL), BF16)
        + 4 * _nbytes((LRU_BLOCKS, LRU_BLOCK_SIZE, LRU_BLOCK_SIZE), BF16)
        + 10 * _nbytes((tm, LRU_WIDTH), F32)
    )
    return pl.pallas_call(
        _lru_prompt_kernel,
        grid=(BATCH, nt),
        in_specs=[row_spec0, row_spec1, row_spec0] + _lru_weight_specs(j),
        out_specs=[
            row_spec0,
            pl.BlockSpec((None, V7X_SUBLANES, LRU_WIDTH), lambda b, i: (b, 0, 0)),
        ],
        out_shape=[
            jax.ShapeDtypeStruct((BATCH * SEQ, D_MODEL), F32),
            jax.ShapeDtypeStruct((BATCH, V7X_SUBLANES, LRU_WIDTH), F32),
        ],
        scratch_shapes=[
            pltpu.VMEM((tm + V7X_SUBLANES, LRU_WIDTH), F32),
            pltpu.VMEM((tm, LRU_WIDTH), F32),
            pltpu.VMEM((tm, LRU_WIDTH), F32),
            pltpu.VMEM((1, LRU_WIDTH), F32),
        ],
        compiler_params=_params(("arbitrary", "arbitrary"), vmem),
        name="lru_prompt",
    )(xy, xy, x, *lw)


def _lru_sample_kernel(xb_ref, gate_ref, x_ref, cbuf_ref, h0_ref, wc_ref, bc_ref, wa_ref, ba_ref, wx_ref, bx_ref,
                       lam_ref, wo_ref, out_ref, hs_ref):
    rows = xb_ref.shape[0]
    t = lax.broadcasted_iota(jnp.int32, (rows, LRU_WIDTH), 0) % DEC_SEQ
    xb = xb_ref[...]
    cbuf = cbuf_ref[...]
    xc = bc_ref[...] + wc_ref[CONV_WIDTH - 1:CONV_WIDTH, :] * xb
    for k in range(1, CONV_WIDTH):
        shift_c = CONV_WIDTH - 1 - k
        from_buf = cbuf if shift_c == 0 else pltpu.roll(cbuf, rows - shift_c, 0)
        xk = jnp.where(t >= k, pltpu.roll(xb, k, 0), from_buf)
        xc = xc + wc_ref[CONV_WIDTH - 1 - k:CONV_WIDTH - k, :] * xk

    a, u = _lru_gates(xc, wa_ref, ba_ref, wx_ref, bx_ref, lam_ref)
    k = 1
    while k < DEC_SEQ:
        m = t >= k
        u = jnp.where(m, a * pltpu.roll(u, k, 0) + u, u)
        a = jnp.where(m, a * pltpu.roll(a, k, 0), a)
        k *= 2
    hs = u + a * h0_ref[...]
    hs_ref[...] = hs
    y = (hs * gate_ref[...]).astype(BF16)
    out_ref[...] = x_ref[...] + jnp.dot(y, wo_ref[...], preferred_element_type=F32)


def _lru_sample(xy, x, cbuf_rows, h0_rows, lw, j):
    rows = N_SAMPLE_ROWS
    row_spec0 = pl.BlockSpec((rows, LRU_WIDTH), lambda i: (0, 0))
    row_spec1 = pl.BlockSpec((rows, LRU_WIDTH), lambda i: (0, 1))
    vmem = 40 * _nbytes((rows, LRU_WIDTH), F32) + 2 * _nbytes((LRU_WIDTH, D_MODEL), BF16)
    return pl.pallas_call(
        _lru_sample_kernel,
        grid=(1,),
        in_specs=[row_spec0, row_spec1, row_spec0, row_spec0, row_spec0] + _lru_weight_specs(j),
        out_specs=[row_spec0, row_spec0],
        out_shape=[
            jax.ShapeDtypeStruct((rows, D_MODEL), F32),
            jax.ShapeDtypeStruct((rows, LRU_WIDTH), F32),
        ],
        compiler_params=_params(("arbitrary",), vmem),
        name="lru_sample",
    )(xy, xy, x, cbuf_rows, h0_rows, *lw)


_FFN_TM = 1024
_FFN_TF = 512
_FFN_ROW_CHUNK = 512


def kernel(x_prompt, x_sample, cache_swa_k, cache_swa_v, state_lru_h, state_lru_conv, cache_mem_k, cache_mem_v,
           mem_prompt, ln_ffn1, ffn1_w_in, ffn1_w_out, ln_mix, swa_w_qkv, swa_w_o, swa_sinks, lru_w_in,
           lru_w_conv, lru_b_conv, lru_w_gate_a, lru_b_gate_a, lru_w_gate_x, lru_b_gate_x, lru_lambda,
           lru_w_out, ln_cross, ln_mem, cross_w_q, cross_w_kv, cross_w_o, ln_ffn2, ffn2_w_in, ffn2_w_out,
           ln_final):
    bf = lambda w: w.astype(BF16)
    ffn_seq = []
    for layer in range(DEPTH):
        ffn_seq += [(ffn1_w_in, ffn1_w_out, layer), (ffn2_w_in, ffn2_w_out, layer)]
    ffn_w = (bf(ffn1_w_in[0]), bf(ffn1_w_out[0]))

    def ffn(x, x_extra, gain, layer, final_gain=None):
        nonlocal ffn_w
        nxt = ffn_seq[1] if len(ffn_seq) > 1 else None
        out = _ffn(x, x_extra, gain, layer, *ffn_w, tm=_FFN_TM, tf=_FFN_TF, row_chunk=_FFN_ROW_CHUNK,
                   final_gain=final_gain, cast_next=nxt)
        del ffn_seq[0]
        if nxt is not None:
            ffn_w = out[2:]
        return out[0], out[1]

    swa_w_qkv, swa_w_o = bf(swa_w_qkv), bf(swa_w_o)
    lru_w_in, lru_w_out = bf(lru_w_in), bf(lru_w_out)
    lru_w_gate_a, lru_w_gate_x = bf(lru_w_gate_a), bf(lru_w_gate_x)
    cross_w_q, cross_w_kv, cross_w_o = bf(cross_w_q), bf(cross_w_kv), bf(cross_w_o)

    xp = x_prompt.reshape(N_PROMPT_ROWS, D_MODEL)
    xs = x_sample.reshape(N_SAMPLE_ROWS, D_MODEL)

    mem_rows = mem_prompt.reshape(BATCH * N_MEM, D_MODEL)
    mkv = [
        _norm_linear(mem_rows, ln_mem, layer, cross_w_kv, layer, tm=BATCH * N_MEM, tn=MEM_WIDTH, out_dtype=F32)
        for layer in range(DEPTH)
    ]
    mem_k_prompt = jnp.stack([m[:, :MEM_WIDTH] for m in mkv]).reshape(DEPTH, BATCH, N_MEM, MEM_WIDTH)
    mem_v_prompt = jnp.stack([m[:, MEM_WIDTH:] for m in mkv]).reshape(DEPTH, BATCH, N_MEM, MEM_WIDTH)

    cos_p, sin_p = _rope_tables(jnp.arange(SEQ, dtype=jnp.int32))
    cos_s, sin_s = _rope_tables(PAST_LEN + jnp.arange(N_SAMPLE_ROWS, dtype=jnp.int32) % DEC_SEQ)

    swa_k_p, swa_v_p, swa_k_s, swa_v_s = [], [], [], []
    lru_h_p, lru_conv_p, lru_h_s, lru_conv_s = [], [], [], []

    for layer in range(DEPTH):
        j = layer // 2
        xp, xs = ffn(xp, xs, ln_ffn1, layer)

        if layer % 2 == 0:
            qkv_tm = 512
            q_p, kv_last, k_pair, v_pair = _qkv(xp, ln_mix, layer, swa_w_qkv, j, cos_p, sin_p, tm=qkv_tm,
                                                q_dtype=BF16, seq_len=SEQ)
            o_p = _swa_prompt(q_p, k_pair, v_pair, swa_sinks[j])
            xp = _linear_residual(o_p, swa_w_o, j, xp, tm=512)
            kv_tail = kv_last[:, qkv_tm - WINDOW:, :]
            swa_k_p.append(kv_tail[..., :KV_WIDTH].reshape(BATCH, WINDOW, N_KV_HEADS, HEAD_DIM))
            swa_v_p.append(kv_tail[..., KV_WIDTH:].reshape(BATCH, WINDOW, N_KV_HEADS, HEAD_DIM))

            q_s, kv_s = _qkv(xs, ln_mix, layer, swa_w_qkv, j, cos_s, sin_s, tm=N_SAMPLE_ROWS, q_dtype=F32)
            o_s, nk, nv = _swa_sample(
                q_s, kv_s,
                cache_swa_k[j].reshape(DEC_BATCH, WINDOW, KV_WIDTH),
                cache_swa_v[j].reshape(DEC_BATCH, WINDOW, KV_WIDTH),
                swa_sinks[j],
            )
            xs = _linear_residual(o_s, swa_w_o, j, xs, tm=N_SAMPLE_ROWS)
            swa_k_s.append(nk.reshape(DEC_BATCH, WINDOW, N_KV_HEADS, HEAD_DIM))
            swa_v_s.append(nv.reshape(DEC_BATCH, WINDOW, N_KV_HEADS, HEAD_DIM))
        else:
            lw = _lru_weight_args(lru_w_conv, lru_b_conv, lru_w_gate_a, lru_b_gate_a, lru_w_gate_x, lru_b_gate_x,
                                  lru_lambda, lru_w_out)
            n_cols = 2 * LRU_WIDTH // 1024
            xy_p = _norm_linear(xp, ln_mix, layer, lru_w_in, j, tm=512, tn=1024, out_dtype=F32,
                                gelu_from=n_cols // 2)
            xp, h_last = _lru_prompt(xy_p, xp, lw, j, tm=256)
            lru_h_p.append(h_last[:, V7X_SUBLANES - 1, :])
            lru_conv_p.append(
                xy_p.reshape(BATCH, SEQ, 2 * LRU_WIDTH)[:, SEQ - (CONV_WIDTH - 1):, :LRU_WIDTH])

            xy_s = _norm_linear(xs, ln_mix, layer, lru_w_in, j, tm=N_SAMPLE_ROWS, tn=1024, out_dtype=F32,
                                gelu_from=n_cols // 2)
            cbuf_rows = jnp.pad(
                state_lru_conv[j], ((0, 0), (0, DEC_SEQ - (CONV_WIDTH - 1)), (0, 0))
            ).reshape(N_SAMPLE_ROWS, LRU_WIDTH)
            h0_rows = jnp.repeat(state_lru_h[j], DEC_SEQ, axis=0)
            xs, hs_s = _lru_sample(xy_s, xs, cbuf_rows, h0_rows, lw, j)
            lru_h_s.append(hs_s.reshape(DEC_BATCH, DEC_SEQ, LRU_WIDTH)[:, DEC_SEQ - 1, :])
            xp_tail = jnp.concatenate(
                [state_lru_conv[j], xy_s.reshape(DEC_BATCH, DEC_SEQ, 2 * LRU_WIDTH)[:, :, :LRU_WIDTH]], axis=1)
            lru_conv_s.append(xp_tail[:, -(CONV_WIDTH - 1):, :])

        xp = _cross_block(xp, ln_cross, cross_w_q, mem_k_prompt, mem_v_prompt, cross_w_o, layer,
                          n_seq=BATCH, seqs=1, rows=512)
        xs = _cross_block(xs, ln_cross, cross_w_q, cache_mem_k, cache_mem_v, cross_w_o, layer,
                          n_seq=DEC_BATCH, seqs=8, rows=DEC_SEQ)

        fin = ln_final if layer == DEPTH - 1 else None
        xp, xs = ffn(xp, xs, ln_ffn2, layer, final_gain=fin)

    return (
        xp.reshape(BATCH, SEQ, D_MODEL),
        xs.reshape(DEC_BATCH, DEC_SEQ, D_MODEL),
        jnp.stack(swa_k_p), jnp.stack(swa_v_p), jnp.stack(swa_k_s), jnp.stack(swa_v_s),
        jnp.stack(lru_h_p), jnp.stack(lru_conv_p), jnp.stack(lru_h_s), jnp.stack(lru_conv_s),
        mem_k_prompt.reshape(DEPTH, BATCH, N_MEM, N_MEM_HEADS, MEM_HEAD_DIM),
        mem_v_prompt.reshape(DEPTH, BATCH, N_MEM, N_MEM_HEADS, MEM_HEAD_DIM),
    )
```

```python
import functools
import math

import jax
import jax.numpy as jnp
from jax import lax
from jax.experimental import pallas as pl
from jax.experimental.pallas import tpu as pltpu

F32 = jnp.float32
BF16 = jnp.bfloat16

D_MODEL = 2048
BATCH = 2
SEQ = 4096
DEPTH = 2
DEC_BATCH = 32
DEC_SEQ = 8
PAST_LEN = 16384
N_HEADS = 32
N_KV_HEADS = 8
HEAD_DIM = 64
GROUP = N_HEADS // N_KV_HEADS
WINDOW = 128
ROPE_THETA = 10000.0
LRU_WIDTH = D_MODEL
LRU_BLOCKS = 8
LRU_BLOCK_SIZE = LRU_WIDTH // LRU_BLOCKS
CONV_WIDTH = 4
LRU_C = 8.0
D_FF = 5632
N_MEM = 256
N_MEM_HEADS = 4
MEM_HEAD_DIM = 128
MEM_WIDTH = N_MEM_HEADS * MEM_HEAD_DIM
NORM_EPS = 1e-6
Q_WIDTH = N_HEADS * HEAD_DIM
KV_WIDTH = N_KV_HEADS * HEAD_DIM

V7X_LANES = 128
V7X_SUBLANES = 8
V7X_VMEM_BYTES = 64 * 1024 * 1024
V7X_VMEM_USABLE_BYTES = V7X_VMEM_BYTES - 2 * 1024 * 1024

N_PROMPT_ROWS = BATCH * SEQ
N_SAMPLE_ROWS = DEC_BATCH * DEC_SEQ


def _params(semantics, vmem_bytes):
    return pltpu.CompilerParams(
        dimension_semantics=semantics,
        vmem_limit_bytes=int(min(max(vmem_bytes, 32 * 1024 * 1024), V7X_VMEM_USABLE_BYTES)),
    )


def _nbytes(shape, dtype):
    return math.prod(shape) * jnp.dtype(dtype).itemsize


def _rms(x, gain):
    ms = jnp.mean(x * x, axis=-1, keepdims=True)
    return x * lax.rsqrt(ms + NORM_EPS) * gain


def _lane_block(c):
    return slice(c * V7X_LANES, (c + 1) * V7X_LANES)


def _row_param(p):
    return p.reshape(p.shape[0], 1, p.shape[-1])


def _row_param_spec(layer, width):
    return pl.BlockSpec((None, 1, width), lambda *_: (layer, 0, 0))


def _for_chunks(n_rows, chunk, body):
    n_chunks = n_rows // chunk
    if n_chunks == 1:
        body(pl.ds(0, n_rows))
        return

    def step(c, carry):
        body(pl.ds(pl.multiple_of(c * chunk, chunk), chunk))
        return carry

    lax.fori_loop(0, n_chunks, step, 0)


def _ffn_kernel(x_ref, gain_ref, wg_ref, wu_ref, wo_ref, xe_ref, *rest, row_chunk, final_norm, cast_next):
    rest = list(rest)
    gfin_ref = rest.pop(0) if final_norm else None
    next_in_ref, next_out_ref = (rest.pop(0), rest.pop(0)) if cast_next else (None, None)
    o_ref, oe_ref = rest.pop(0), rest.pop(0)
    next_in_bf_ref, next_out_bf_ref = (rest.pop(0), rest.pop(0)) if cast_next else (None, None)
    h_scr, he_scr = rest
    f = pl.program_id(1)

    if cast_next:
        next_in_bf_ref[...] = next_in_ref[...].astype(BF16)
        next_out_bf_ref[...] = next_out_ref[...].astype(BF16)

    def run(x_ref, o_ref, h_scr, chunk):
        n_rows = x_ref.shape[0]

        @pl.when(f == 0)
        def _():
            def prologue(rows):
                h_scr[rows, :] = _rms(x_ref[rows, :], gain_ref[...]).astype(BF16)
                o_ref[rows, :] = jnp.zeros((chunk, D_MODEL), F32)

            _for_chunks(n_rows, chunk, prologue)

        def accumulate(rows):
            h = h_scr[rows, :]
            g = jnp.dot(h, wg_ref[...], preferred_element_type=F32)
            u = jnp.dot(h, wu_ref[...], preferred_element_type=F32)
            a = (g * jax.nn.sigmoid(g) * u).astype(BF16)
            o_ref[rows, :] += jnp.dot(a, wo_ref[...], preferred_element_type=F32)

        _for_chunks(n_rows, chunk, accumulate)

        @pl.when(f == pl.num_programs(1) - 1)
        def _():
            def epilogue(rows):
                y = x_ref[rows, :] + 0.5 * o_ref[rows, :]
                if final_norm:
                    y = _rms(y, gfin_ref[...])
                o_ref[rows, :] = y

            _for_chunks(n_rows, chunk, epilogue)

    run(x_ref, o_ref, h_scr, row_chunk)

    @pl.when(pl.program_id(0) == 0)
    def _():
        run(xe_ref, oe_ref, he_scr, xe_ref.shape[0])


def _ffn(x, x_extra, gain, layer, w_in, w_out, *, tm, tf, row_chunk, final_gain=None, cast_next=None):
    m = x.shape[0]
    me = x_extra.shape[0]
    n_tiles = m // tm
    nf = D_FF // tf
    final_norm = final_gain is not None
    row_spec = pl.BlockSpec((tm, D_MODEL), lambda i, f: (i, 0))
    extra_spec = pl.BlockSpec((me, D_MODEL), lambda i, f: (0, 0))
    in_specs = [
        row_spec,
        _row_param_spec(layer, D_MODEL),
        pl.BlockSpec((D_MODEL, tf), lambda i, f: (0, f)),
        pl.BlockSpec((D_MODEL, tf), lambda i, f: (0, f + nf)),
        pl.BlockSpec((tf, D_MODEL), lambda i, f: (f, 0)),
        pl.BlockSpec((me, D_MODEL), lambda i, f: (0, 0), pipeline_mode=pl.Buffered(1)),
    ]
    args = [x, _row_param(gain), w_in, w_in, w_out, x_extra]
    out_specs = [row_spec, extra_spec]
    out_shape = [jax.ShapeDtypeStruct((m, D_MODEL), F32), jax.ShapeDtypeStruct((me, D_MODEL), F32)]
    if final_norm:
        in_specs.append(_row_param_spec(0, D_MODEL))
        args.append(final_gain.reshape(1, 1, D_MODEL))
    vmem = (
        4 * _nbytes((tm, D_MODEL), F32)
        + _nbytes((tm, D_MODEL), BF16)
        + 4 * _nbytes((me, D_MODEL), F32)
        + _nbytes((me, D_MODEL), BF16)
        + 2 * 3 * _nbytes((D_MODEL, tf), BF16)
        + 4 * _nbytes((row_chunk, tf), F32)
        + 2 * _nbytes((row_chunk, D_MODEL), F32)
    )
    if cast_next is not None:
        nw_in, nw_out, nl = cast_next
        in_blk = (D_MODEL // n_tiles, 2 * D_FF // nf)
        out_blk = (D_FF // nf, D_MODEL // n_tiles)
        in_specs += [
            pl.BlockSpec((None,) + in_blk, lambda i, f: (nl, i, f)),
            pl.BlockSpec((None,) + out_blk, lambda i, f: (nl, f, i)),
        ]
        args += [nw_in, nw_out]
        out_specs += [pl.BlockSpec(in_blk, lambda i, f: (i, f)), pl.BlockSpec(out_blk, lambda i, f: (f, i))]
        out_shape += [jax.ShapeDtypeStruct(nw_in.shape[1:], BF16), jax.ShapeDtypeStruct(nw_out.shape[1:], BF16)]
        vmem += 2 * (_nbytes(in_blk, F32) + _nbytes(in_blk, BF16) + _nbytes(out_blk, F32) + _nbytes(out_blk, BF16))
    return pl.pallas_call(
        functools.partial(_ffn_kernel, row_chunk=row_chunk, final_norm=final_norm, cast_next=cast_next is not None),
        grid=(n_tiles, nf),
        in_specs=in_specs,
        out_specs=out_specs,
        out_shape=out_shape,
        scratch_shapes=[pltpu.VMEM((tm, D_MODEL), BF16), pltpu.VMEM((me, D_MODEL), BF16)],
        compiler_params=_params(("arbitrary", "arbitrary"), vmem),
        name="ffn",
    )(*args)


def _gelu_tanh(y):
    return 0.5 * y * (1.0 + jnp.tanh(math.sqrt(2.0 / math.pi) * (y + 0.044715 * (y * y * y))))


def _norm_linear_kernel(x_ref, gain_ref, w_ref, o_ref, h_scr):
    @pl.when(pl.program_id(1) == 0)
    def _():
        h_scr[...] = _rms(x_ref[...], gain_ref[...]).astype(BF16)

    o_ref[...] = jnp.dot(h_scr[...], w_ref[...], preferred_element_type=F32).astype(o_ref.dtype)


def _lru_in_kernel(x_ref, gain_ref, wx_ref, wg_ref, xb_ref, gate_ref, h_scr):
    @pl.when(pl.program_id(1) == 0)
    def _():
        h_scr[...] = _rms(x_ref[...], gain_ref[...]).astype(BF16)

    h = h_scr[...]
    xb_ref[...] = jnp.dot(h, wx_ref[...], preferred_element_type=F32)
    gate_ref[...] = _gelu_tanh(jnp.dot(h, wg_ref[...], preferred_element_type=F32))


def _lru_in(x, gain, gain_layer, w, w_layer, *, tm, tn):
    m = x.shape[0]
    n_col = LRU_WIDTH // tn
    vmem = (
        2 * _nbytes((tm, D_MODEL), F32)
        + _nbytes((tm, D_MODEL), BF16)
        + 4 * _nbytes((D_MODEL, tn), BF16)
        + 10 * _nbytes((tm, tn), F32)
    )
    out_spec = pl.BlockSpec((tm, tn), lambda i, j: (i, j))
    return pl.pallas_call(
        _lru_in_kernel,
        grid=(m // tm, n_col),
        in_specs=[
            pl.BlockSpec((tm, D_MODEL), lambda i, j: (i, 0)),
            _row_param_spec(gain_layer, D_MODEL),
            pl.BlockSpec((None, D_MODEL, tn), lambda i, j: (w_layer, 0, j)),
            pl.BlockSpec((None, D_MODEL, tn), lambda i, j: (w_layer, 0, n_col + j)),
        ],
        out_specs=[out_spec, out_spec],
        out_shape=[jax.ShapeDtypeStruct((m, LRU_WIDTH), F32)] * 2,
        scratch_shapes=[pltpu.VMEM((tm, D_MODEL), BF16)],
        compiler_params=_params(("parallel", "arbitrary"), vmem),
        name="lru_in",
    )(x, _row_param(gain), w, w)


def _norm_linear(x, gain, gain_layer, w, w_layer, *, tm, tn, out_dtype):
    m = x.shape[0]
    n = w.shape[-1]
    vmem = (
        2 * _nbytes((tm, D_MODEL), F32)
        + _nbytes((tm, D_MODEL), BF16)
        + 2 * _nbytes((D_MODEL, tn), BF16)
        + 4 * _nbytes((tm, tn), F32)
    )
    return pl.pallas_call(
        _norm_linear_kernel,
        grid=(m // tm, n // tn),
        in_specs=[
            pl.BlockSpec((tm, D_MODEL), lambda i, j: (i, 0)),
            _row_param_spec(gain_layer, D_MODEL),
            pl.BlockSpec((None, D_MODEL, tn), lambda i, j: (w_layer, 0, j)),
        ],
        out_specs=pl.BlockSpec((tm, tn), lambda i, j: (i, j)),
        out_shape=jax.ShapeDtypeStruct((m, n), out_dtype),
        scratch_shapes=[pltpu.VMEM((tm, D_MODEL), BF16)],
        compiler_params=_params(("parallel", "arbitrary"), vmem),
        name="norm_linear",
    )(x, _row_param(gain), w)


_QKV_STEPS = 2
_Q_TN = Q_WIDTH // _QKV_STEPS
_ATTN_SCALE = HEAD_DIM ** -0.5
assert math.frexp(_ATTN_SCALE)[0] == 0.5
assert 2 * KV_WIDTH // _QKV_STEPS == KV_WIDTH


def _rope_tables(positions):
    half = HEAD_DIM // 2
    inv = ROPE_THETA ** (-jnp.arange(half, dtype=F32) * (2.0 / HEAD_DIM))
    ang = positions.astype(F32)[:, None] * inv[None, :]
    cos = jnp.cos(ang)
    sin = jnp.sin(ang)
    reps = V7X_LANES // HEAD_DIM
    cos_t = jnp.tile(jnp.concatenate([cos, cos], axis=-1), (1, reps))
    sin_t = jnp.tile(jnp.concatenate([-sin, sin], axis=-1), (1, reps))
    return jnp.stack([cos_t, jnp.ones_like(cos_t)]), jnp.stack([sin_t, jnp.zeros_like(sin_t)])


def _qkv_kernel(x_ref, gain_ref, wq_ref, wkv_ref, cos_ref, sin_ref, cos_kv_ref, sin_kv_ref, q_ref, kv_ref,
                pair_ref=None):
    tm = x_ref.shape[0]
    lane = lax.broadcasted_iota(jnp.int32, (tm, V7X_LANES), 1)
    first_half = (lane % HEAD_DIM) < (HEAD_DIM // 2)
    low_half = lane < HEAD_DIM

    def rope(yc, cos, sin):
        rot = jnp.where(
            first_half,
            pltpu.roll(yc, V7X_LANES - HEAD_DIM // 2, 1),
            pltpu.roll(yc, HEAD_DIM // 2, 1),
        )
        return yc * cos + rot * sin

    h = _rms(x_ref[...], gain_ref[...]).astype(BF16)
    yq = jnp.dot(h, wq_ref[...], preferred_element_type=F32)
    cos, sin = cos_ref[...], sin_ref[...]
    for c in range(_Q_TN // V7X_LANES):
        q_ref[:, _lane_block(c)] = (rope(yq[:, _lane_block(c)], cos, sin) * _ATTN_SCALE).astype(q_ref.dtype)

    ykv = jnp.dot(h, wkv_ref[...], preferred_element_type=F32)
    cos_kv, sin_kv = cos_kv_ref[...], sin_kv_ref[...]
    heads_per_block = V7X_LANES // HEAD_DIM
    for c in range(KV_WIDTH // V7X_LANES):
        r = rope(ykv[:, _lane_block(c)], cos_kv, sin_kv)
        kv_ref[:, _lane_block(c)] = r
        if pair_ref is not None:
            swapped = pltpu.roll(r, HEAD_DIM, 1)
            pair_ref[:, _lane_block(heads_per_block * c)] = jnp.where(low_half, r, swapped).astype(BF16)
            pair_ref[:, _lane_block(heads_per_block * c + 1)] = jnp.where(low_half, swapped, r).astype(BF16)


def _qkv(x, gain, gain_layer, w, w_layer, cos_t, sin_t, *, tm, q_dtype, seq_len=None):
    m = x.shape[0]
    n_tab = cos_t.shape[1] // tm
    paired_kv = seq_len is not None
    pair_width = N_KV_HEADS * V7X_LANES
    vmem = (
        2 * _nbytes((tm, D_MODEL), F32)
        + _nbytes((tm, D_MODEL), BF16)
        + 2 * _nbytes((D_MODEL, _Q_TN + KV_WIDTH), BF16)
        + 8 * _nbytes((tm, _Q_TN + KV_WIDTH), F32)
        + 4 * _nbytes((tm, pair_width), BF16)
    )
    out_specs = [pl.BlockSpec((tm, _Q_TN), lambda j, i: (i, j))]
    out_shape = [jax.ShapeDtypeStruct((m, Q_WIDTH), q_dtype)]
    if paired_kv:
        tiles_per_seq = seq_len // tm
        out_specs.append(pl.BlockSpec((None, tm, KV_WIDTH), lambda j, i: (i // tiles_per_seq, 0, j)))
        out_shape.append(jax.ShapeDtypeStruct((m // seq_len, tm, 2 * KV_WIDTH), F32))
        out_specs.append(pl.BlockSpec((tm, pair_width), lambda j, i: (i, j)))
        out_shape.append(jax.ShapeDtypeStruct((m, 2 * pair_width), BF16))
    else:
        out_specs.append(pl.BlockSpec((tm, KV_WIDTH), lambda j, i: (i, j)))
        out_shape.append(jax.ShapeDtypeStruct((m, 2 * KV_WIDTH), F32))
    rot_spec = pl.BlockSpec((None, tm, V7X_LANES), lambda j, i: (0, i % n_tab, 0))
    rot_or_id_spec = pl.BlockSpec((None, tm, V7X_LANES), lambda j, i: (j, i % n_tab, 0))
    return pl.pallas_call(
        _qkv_kernel,
        grid=(_QKV_STEPS, m // tm),
        in_specs=[
            pl.BlockSpec((tm, D_MODEL), lambda j, i: (i, 0)),
            _row_param_spec(gain_layer, D_MODEL),
            pl.BlockSpec((None, D_MODEL, _Q_TN), lambda j, i: (w_layer, 0, j)),
            pl.BlockSpec((None, D_MODEL, KV_WIDTH), lambda j, i: (w_layer, 0, Q_WIDTH // KV_WIDTH + j)),
            rot_spec, rot_spec, rot_or_id_spec, rot_or_id_spec,
        ],
        out_specs=out_specs,
        out_shape=out_shape,
        compiler_params=_params(("arbitrary", "arbitrary"), vmem),
        name="qkv_rope",
    )(x, _row_param(gain), w, w, cos_t, sin_t, cos_t, sin_t)


def _linear_residual_kernel(o_ref, w_ref, x_ref, out_ref):
    out_ref[...] = x_ref[...] + jnp.dot(o_ref[...].astype(BF16), w_ref[...], preferred_element_type=F32)


def _linear_residual(o, w, layer, x, *, tm):
    m, k = o.shape
    vmem = (
        2 * _nbytes((tm, k), o.dtype)
        + 2 * _nbytes((k, D_MODEL), BF16)
        + 6 * _nbytes((tm, D_MODEL), F32)
    )
    return pl.pallas_call(
        _linear_residual_kernel,
        grid=(m // tm,),
        in_specs=[
            pl.BlockSpec((tm, k), lambda i: (i, 0)),
            pl.BlockSpec((None, k, D_MODEL), lambda i: (layer, 0, 0)),
            pl.BlockSpec((tm, D_MODEL), lambda i: (i, 0)),
        ],
        out_specs=pl.BlockSpec((tm, D_MODEL), lambda i: (i, 0)),
        out_shape=jax.ShapeDtypeStruct((m, D_MODEL), F32),
        compiler_params=_params(("parallel",), vmem),
        name="linear_residual",
    )(o, w, x)


def _head_cols(h):
    return slice(h * HEAD_DIM, (h + 1) * HEAD_DIM)


def _group_sinks(sinks_ref, kvh, g_of_row):
    sink = jnp.full(g_of_row.shape, sinks_ref[kvh * GROUP], F32)
    for g in range(1, GROUP):
        sink = jnp.where(g_of_row == g, sinks_ref[kvh * GROUP + g], sink)
    return sink


def _sink_softmax_weights(s, sink):
    m = jnp.maximum(jnp.max(s, axis=-1, keepdims=True), sink)
    e = jnp.exp(s - m)
    denom = jnp.sum(e, axis=-1, keepdims=True) + jnp.exp(sink - m)
    return e, 1.0 / denom


def _swa_prompt_kernel(sinks_ref, q_ref, k_prev_ref, k_cur_ref, v_prev_ref, v_cur_ref, o_ref):
    n = pl.program_id(1)
    blk = WINDOW
    rows = GROUP * blk
    qi = lax.broadcasted_iota(jnp.int32, (blk, 2 * blk), 0)
    kj = lax.broadcasted_iota(jnp.int32, (blk, 2 * blk), 1)
    d = qi + blk - kj
    mask = ((d >= 0) & (d < WINDOW) & ((kj >= blk) | (n > 0))) | (kj == 0)
    bias = jnp.where(mask, 0.0, -jnp.inf)[None]
    low_half = lax.broadcasted_iota(jnp.int32, (blk, V7X_LANES), 1) < HEAD_DIM
    g_of_row = lax.broadcasted_iota(jnp.int32, (rows, V7X_LANES), 0) // blk
    col = lax.broadcasted_iota(jnp.int32, (rows, V7X_LANES), 1)
    q_onehot = jnp.where((col == g_of_row) | (col == g_of_row + GROUP), 1.0, 0.0).astype(BF16)
    keep_low = jnp.where(low_half, 1.0, 0.0).astype(BF16)
    keep_high = jnp.where(low_half, 0.0, 1.0).astype(BF16)
    head_rows = 2 * V7X_SUBLANES
    is_slot0 = lax.broadcasted_iota(jnp.int32, (head_rows, V7X_LANES), 0) == 0
    kcol = lax.broadcasted_iota(jnp.int32, (1, V7X_LANES), 1)
    ones = jnp.ones((2 * blk, V7X_LANES), BF16)
    no_feat = jnp.zeros((2 * blk - head_rows, V7X_LANES), BF16)

    def without_slot0(prev_ref, cur_ref, kvh):
        head = jnp.where(is_slot0, 0.0, prev_ref[:head_rows, _lane_block(kvh)].astype(F32)).astype(BF16)
        return jnp.concatenate([head, prev_ref[head_rows:, _lane_block(kvh)], cur_ref[:, _lane_block(kvh)]], axis=0)

    for kvh in range(N_KV_HEADS):
        sink_row = jnp.zeros((1, V7X_LANES), F32)
        for g in range(GROUP):
            sink_row = jnp.where((kcol == g) | (kcol == g + GROUP), sinks_ref[kvh * GROUP + g], sink_row)
        sink_hi = sink_row.astype(BF16).astype(F32)
        sink_feat = jnp.where(kcol < GROUP, sink_hi, sink_row - sink_hi)
        feat = jnp.concatenate([jnp.where(is_slot0, sink_feat, 0.0).astype(BF16), no_feat], axis=0)
        k_ext = jnp.concatenate([without_slot0(k_prev_ref, k_cur_ref, kvh), feat], axis=1)
        v_ones = jnp.concatenate([without_slot0(v_prev_ref, v_cur_ref, kvh), ones], axis=1)
        q_heads = []
        for pair in range(GROUP // 2):
            q_pair = q_ref[:, _lane_block(kvh * (GROUP // 2) + pair)]
            q_heads += [q_pair * keep_low, q_pair * keep_high]
        q_ext = jnp.concatenate([jnp.concatenate(q_heads, axis=0), q_onehot], axis=1)
        s = lax.dot_general(q_ext, k_ext, (((1,), (1,)), ((), ())), preferred_element_type=F32)
        s = (s.reshape(GROUP, blk, 2 * blk) + bias).reshape(rows, 2 * blk)
        e = jnp.exp(s - jnp.max(s, axis=-1, keepdims=True)).astype(BF16)
        o_sum = jnp.dot(e, v_ones, preferred_element_type=F32)
        o = o_sum[:, :V7X_LANES] * (1.0 / o_sum[:, V7X_LANES:])
        for pair in range(GROUP // 2):
            even = o[(2 * pair) * blk:(2 * pair + 1) * blk, :]
            odd = o[(2 * pair + 1) * blk:(2 * pair + 2) * blk, :]
            o_ref[:, _lane_block(kvh * (GROUP // 2) + pair)] = jnp.where(low_half, even, odd).astype(o_ref.dtype)


def _swa_prompt(q, kv_pair, sinks):
    nb = SEQ // WINDOW
    pair_width = N_KV_HEADS * V7X_LANES

    def cur(part):
        return pl.BlockSpec((WINDOW, pair_width), lambda b, n: (b * nb + n, part))

    def prev(part):
        return pl.BlockSpec((WINDOW, pair_width), lambda b, n: (b * nb + jnp.maximum(n - 1, 0), part))

    vmem = (
        4 * _nbytes((WINDOW, Q_WIDTH), BF16)
        + 8 * _nbytes((WINDOW, pair_width), BF16)
        + 24 * _nbytes((GROUP * WINDOW, 2 * WINDOW), F32)
    )
    return pl.pallas_call(
        _swa_prompt_kernel,
        grid=(BATCH, nb),
        in_specs=[
            pl.BlockSpec(memory_space=pltpu.SMEM),
            pl.BlockSpec((WINDOW, Q_WIDTH), lambda b, n: (b * nb + n, 0)),
            prev(0), cur(0), prev(1), cur(1),
        ],
        out_specs=pl.BlockSpec((WINDOW, Q_WIDTH), lambda b, n: (b * nb + n, 0)),
        out_shape=jax.ShapeDtypeStruct((BATCH * SEQ, Q_WIDTH), BF16),
        compiler_params=_params(("parallel", "arbitrary"), vmem),
        name="swa_prompt",
    )(sinks, q, kv_pair, kv_pair, kv_pair, kv_pair)


def _swa_sample_kernel(sinks_ref, q_ref, kv_ref, ck_ref, cv_ref, o_ref, nk_ref, nv_ref, kall_scr, vall_scr):
    nb = ck_ref.shape[0]
    t_new = DEC_SEQ
    s_len = WINDOW + t_new
    rows = GROUP * t_new
    kall_scr[:, :WINDOW, :] = ck_ref[...]
    vall_scr[:, :WINDOW, :] = cv_ref[...]
    kall_scr[:, WINDOW:, :] = kv_ref[:, :KV_WIDTH].reshape(nb, t_new, KV_WIDTH)
    vall_scr[:, WINDOW:, :] = kv_ref[:, KV_WIDTH:].reshape(nb, t_new, KV_WIDTH)
    nk_ref[...] = kall_scr[:, t_new:, :]
    nv_ref[...] = vall_scr[:, t_new:, :]

    row = lax.broadcasted_iota(jnp.int32, (1, rows, s_len), 1)
    kj = lax.broadcasted_iota(jnp.int32, (1, rows, s_len), 2)
    d = row % t_new + WINDOW - kj
    mask = (d >= 0) & (d < WINDOW)
    g_of_row = lax.broadcasted_iota(jnp.int32, (1, rows, 1), 1) // t_new
    q3 = q_ref[...].reshape(nb, t_new, Q_WIDTH)
    for kvh in range(N_KV_HEADS):
        k = kall_scr[:, :, _head_cols(kvh)].astype(BF16)
        v = vall_scr[:, :, _head_cols(kvh)].astype(BF16)
        q = jnp.concatenate([q3[:, :, _head_cols(kvh * GROUP + g)] for g in range(GROUP)], axis=1).astype(BF16)
        s = jnp.einsum("bqd,bkd->bqk", q, k, preferred_element_type=F32)
        s = jnp.where(mask, s, -jnp.inf)
        e, inv = _sink_softmax_weights(s, _group_sinks(sinks_ref, kvh, g_of_row))
        o = jnp.einsum("bqk,bkd->bqd", e.astype(BF16), v, preferred_element_type=F32) * inv
        for g in range(GROUP):
            o_ref[:, _head_cols(kvh * GROUP + g)] = o[:, g * t_new:(g + 1) * t_new, :].reshape(nb * t_new, HEAD_DIM)


_SWA_SAMPLE_SEQS_PER_STEP = 8


def _swa_sample(q, kv, cache_k, cache_v, sinks):
    nb = _SWA_SAMPLE_SEQS_PER_STEP
    cache_spec = pl.BlockSpec((nb, WINDOW, KV_WIDTH), lambda b: (b, 0, 0))
    all_keys = (nb, WINDOW + DEC_SEQ, KV_WIDTH)
    vmem = 8 * _nbytes((nb, WINDOW, KV_WIDTH), F32) + 2 * _nbytes(all_keys, F32) + 16 * 1024 * 1024
    return pl.pallas_call(
        _swa_sample_kernel,
        grid=(DEC_BATCH // nb,),
        in_specs=[
            pl.BlockSpec(memory_space=pltpu.SMEM),
            pl.BlockSpec((nb * DEC_SEQ, Q_WIDTH), lambda b: (b, 0)),
            pl.BlockSpec((nb * DEC_SEQ, 2 * KV_WIDTH), lambda b: (b, 0)),
            cache_spec,
            cache_spec,
        ],
        out_specs=[pl.BlockSpec((nb * DEC_SEQ, Q_WIDTH), lambda b: (b, 0)), cache_spec, cache_spec],
        out_shape=[
            jax.ShapeDtypeStruct((N_SAMPLE_ROWS, Q_WIDTH), F32),
            jax.ShapeDtypeStruct((DEC_BATCH, WINDOW, KV_WIDTH), F32),
            jax.ShapeDtypeStruct((DEC_BATCH, WINDOW, KV_WIDTH), F32),
        ],
        scratch_shapes=[pltpu.VMEM(all_keys, F32), pltpu.VMEM(all_keys, F32)],
        compiler_params=_params(("parallel",), vmem),
        name="swa_sample",
    )(sinks, q, kv, cache_k, cache_v)


def _cross_kernel(x_ref, gain_ref, wq_ref, mk_ref, mv_ref, wo_ref, out_ref, *, seqs, rows):
    scale = MEM_HEAD_DIM ** -0.5
    x = x_ref[...]
    h = _rms(x, gain_ref[...]).astype(BF16)
    q = jnp.dot(h, wq_ref[...], preferred_element_type=F32)
    heads = []
    for hd in range(N_MEM_HEADS):
        cs = slice(hd * MEM_HEAD_DIM, (hd + 1) * MEM_HEAD_DIM)
        qh = q[:, cs].reshape(seqs, rows, MEM_HEAD_DIM).astype(BF16)
        if mk_ref.ndim == 4:
            k = mk_ref[:, :, hd, :].astype(BF16)
            v = mv_ref[:, :, hd, :].astype(BF16)
        else:
            k = mk_ref[:, :, cs].astype(BF16)
            v = mv_ref[:, :, cs].astype(BF16)
        s = jnp.einsum("bqd,bkd->bqk", qh, k, preferred_element_type=F32) * scale
        e = jnp.exp(s - jnp.max(s, axis=-1, keepdims=True))
        inv = 1.0 / jnp.sum(e, axis=-1, keepdims=True)
        o = jnp.einsum("bqk,bkd->bqd", e.astype(BF16), v, preferred_element_type=F32) * inv
        heads.append(o.reshape(seqs * rows, MEM_HEAD_DIM))
    o_all = jnp.concatenate(heads, axis=-1).astype(BF16)
    out_ref[...] = x + jnp.dot(o_all, wo_ref[...], preferred_element_type=F32)


def _cross_block(x, gain, wq, mem_k, mem_v, wo, layer, *, n_seq, seqs, rows):
    seq_len = x.shape[0] // n_seq
    nt = seq_len // rows if seqs == 1 else 1
    tile = seqs * rows
    x_spec = pl.BlockSpec((tile, D_MODEL), lambda g, i: (g * nt + i, 0))
    mem_tail = mem_k.shape[3:]
    mem_spec = pl.BlockSpec((None, seqs, N_MEM) + mem_tail, lambda g, i: (layer, g, 0) + (0,) * len(mem_tail))
    mem_pad = V7X_SUBLANES // N_MEM_HEADS if len(mem_tail) == 2 else 1
    vmem = (
        4 * _nbytes((tile, D_MODEL), F32)
        + 4 * mem_pad * _nbytes((seqs, N_MEM, MEM_WIDTH), F32)
        + 4 * _nbytes((D_MODEL, MEM_WIDTH), BF16)
        + 4 * _nbytes((tile, D_MODEL), F32)
        + 8 * _nbytes((tile, N_MEM), F32)
    )
    return pl.pallas_call(
        functools.partial(_cross_kernel, seqs=seqs, rows=rows),
        grid=(n_seq // seqs, nt),
        in_specs=[
            x_spec,
            _row_param_spec(layer, D_MODEL),
            pl.BlockSpec((None, D_MODEL, MEM_WIDTH), lambda g, i: (layer, 0, 0)),
            mem_spec,
            mem_spec,
            pl.BlockSpec((None, MEM_WIDTH, D_MODEL), lambda g, i: (layer, 0, 0)),
        ],
        out_specs=x_spec,
        out_shape=jax.ShapeDtypeStruct(x.shape, F32),
        compiler_params=_params(("parallel", "arbitrary"), vmem),
        name="cross_block",
    )(x, _row_param(gain), wq, mem_k, mem_v, wo)


def _log_sigmoid(x):
    return jnp.minimum(x, 0.0) - jnp.log1p(jnp.exp(-jnp.abs(x)))


def _one_minus_exp2(y, exp_y):
    return jnp.tanh(-y) * (1.0 + exp_y * exp_y)


def _lru_gates(xc, wa_ref, ba_ref, wx_ref, bx_ref, lam_ref):
    r_parts = []
    i_parts = []
    for n in range(LRU_BLOCKS):
        cs = slice(n * LRU_BLOCK_SIZE, (n + 1) * LRU_BLOCK_SIZE)
        xb = xc[:, cs].astype(BF16)
        r_parts.append(jnp.dot(xb, wa_ref[n], preferred_element_type=F32))
        i_parts.append(jnp.dot(xb, wx_ref[n], preferred_element_type=F32))
    r = jax.nn.sigmoid(jnp.concatenate(r_parts, axis=-1) + ba_ref[...])
    ig = jax.nn.sigmoid(jnp.concatenate(i_parts, axis=-1) + bx_ref[...])
    log_a = r * (LRU_C * _log_sigmoid(lam_ref[...]))
    a = jnp.exp(log_a)
    z = _one_minus_exp2(log_a, a)
    u = jnp.where(z > 0.0, z * lax.rsqrt(z), 0.0) * (ig * xc)
    return a, u


def _lru_prompt_kernel(xb_ref, gate_ref, x_ref, wc_ref, bc_ref, wa_ref, ba_ref, wx_ref, bx_ref, lam_ref,
                       wo_ref, out_ref, hlast_ref, xext_scr, a_scr, u_scr, h_scr):
    i = pl.program_id(1)
    tm = xb_ref.shape[0]
    pad = V7X_SUBLANES

    @pl.when(i == 0)
    def _():
        xext_scr[0:pad, :] = jnp.zeros((pad, LRU_WIDTH), F32)
        h_scr[...] = jnp.zeros((1, LRU_WIDTH), F32)

    xext_scr[pad:pad + tm, :] = xb_ref[...]
    xc = bc_ref[...] + wc_ref[CONV_WIDTH - 1:CONV_WIDTH, :] * xb_ref[...]
    for k in range(1, CONV_WIDTH):
        xc = xc + wc_ref[CONV_WIDTH - 1 - k:CONV_WIDTH - k, :] * xext_scr[pad - k:pad - k + tm, :]
    xext_scr[0:pad, :] = xext_scr[tm:tm + pad, :]

    a, u = _lru_gates(xc, wa_ref, ba_ref, wx_ref, bx_ref, lam_ref)
    a_scr[...] = a
    u_scr[...] = u

    def step(t, h):
        h = a_scr[pl.ds(t, 1), :] * h + u_scr[pl.ds(t, 1), :]
        u_scr[pl.ds(t, 1), :] = h
        return h

    h_scr[...] = lax.fori_loop(0, tm, step, h_scr[...], unroll=8)
    hlast_ref[...] = u_scr[tm - pad:tm, :]
    y = (u_scr[...] * gate_ref[...]).astype(BF16)
    out_ref[...] = x_ref[...] + jnp.dot(y, wo_ref[...], preferred_element_type=F32)


def _lru_weight_specs(layer_idx):
    def const3(*_):
        return (layer_idx, 0, 0)

    def const4(*_):
        return (layer_idx, 0, 0, 0)

    gate_w = pl.BlockSpec((None, LRU_BLOCKS, LRU_BLOCK_SIZE, LRU_BLOCK_SIZE), const4)
    vec = _row_param_spec(layer_idx, LRU_WIDTH)
    return [
        pl.BlockSpec((None, CONV_WIDTH, LRU_WIDTH), const3),
        vec,
        gate_w,
        vec,
        gate_w,
        vec,
        vec,
        pl.BlockSpec((None, LRU_WIDTH, D_MODEL), const3),
    ]


def _lru_weight_args(w_conv, b_conv, w_gate_a, b_gate_a, w_gate_x, b_gate_x, lam, w_out):
    return (w_conv, _row_param(b_conv), w_gate_a, _row_param(b_gate_a), w_gate_x, _row_param(b_gate_x),
            _row_param(lam), w_out)


def _lru_prompt(xb, gate, x, lw, j, *, tm):
    nt = SEQ // tm
    row_spec0 = pl.BlockSpec((tm, LRU_WIDTH), lambda b, i: (b * nt + i, 0))
    vmem = (
        8 * _nbytes((tm, LRU_WIDTH), F32)
        + 3 * _nbytes((tm + V7X_SUBLANES, LRU_WIDTH), F32)
        + 2 * _nbytes((LRU_WIDTH, D_MODEL), BF16)
        + 4 * _nbytes((LRU_BLOCKS, LRU_BLOCK_SIZE, LRU_BLOCK_SIZE), BF16)
        + 10 * _nbytes((tm, LRU_WIDTH), F32)
    )
    return pl.pallas_call(
        _lru_prompt_kernel,
        grid=(BATCH, nt),
        in_specs=[row_spec0, row_spec0, row_spec0] + _lru_weight_specs(j),
        out_specs=[
            row_spec0,
            pl.BlockSpec((None, V7X_SUBLANES, LRU_WIDTH), lambda b, i: (b, 0, 0)),
        ],
        out_shape=[
            jax.ShapeDtypeStruct((BATCH * SEQ, D_MODEL), F32),
            jax.ShapeDtypeStruct((BATCH, V7X_SUBLANES, LRU_WIDTH), F32),
        ],
        scratch_shapes=[
            pltpu.VMEM((tm + V7X_SUBLANES, LRU_WIDTH), F32),
            pltpu.VMEM((tm, LRU_WIDTH), F32),
            pltpu.VMEM((tm, LRU_WIDTH), F32),
            pltpu.VMEM((1, LRU_WIDTH), F32),
        ],
        compiler_params=_params(("arbitrary", "arbitrary"), vmem),
        name="lru_prompt",
    )(xb, gate, x, *lw)


def _lru_sample_kernel(xb_ref, gate_ref, x_ref, cbuf_ref, h0_ref, wc_ref, bc_ref, wa_ref, ba_ref, wx_ref, bx_ref,
                       lam_ref, wo_ref, out_ref, hs_ref):
    rows = xb_ref.shape[0]
    t = lax.broadcasted_iota(jnp.int32, (rows, LRU_WIDTH), 0) % DEC_SEQ
    xb = xb_ref[...]
    cbuf = cbuf_ref[...]
    xc = bc_ref[...] + wc_ref[CONV_WIDTH - 1:CONV_WIDTH, :] * xb
    for k in range(1, CONV_WIDTH):
        shift_c = CONV_WIDTH - 1 - k
        from_buf = cbuf if shift_c == 0 else pltpu.roll(cbuf, rows - shift_c, 0)
        xk = jnp.where(t >= k, pltpu.roll(xb, k, 0), from_buf)
        xc = xc + wc_ref[CONV_WIDTH - 1 - k:CONV_WIDTH - k, :] * xk

    a, u = _lru_gates(xc, wa_ref, ba_ref, wx_ref, bx_ref, lam_ref)
    k = 1
    while k < DEC_SEQ:
        m = t >= k
        u = jnp.where(m, a * pltpu.roll(u, k, 0) + u, u)
        a = jnp.where(m, a * pltpu.roll(a, k, 0), a)
        k *= 2
    hs = u + a * h0_ref[...]
    hs_ref[...] = hs
    y = (hs * gate_ref[...]).astype(BF16)
    out_ref[...] = x_ref[...] + jnp.dot(y, wo_ref[...], preferred_element_type=F32)


def _lru_sample(xb, gate, x, cbuf_rows, h0_rows, lw, j):
    rows = N_SAMPLE_ROWS
    row_spec0 = pl.BlockSpec((rows, LRU_WIDTH), lambda i: (0, 0))
    vmem = 40 * _nbytes((rows, LRU_WIDTH), F32) + 2 * _nbytes((LRU_WIDTH, D_MODEL), BF16)
    return pl.pallas_call(
        _lru_sample_kernel,
        grid=(1,),
        in_specs=[row_spec0] * 5 + _lru_weight_specs(j),
        out_specs=[row_spec0, row_spec0],
        out_shape=[
            jax.ShapeDtypeStruct((rows, D_MODEL), F32),
            jax.ShapeDtypeStruct((rows, LRU_WIDTH), F32),
        ],
        compiler_params=_params(("arbitrary",), vmem),
        name="lru_sample",
    )(xb, gate, x, cbuf_rows, h0_rows, *lw)


_FFN_TM = 1024
_FFN_TF = 512
_FFN_ROW_CHUNK = 512


def kernel(x_prompt, x_sample, cache_swa_k, cache_swa_v, state_lru_h, state_lru_conv, cache_mem_k, cache_mem_v,
           mem_prompt, ln_ffn1, ffn1_w_in, ffn1_w_out, ln_mix, swa_w_qkv, swa_w_o, swa_sinks, lru_w_in,
           lru_w_conv, lru_b_conv, lru_w_gate_a, lru_b_gate_a, lru_w_gate_x, lru_b_gate_x, lru_lambda,
           lru_w_out, ln_cross, ln_mem, cross_w_q, cross_w_kv, cross_w_o, ln_ffn2, ffn2_w_in, ffn2_w_out,
           ln_final):
    bf = lambda w: w.astype(BF16)
    ffn_seq = []
    for layer in range(DEPTH):
        ffn_seq += [(ffn1_w_in, ffn1_w_out, layer), (ffn2_w_in, ffn2_w_out, layer)]
    ffn_w = (bf(ffn1_w_in[0]), bf(ffn1_w_out[0]))

    def ffn(x, x_extra, gain, layer, final_gain=None):
        nonlocal ffn_w
        nxt = ffn_seq[1] if len(ffn_seq) > 1 else None
        out = _ffn(x, x_extra, gain, layer, *ffn_w, tm=_FFN_TM, tf=_FFN_TF, row_chunk=_FFN_ROW_CHUNK,
                   final_gain=final_gain, cast_next=nxt)
        del ffn_seq[0]
        if nxt is not None:
            ffn_w = out[2:]
        return out[0], out[1]

    swa_w_qkv, swa_w_o = bf(swa_w_qkv), bf(swa_w_o)
    lru_w_in, lru_w_out = bf(lru_w_in), bf(lru_w_out)
    lru_w_gate_a, lru_w_gate_x = bf(lru_w_gate_a), bf(lru_w_gate_x)
    cross_w_q, cross_w_kv, cross_w_o = bf(cross_w_q), bf(cross_w_kv), bf(cross_w_o)

    xp = x_prompt.reshape(N_PROMPT_ROWS, D_MODEL)
    xs = x_sample.reshape(N_SAMPLE_ROWS, D_MODEL)

    mem_rows = mem_prompt.reshape(BATCH * N_MEM, D_MODEL)
    mkv = [
        _norm_linear(mem_rows, ln_mem, layer, cross_w_kv, layer, tm=BATCH * N_MEM, tn=MEM_WIDTH, out_dtype=F32)
        for layer in range(DEPTH)
    ]
    mem_k_prompt = jnp.stack([m[:, :MEM_WIDTH] for m in mkv]).reshape(DEPTH, BATCH, N_MEM, MEM_WIDTH)
    mem_v_prompt = jnp.stack([m[:, MEM_WIDTH:] for m in mkv]).reshape(DEPTH, BATCH, N_MEM, MEM_WIDTH)

    cos_p, sin_p = _rope_tables(jnp.arange(SEQ, dtype=jnp.int32))
    cos_s, sin_s = _rope_tables(PAST_LEN + jnp.arange(N_SAMPLE_ROWS, dtype=jnp.int32) % DEC_SEQ)

    swa_k_p, swa_v_p, swa_k_s, swa_v_s = [], [], [], []
    lru_h_p, lru_conv_p, lru_h_s, lru_conv_s = [], [], [], []

    for layer in range(DEPTH):
        j = layer // 2
        xp, xs = ffn(xp, xs, ln_ffn1, layer)

        if layer % 2 == 0:
            qkv_tm = 512
            q_p, kv_last, kv_pair = _qkv(xp, ln_mix, layer, swa_w_qkv, j, cos_p, sin_p, tm=qkv_tm, q_dtype=BF16,
                                         seq_len=SEQ)
            o_p = _swa_prompt(q_p, kv_pair, swa_sinks[j])
            xp = _linear_residual(o_p, swa_w_o, j, xp, tm=512)
            kv_tail = kv_last[:, qkv_tm - WINDOW:, :]
            swa_k_p.append(kv_tail[..., :KV_WIDTH].reshape(BATCH, WINDOW, N_KV_HEADS, HEAD_DIM))
            swa_v_p.append(kv_tail[..., KV_WIDTH:].reshape(BATCH, WINDOW, N_KV_HEADS, HEAD_DIM))

            q_s, kv_s = _qkv(xs, ln_mix, layer, swa_w_qkv, j, cos_s, sin_s, tm=N_SAMPLE_ROWS, q_dtype=F32)
            o_s, nk, nv = _swa_sample(
                q_s, kv_s,
                cache_swa_k[j].reshape(DEC_BATCH, WINDOW, KV_WIDTH),
                cache_swa_v[j].reshape(DEC_BATCH, WINDOW, KV_WIDTH),
                swa_sinks[j],
            )
            xs = _linear_residual(o_s, swa_w_o, j, xs, tm=N_SAMPLE_ROWS)
            swa_k_s.append(nk.reshape(DEC_BATCH, WINDOW, N_KV_HEADS, HEAD_DIM))
            swa_v_s.append(nv.reshape(DEC_BATCH, WINDOW, N_KV_HEADS, HEAD_DIM))
        else:
            lw = _lru_weight_args(lru_w_conv, lru_b_conv, lru_w_gate_a, lru_b_gate_a, lru_w_gate_x, lru_b_gate_x,
                                  lru_lambda, lru_w_out)
            xb_p, gate_p = _lru_in(xp, ln_mix, layer, lru_w_in, j, tm=512, tn=1024)
            xp, h_last = _lru_prompt(xb_p, gate_p, xp, lw, j, tm=256)
            lru_h_p.append(h_last[:, V7X_SUBLANES - 1, :])
            lru_conv_p.append(xb_p.reshape(BATCH, SEQ, LRU_WIDTH)[:, SEQ - (CONV_WIDTH - 1):, :])

            xb_s, gate_s = _lru_in(xs, ln_mix, layer, lru_w_in, j, tm=N_SAMPLE_ROWS, tn=1024)
            cbuf_rows = jnp.pad(
                state_lru_conv[j], ((0, 0), (0, DEC_SEQ - (CONV_WIDTH - 1)), (0, 0))
            ).reshape(N_SAMPLE_ROWS, LRU_WIDTH)
            h0_rows = jnp.repeat(state_lru_h[j], DEC_SEQ, axis=0)
            xs, hs_s = _lru_sample(xb_s, gate_s, xs, cbuf_rows, h0_rows, lw, j)
            lru_h_s.append(hs_s.reshape(DEC_BATCH, DEC_SEQ, LRU_WIDTH)[:, DEC_SEQ - 1, :])
            xp_tail = jnp.concatenate(
                [state_lru_conv[j], xb_s.reshape(DEC_BATCH, DEC_SEQ, LRU_WIDTH)], axis=1)
            lru_conv_s.append(xp_tail[:, -(CONV_WIDTH - 1):, :])

        xp = _cross_block(xp, ln_cross, cross_w_q, mem_k_prompt, mem_v_prompt, cross_w_o, layer,
                          n_seq=BATCH, seqs=1, rows=512)
        xs = _cross_block(xs, ln_cross, cross_w_q, cache_mem_k, cache_mem_v, cross_w_o, layer,
                          n_seq=DEC_BATCH, seqs=8, rows=DEC_SEQ)

        fin = ln_final if layer == DEPTH - 1 else None
        xp, xs = ffn(xp, xs, ln_ffn2, layer, final_gain=fin)

    return (
        xp.reshape(BATCH, SEQ, D_MODEL),
        xs.reshape(DEC_BATCH, DEC_SEQ, D_MODEL),
        jnp.stack(swa_k_p), jnp.stack(swa_v_p), jnp.stack(swa_k_s), jnp.stack(swa_v_s),
        jnp.stack(lru_h_p), jnp.stack(lru_conv_p), jnp.stack(lru_h_s), jnp.stack(lru_conv_s),
        mem_k_prompt.reshape(DEPTH, BATCH, N_MEM, N_MEM_HEADS, MEM_HEAD_DIM),
        mem_v_prompt.reshape(DEPTH, BATCH, N_MEM, N_MEM_HEADS, MEM_HEAD_DIM),
    )
```

```python
import functools
import math

import jax
import jax.numpy as jnp
from jax import lax
from jax.experimental import pallas as pl
from jax.experimental.pallas import tpu as pltpu

F32 = jnp.float32
BF16 = jnp.bfloat16

D_MODEL = 2048
BATCH = 2
SEQ = 4096
DEPTH = 2
DEC_BATCH = 32
DEC_SEQ = 8
PAST_LEN = 16384
N_HEADS = 32
N_KV_HEADS = 8
HEAD_DIM = 64
GROUP = N_HEADS // N_KV_HEADS
WINDOW = 128
ROPE_THETA = 10000.0
LRU_WIDTH = D_MODEL
LRU_BLOCKS = 8
LRU_BLOCK_SIZE = LRU_WIDTH // LRU_BLOCKS
CONV_WIDTH = 4
LRU_C = 8.0
D_FF = 5632
N_MEM = 256
N_MEM_HEADS = 4
MEM_HEAD_DIM = 128
MEM_WIDTH = N_MEM_HEADS * MEM_HEAD_DIM
NORM_EPS = 1e-6
Q_WIDTH = N_HEADS * HEAD_DIM
KV_WIDTH = N_KV_HEADS * HEAD_DIM

V7X_LANES = 128
V7X_SUBLANES = 8
V7X_VMEM_BYTES = 64 * 1024 * 1024
V7X_VMEM_USABLE_BYTES = V7X_VMEM_BYTES - 2 * 1024 * 1024

N_PROMPT_ROWS = BATCH * SEQ
N_SAMPLE_ROWS = DEC_BATCH * DEC_SEQ


def _params(semantics, vmem_bytes):
    return pltpu.CompilerParams(
        dimension_semantics=semantics,
        vmem_limit_bytes=int(min(max(vmem_bytes, 32 * 1024 * 1024), V7X_VMEM_USABLE_BYTES)),
    )


def _nbytes(shape, dtype):
    return math.prod(shape) * jnp.dtype(dtype).itemsize


def _rms(x, gain):
    ms = jnp.mean(x * x, axis=-1, keepdims=True)
    return x * lax.rsqrt(ms + NORM_EPS) * gain


def _lane_block(c):
    return slice(c * V7X_LANES, (c + 1) * V7X_LANES)


def _row_param(p):
    return p.reshape(p.shape[0], 1, p.shape[-1])


def _row_param_spec(layer, width):
    return pl.BlockSpec((None, 1, width), lambda *_: (layer, 0, 0))


def _for_chunks(n_rows, chunk, body):
    n_chunks = n_rows // chunk
    if n_chunks == 1:
        body(pl.ds(0, n_rows))
        return

    def step(c, carry):
        body(pl.ds(pl.multiple_of(c * chunk, chunk), chunk))
        return carry

    lax.fori_loop(0, n_chunks, step, 0)


def _ffn_kernel(x_ref, gain_ref, wg_ref, wu_ref, wo_ref, xe_ref, *rest, row_chunk, final_norm, cast_next):
    rest = list(rest)
    gfin_ref = rest.pop(0) if final_norm else None
    next_in_ref, next_out_ref = (rest.pop(0), rest.pop(0)) if cast_next else (None, None)
    o_ref, oe_ref = rest.pop(0), rest.pop(0)
    next_in_bf_ref, next_out_bf_ref = (rest.pop(0), rest.pop(0)) if cast_next else (None, None)
    h_scr, he_scr = rest
    f = pl.program_id(1)

    if cast_next:
        next_in_bf_ref[...] = next_in_ref[...].astype(BF16)
        next_out_bf_ref[...] = next_out_ref[...].astype(BF16)

    def run(x_ref, o_ref, h_scr, chunk):
        n_rows = x_ref.shape[0]

        @pl.when(f == 0)
        def _():
            def prologue(rows):
                h_scr[rows, :] = _rms(x_ref[rows, :], gain_ref[...]).astype(BF16)
                o_ref[rows, :] = jnp.zeros((chunk, D_MODEL), F32)

            _for_chunks(n_rows, chunk, prologue)

        chunks = [pl.ds(c * chunk, chunk) for c in range(n_rows // chunk)]
        acts = []
        for rows in chunks:
            h = h_scr[rows, :]
            g = jnp.dot(h, wg_ref[...], preferred_element_type=F32)
            u = jnp.dot(h, wu_ref[...], preferred_element_type=F32)
            acts.append((g * jax.nn.sigmoid(g) * u).astype(BF16))
        for rows, a in zip(chunks, acts):
            o_ref[rows, :] += jnp.dot(a, wo_ref[...], preferred_element_type=F32)

        @pl.when(f == pl.num_programs(1) - 1)
        def _():
            def epilogue(rows):
                y = x_ref[rows, :] + 0.5 * o_ref[rows, :]
                if final_norm:
                    y = _rms(y, gfin_ref[...])
                o_ref[rows, :] = y

            _for_chunks(n_rows, chunk, epilogue)

    run(x_ref, o_ref, h_scr, row_chunk)

    @pl.when(pl.program_id(0) == 0)
    def _():
        run(xe_ref, oe_ref, he_scr, xe_ref.shape[0])


def _ffn(x, x_extra, gain, layer, w_in, w_out, *, tm, tf, row_chunk, final_gain=None, cast_next=None):
    m = x.shape[0]
    me = x_extra.shape[0]
    n_tiles = m // tm
    nf = D_FF // tf
    final_norm = final_gain is not None
    row_spec = pl.BlockSpec((tm, D_MODEL), lambda i, f: (i, 0))
    extra_spec = pl.BlockSpec((me, D_MODEL), lambda i, f: (0, 0))
    in_specs = [
        row_spec,
        _row_param_spec(layer, D_MODEL),
        pl.BlockSpec((D_MODEL, tf), lambda i, f: (0, f)),
        pl.BlockSpec((D_MODEL, tf), lambda i, f: (0, f + nf)),
        pl.BlockSpec((tf, D_MODEL), lambda i, f: (f, 0)),
        pl.BlockSpec((me, D_MODEL), lambda i, f: (0, 0), pipeline_mode=pl.Buffered(1)),
    ]
    args = [x, _row_param(gain), w_in, w_in, w_out, x_extra]
    out_specs = [row_spec, extra_spec]
    out_shape = [jax.ShapeDtypeStruct((m, D_MODEL), F32), jax.ShapeDtypeStruct((me, D_MODEL), F32)]
    if final_norm:
        in_specs.append(_row_param_spec(0, D_MODEL))
        args.append(final_gain.reshape(1, 1, D_MODEL))
    vmem = (
        4 * _nbytes((tm, D_MODEL), F32)
        + _nbytes((tm, D_MODEL), BF16)
        + 4 * _nbytes((me, D_MODEL), F32)
        + _nbytes((me, D_MODEL), BF16)
        + 2 * 3 * _nbytes((D_MODEL, tf), BF16)
        + 4 * _nbytes((row_chunk, tf), F32)
        + 2 * _nbytes((row_chunk, D_MODEL), F32)
    )
    if cast_next is not None:
        nw_in, nw_out, nl = cast_next
        in_blk = (D_MODEL // n_tiles, 2 * D_FF // nf)
        out_blk = (D_FF // nf, D_MODEL // n_tiles)
        in_specs += [
            pl.BlockSpec((None,) + in_blk, lambda i, f: (nl, i, f)),
            pl.BlockSpec((None,) + out_blk, lambda i, f: (nl, f, i)),
        ]
        args += [nw_in, nw_out]
        out_specs += [pl.BlockSpec(in_blk, lambda i, f: (i, f)), pl.BlockSpec(out_blk, lambda i, f: (f, i))]
        out_shape += [jax.ShapeDtypeStruct(nw_in.shape[1:], BF16), jax.ShapeDtypeStruct(nw_out.shape[1:], BF16)]
        vmem += 2 * (_nbytes(in_blk, F32) + _nbytes(in_blk, BF16) + _nbytes(out_blk, F32) + _nbytes(out_blk, BF16))
    return pl.pallas_call(
        functools.partial(_ffn_kernel, row_chunk=row_chunk, final_norm=final_norm, cast_next=cast_next is not None),
        grid=(n_tiles, nf),
        in_specs=in_specs,
        out_specs=out_specs,
        out_shape=out_shape,
        scratch_shapes=[pltpu.VMEM((tm, D_MODEL), BF16), pltpu.VMEM((me, D_MODEL), BF16)],
        compiler_params=_params(("arbitrary", "arbitrary"), vmem),
        name="ffn",
    )(*args)


def _gelu_tanh(y):
    return 0.5 * y * (1.0 + jnp.tanh(math.sqrt(2.0 / math.pi) * (y + 0.044715 * (y * y * y))))


def _norm_linear_kernel(x_ref, gain_ref, w_ref, o_ref, h_scr):
    @pl.when(pl.program_id(1) == 0)
    def _():
        h_scr[...] = _rms(x_ref[...], gain_ref[...]).astype(BF16)

    o_ref[...] = jnp.dot(h_scr[...], w_ref[...], preferred_element_type=F32).astype(o_ref.dtype)


def _lru_in_kernel(x_ref, gain_ref, wx_ref, wg_ref, xb_ref, gate_ref, h_scr):
    @pl.when(pl.program_id(1) == 0)
    def _():
        h_scr[...] = _rms(x_ref[...], gain_ref[...]).astype(BF16)

    h = h_scr[...]
    xb_ref[...] = jnp.dot(h, wx_ref[...], preferred_element_type=F32)
    gate_ref[...] = _gelu_tanh(jnp.dot(h, wg_ref[...], preferred_element_type=F32))


def _lru_in(x, gain, gain_layer, w, w_layer, *, tm, tn):
    m = x.shape[0]
    n_col = LRU_WIDTH // tn
    vmem = (
        2 * _nbytes((tm, D_MODEL), F32)
        + _nbytes((tm, D_MODEL), BF16)
        + 4 * _nbytes((D_MODEL, tn), BF16)
        + 10 * _nbytes((tm, tn), F32)
    )
    out_spec = pl.BlockSpec((tm, tn), lambda i, j: (i, j))
    return pl.pallas_call(
        _lru_in_kernel,
        grid=(m // tm, n_col),
        in_specs=[
            pl.BlockSpec((tm, D_MODEL), lambda i, j: (i, 0)),
            _row_param_spec(gain_layer, D_MODEL),
            pl.BlockSpec((None, D_MODEL, tn), lambda i, j: (w_layer, 0, j)),
            pl.BlockSpec((None, D_MODEL, tn), lambda i, j: (w_layer, 0, n_col + j)),
        ],
        out_specs=[out_spec, out_spec],
        out_shape=[jax.ShapeDtypeStruct((m, LRU_WIDTH), F32)] * 2,
        scratch_shapes=[pltpu.VMEM((tm, D_MODEL), BF16)],
        compiler_params=_params(("parallel", "arbitrary"), vmem),
        name="lru_in",
    )(x, _row_param(gain), w, w)


def _norm_linear(x, gain, gain_layer, w, w_layer, *, tm, tn, out_dtype):
    m = x.shape[0]
    n = w.shape[-1]
    vmem = (
        2 * _nbytes((tm, D_MODEL), F32)
        + _nbytes((tm, D_MODEL), BF16)
        + 2 * _nbytes((D_MODEL, tn), BF16)
        + 4 * _nbytes((tm, tn), F32)
    )
    return pl.pallas_call(
        _norm_linear_kernel,
        grid=(m // tm, n // tn),
        in_specs=[
            pl.BlockSpec((tm, D_MODEL), lambda i, j: (i, 0)),
            _row_param_spec(gain_layer, D_MODEL),
            pl.BlockSpec((None, D_MODEL, tn), lambda i, j: (w_layer, 0, j)),
        ],
        out_specs=pl.BlockSpec((tm, tn), lambda i, j: (i, j)),
        out_shape=jax.ShapeDtypeStruct((m, n), out_dtype),
        scratch_shapes=[pltpu.VMEM((tm, D_MODEL), BF16)],
        compiler_params=_params(("parallel", "arbitrary"), vmem),
        name="norm_linear",
    )(x, _row_param(gain), w)


_QKV_STEPS = 2
_Q_TN = Q_WIDTH // _QKV_STEPS
_ATTN_SCALE = HEAD_DIM ** -0.5
assert math.frexp(_ATTN_SCALE)[0] == 0.5
assert 2 * KV_WIDTH // _QKV_STEPS == KV_WIDTH


def _rope_tables(positions):
    half = HEAD_DIM // 2
    inv = ROPE_THETA ** (-jnp.arange(half, dtype=F32) * (2.0 / HEAD_DIM))
    ang = positions.astype(F32)[:, None] * inv[None, :]
    cos = jnp.cos(ang)
    sin = jnp.sin(ang)
    reps = V7X_LANES // HEAD_DIM
    cos_t = jnp.tile(jnp.concatenate([cos, cos], axis=-1), (1, reps))
    sin_t = jnp.tile(jnp.concatenate([-sin, sin], axis=-1), (1, reps))
    return jnp.stack([cos_t, jnp.ones_like(cos_t)]), jnp.stack([sin_t, jnp.zeros_like(sin_t)])


def _qkv_kernel(x_ref, gain_ref, wq_ref, wkv_ref, cos_ref, sin_ref, cos_kv_ref, sin_kv_ref, q_ref, kv_ref,
                pair_ref=None):
    tm = x_ref.shape[0]
    lane = lax.broadcasted_iota(jnp.int32, (tm, V7X_LANES), 1)
    first_half = (lane % HEAD_DIM) < (HEAD_DIM // 2)
    low_half = lane < HEAD_DIM

    def rope(yc, cos, sin):
        rot = jnp.where(
            first_half,
            pltpu.roll(yc, V7X_LANES - HEAD_DIM // 2, 1),
            pltpu.roll(yc, HEAD_DIM // 2, 1),
        )
        return yc * cos + rot * sin

    h = _rms(x_ref[...], gain_ref[...]).astype(BF16)
    yq = jnp.dot(h, wq_ref[...], preferred_element_type=F32)
    cos, sin = cos_ref[...], sin_ref[...]
    for c in range(_Q_TN // V7X_LANES):
        q_ref[:, _lane_block(c)] = (rope(yq[:, _lane_block(c)], cos, sin) * _ATTN_SCALE).astype(q_ref.dtype)

    ykv = jnp.dot(h, wkv_ref[...], preferred_element_type=F32)
    cos_kv, sin_kv = cos_kv_ref[...], sin_kv_ref[...]
    heads_per_block = V7X_LANES // HEAD_DIM
    for c in range(KV_WIDTH // V7X_LANES):
        r = rope(ykv[:, _lane_block(c)], cos_kv, sin_kv)
        kv_ref[:, _lane_block(c)] = r
        if pair_ref is not None:
            swapped = pltpu.roll(r, HEAD_DIM, 1)
            pair_ref[:, _lane_block(heads_per_block * c)] = jnp.where(low_half, r, swapped).astype(BF16)
            pair_ref[:, _lane_block(heads_per_block * c + 1)] = jnp.where(low_half, swapped, r).astype(BF16)


def _qkv(x, gain, gain_layer, w, w_layer, cos_t, sin_t, *, tm, q_dtype, seq_len=None):
    m = x.shape[0]
    n_tab = cos_t.shape[1] // tm
    paired_kv = seq_len is not None
    pair_width = N_KV_HEADS * V7X_LANES
    vmem = (
        2 * _nbytes((tm, D_MODEL), F32)
        + _nbytes((tm, D_MODEL), BF16)
        + 2 * _nbytes((D_MODEL, _Q_TN + KV_WIDTH), BF16)
        + 8 * _nbytes((tm, _Q_TN + KV_WIDTH), F32)
        + 4 * _nbytes((tm, pair_width), BF16)
    )
    out_specs = [pl.BlockSpec((tm, _Q_TN), lambda j, i: (i, j))]
    out_shape = [jax.ShapeDtypeStruct((m, Q_WIDTH), q_dtype)]
    if paired_kv:
        tiles_per_seq = seq_len // tm
        out_specs.append(pl.BlockSpec((None, tm, KV_WIDTH), lambda j, i: (i // tiles_per_seq, 0, j)))
        out_shape.append(jax.ShapeDtypeStruct((m // seq_len, tm, 2 * KV_WIDTH), F32))
        out_specs.append(pl.BlockSpec((tm, pair_width), lambda j, i: (i, j)))
        out_shape.append(jax.ShapeDtypeStruct((m, 2 * pair_width), BF16))
    else:
        out_specs.append(pl.BlockSpec((tm, KV_WIDTH), lambda j, i: (i, j)))
        out_shape.append(jax.ShapeDtypeStruct((m, 2 * KV_WIDTH), F32))
    rot_spec = pl.BlockSpec((None, tm, V7X_LANES), lambda j, i: (0, i % n_tab, 0))
    rot_or_id_spec = pl.BlockSpec((None, tm, V7X_LANES), lambda j, i: (j, i % n_tab, 0))
    return pl.pallas_call(
        _qkv_kernel,
        grid=(_QKV_STEPS, m // tm),
        in_specs=[
            pl.BlockSpec((tm, D_MODEL), lambda j, i: (i, 0)),
            _row_param_spec(gain_layer, D_MODEL),
            pl.BlockSpec((None, D_MODEL, _Q_TN), lambda j, i: (w_layer, 0, j)),
            pl.BlockSpec((None, D_MODEL, KV_WIDTH), lambda j, i: (w_layer, 0, Q_WIDTH // KV_WIDTH + j)),
            rot_spec, rot_spec, rot_or_id_spec, rot_or_id_spec,
        ],
        out_specs=out_specs,
        out_shape=out_shape,
        compiler_params=_params(("arbitrary", "arbitrary"), vmem),
        name="qkv_rope",
    )(x, _row_param(gain), w, w, cos_t, sin_t, cos_t, sin_t)


def _linear_residual_kernel(o_ref, w_ref, x_ref, out_ref):
    out_ref[...] = x_ref[...] + jnp.dot(o_ref[...].astype(BF16), w_ref[...], preferred_element_type=F32)


def _linear_residual(o, w, layer, x, *, tm):
    m, k = o.shape
    vmem = (
        2 * _nbytes((tm, k), o.dtype)
        + 2 * _nbytes((k, D_MODEL), BF16)
        + 6 * _nbytes((tm, D_MODEL), F32)
    )
    return pl.pallas_call(
        _linear_residual_kernel,
        grid=(m // tm,),
        in_specs=[
            pl.BlockSpec((tm, k), lambda i: (i, 0)),
            pl.BlockSpec((None, k, D_MODEL), lambda i: (layer, 0, 0)),
            pl.BlockSpec((tm, D_MODEL), lambda i: (i, 0)),
        ],
        out_specs=pl.BlockSpec((tm, D_MODEL), lambda i: (i, 0)),
        out_shape=jax.ShapeDtypeStruct((m, D_MODEL), F32),
        compiler_params=_params(("parallel",), vmem),
        name="linear_residual",
    )(o, w, x)


def _head_cols(h):
    return slice(h * HEAD_DIM, (h + 1) * HEAD_DIM)


def _group_sinks(sinks_ref, kvh, g_of_row):
    sink = jnp.full(g_of_row.shape, sinks_ref[kvh * GROUP], F32)
    for g in range(1, GROUP):
        sink = jnp.where(g_of_row == g, sinks_ref[kvh * GROUP + g], sink)
    return sink


def _sink_softmax_weights(s, sink):
    m = jnp.maximum(jnp.max(s, axis=-1, keepdims=True), sink)
    e = jnp.exp(s - m)
    denom = jnp.sum(e, axis=-1, keepdims=True) + jnp.exp(sink - m)
    return e, 1.0 / denom


def _swa_prompt_kernel(sinks_ref, q_ref, k_prev_ref, k_cur_ref, v_prev_ref, v_cur_ref, o_ref):
    n = pl.program_id(1)
    blk = WINDOW
    rows = GROUP * blk
    qi = lax.broadcasted_iota(jnp.int32, (blk, 2 * blk), 0)
    kj = lax.broadcasted_iota(jnp.int32, (blk, 2 * blk), 1)
    d = qi + blk - kj
    mask = ((d >= 0) & (d < WINDOW) & ((kj >= blk) | (n > 0))) | (kj == 0)
    bias = jnp.where(mask, 0.0, -jnp.inf)[None]
    low_half = lax.broadcasted_iota(jnp.int32, (blk, V7X_LANES), 1) < HEAD_DIM
    g_of_row = lax.broadcasted_iota(jnp.int32, (rows, V7X_LANES), 0) // blk
    col = lax.broadcasted_iota(jnp.int32, (rows, V7X_LANES), 1)
    q_onehot = jnp.where((col == g_of_row) | (col == g_of_row + GROUP), 1.0, 0.0).astype(BF16)
    keep_low = jnp.where(low_half, 1.0, 0.0).astype(BF16)
    keep_high = jnp.where(low_half, 0.0, 1.0).astype(BF16)
    head_rows = 2 * V7X_SUBLANES
    is_slot0 = lax.broadcasted_iota(jnp.int32, (head_rows, V7X_LANES), 0) == 0
    kcol = lax.broadcasted_iota(jnp.int32, (1, V7X_LANES), 1)
    ones = jnp.ones((2 * blk, V7X_LANES), BF16)
    no_feat = jnp.zeros((2 * blk - head_rows, V7X_LANES), BF16)

    def without_slot0(prev_ref, cur_ref, kvh):
        head = jnp.where(is_slot0, 0.0, prev_ref[:head_rows, _lane_block(kvh)].astype(F32)).astype(BF16)
        return jnp.concatenate([head, prev_ref[head_rows:, _lane_block(kvh)], cur_ref[:, _lane_block(kvh)]], axis=0)

    for kvh in range(N_KV_HEADS):
        sink_row = jnp.zeros((1, V7X_LANES), F32)
        for g in range(GROUP):
            sink_row = jnp.where((kcol == g) | (kcol == g + GROUP), sinks_ref[kvh * GROUP + g], sink_row)
        sink_hi = sink_row.astype(BF16).astype(F32)
        sink_feat = jnp.where(kcol < GROUP, sink_hi, sink_row - sink_hi)
        feat = jnp.concatenate([jnp.where(is_slot0, sink_feat, 0.0).astype(BF16), no_feat], axis=0)
        k_ext = jnp.concatenate([without_slot0(k_prev_ref, k_cur_ref, kvh), feat], axis=1)
        v_ones = jnp.concatenate([without_slot0(v_prev_ref, v_cur_ref, kvh), ones], axis=1)
        q_heads = []
        for pair in range(GROUP // 2):
            q_pair = q_ref[:, _lane_block(kvh * (GROUP // 2) + pair)]
            q_heads += [q_pair * keep_low, q_pair * keep_high]
        q_ext = jnp.concatenate([jnp.concatenate(q_heads, axis=0), q_onehot], axis=1)
        s = lax.dot_general(q_ext, k_ext, (((1,), (1,)), ((), ())), preferred_element_type=F32)
        s = (s.reshape(GROUP, blk, 2 * blk) + bias).reshape(rows, 2 * blk)
        e = jnp.exp(s - jnp.max(s, axis=-1, keepdims=True)).astype(BF16)
        o_sum = jnp.dot(e, v_ones, preferred_element_type=F32)
        o = o_sum[:, :V7X_LANES] * (1.0 / o_sum[:, V7X_LANES:])
        for pair in range(GROUP // 2):
            even = o[(2 * pair) * blk:(2 * pair + 1) * blk, :]
            odd = o[(2 * pair + 1) * blk:(2 * pair + 2) * blk, :]
            o_ref[:, _lane_block(kvh * (GROUP // 2) + pair)] = jnp.where(low_half, even, odd).astype(o_ref.dtype)


def _swa_prompt(q, kv_pair, sinks):
    nb = SEQ // WINDOW
    pair_width = N_KV_HEADS * V7X_LANES

    def cur(part):
        return pl.BlockSpec((WINDOW, pair_width), lambda b, n: (b * nb + n, part))

    def prev(part):
        return pl.BlockSpec((WINDOW, pair_width), lambda b, n: (b * nb + jnp.maximum(n - 1, 0), part))

    vmem = (
        4 * _nbytes((WINDOW, Q_WIDTH), BF16)
        + 8 * _nbytes((WINDOW, pair_width), BF16)
        + 24 * _nbytes((GROUP * WINDOW, 2 * WINDOW), F32)
    )
    return pl.pallas_call(
        _swa_prompt_kernel,
        grid=(BATCH, nb),
        in_specs=[
            pl.BlockSpec(memory_space=pltpu.SMEM),
            pl.BlockSpec((WINDOW, Q_WIDTH), lambda b, n: (b * nb + n, 0)),
            prev(0), cur(0), prev(1), cur(1),
        ],
        out_specs=pl.BlockSpec((WINDOW, Q_WIDTH), lambda b, n: (b * nb + n, 0)),
        out_shape=jax.ShapeDtypeStruct((BATCH * SEQ, Q_WIDTH), BF16),
        compiler_params=_params(("parallel", "arbitrary"), vmem),
        name="swa_prompt",
    )(sinks, q, kv_pair, kv_pair, kv_pair, kv_pair)


def _swa_sample_kernel(sinks_ref, q_ref, kv_ref, ck_ref, cv_ref, o_ref, nk_ref, nv_ref, kall_scr, vall_scr):
    nb = ck_ref.shape[0]
    t_new = DEC_SEQ
    s_len = WINDOW + t_new
    rows = GROUP * t_new
    kall_scr[:, :WINDOW, :] = ck_ref[...]
    vall_scr[:, :WINDOW, :] = cv_ref[...]
    kall_scr[:, WINDOW:, :] = kv_ref[:, :KV_WIDTH].reshape(nb, t_new, KV_WIDTH)
    vall_scr[:, WINDOW:, :] = kv_ref[:, KV_WIDTH:].reshape(nb, t_new, KV_WIDTH)
    nk_ref[...] = kall_scr[:, t_new:, :]
    nv_ref[...] = vall_scr[:, t_new:, :]

    row = lax.broadcasted_iota(jnp.int32, (1, rows, s_len), 1)
    kj = lax.broadcasted_iota(jnp.int32, (1, rows, s_len), 2)
    d = row % t_new + WINDOW - kj
    mask = (d >= 0) & (d < WINDOW)
    g_of_row = lax.broadcasted_iota(jnp.int32, (1, rows, 1), 1) // t_new
    q3 = q_ref[...].reshape(nb, t_new, Q_WIDTH)
    for kvh in range(N_KV_HEADS):
        k = kall_scr[:, :, _head_cols(kvh)].astype(BF16)
        v = vall_scr[:, :, _head_cols(kvh)].astype(BF16)
        q = jnp.concatenate([q3[:, :, _head_cols(kvh * GROUP + g)] for g in range(GROUP)], axis=1).astype(BF16)
        s = jnp.einsum("bqd,bkd->bqk", q, k, preferred_element_type=F32)
        s = jnp.where(mask, s, -jnp.inf)
        e, inv = _sink_softmax_weights(s, _group_sinks(sinks_ref, kvh, g_of_row))
        o = jnp.einsum("bqk,bkd->bqd", e.astype(BF16), v, preferred_element_type=F32) * inv
        for g in range(GROUP):
            o_ref[:, _head_cols(kvh * GROUP + g)] = o[:, g * t_new:(g + 1) * t_new, :].reshape(nb * t_new, HEAD_DIM)


_SWA_SAMPLE_SEQS_PER_STEP = 8


def _swa_sample(q, kv, cache_k, cache_v, sinks):
    nb = _SWA_SAMPLE_SEQS_PER_STEP
    cache_spec = pl.BlockSpec((nb, WINDOW, KV_WIDTH), lambda b: (b, 0, 0))
    all_keys = (nb, WINDOW + DEC_SEQ, KV_WIDTH)
    vmem = 8 * _nbytes((nb, WINDOW, KV_WIDTH), F32) + 2 * _nbytes(all_keys, F32) + 16 * 1024 * 1024
    return pl.pallas_call(
        _swa_sample_kernel,
        grid=(DEC_BATCH // nb,),
        in_specs=[
            pl.BlockSpec(memory_space=pltpu.SMEM),
            pl.BlockSpec((nb * DEC_SEQ, Q_WIDTH), lambda b: (b, 0)),
            pl.BlockSpec((nb * DEC_SEQ, 2 * KV_WIDTH), lambda b: (b, 0)),
            cache_spec,
            cache_spec,
        ],
        out_specs=[pl.BlockSpec((nb * DEC_SEQ, Q_WIDTH), lambda b: (b, 0)), cache_spec, cache_spec],
        out_shape=[
            jax.ShapeDtypeStruct((N_SAMPLE_ROWS, Q_WIDTH), F32),
            jax.ShapeDtypeStruct((DEC_BATCH, WINDOW, KV_WIDTH), F32),
            jax.ShapeDtypeStruct((DEC_BATCH, WINDOW, KV_WIDTH), F32),
        ],
        scratch_shapes=[pltpu.VMEM(all_keys, F32), pltpu.VMEM(all_keys, F32)],
        compiler_params=_params(("parallel",), vmem),
        name="swa_sample",
    )(sinks, q, kv, cache_k, cache_v)


def _cross_kernel(x_ref, gain_ref, wq_ref, mk_ref, mv_ref, wo_ref, out_ref, *, seqs, rows):
    scale = MEM_HEAD_DIM ** -0.5
    x = x_ref[...]
    h = _rms(x, gain_ref[...]).astype(BF16)
    q = jnp.dot(h, wq_ref[...], preferred_element_type=F32)
    heads = []
    for hd in range(N_MEM_HEADS):
        cs = slice(hd * MEM_HEAD_DIM, (hd + 1) * MEM_HEAD_DIM)
        qh = q[:, cs].reshape(seqs, rows, MEM_HEAD_DIM).astype(BF16)
        if mk_ref.ndim == 4:
            k = mk_ref[:, :, hd, :].astype(BF16)
            v = mv_ref[:, :, hd, :].astype(BF16)
        else:
            k = mk_ref[:, :, cs].astype(BF16)
            v = mv_ref[:, :, cs].astype(BF16)
        s = jnp.einsum("bqd,bkd->bqk", qh, k, preferred_element_type=F32) * scale
        e = jnp.exp(s - jnp.max(s, axis=-1, keepdims=True))
        inv = 1.0 / jnp.sum(e, axis=-1, keepdims=True)
        o = jnp.einsum("bqk,bkd->bqd", e.astype(BF16), v, preferred_element_type=F32) * inv
        heads.append(o.reshape(seqs * rows, MEM_HEAD_DIM))
    o_all = jnp.concatenate(heads, axis=-1).astype(BF16)
    out_ref[...] = x + jnp.dot(o_all, wo_ref[...], preferred_element_type=F32)


def _cross_block(x, gain, wq, mem_k, mem_v, wo, layer, *, n_seq, seqs, rows):
    seq_len = x.shape[0] // n_seq
    nt = seq_len // rows if seqs == 1 else 1
    tile = seqs * rows
    x_spec = pl.BlockSpec((tile, D_MODEL), lambda g, i: (g * nt + i, 0))
    mem_tail = mem_k.shape[3:]
    mem_spec = pl.BlockSpec((None, seqs, N_MEM) + mem_tail, lambda g, i: (layer, g, 0) + (0,) * len(mem_tail))
    mem_pad = V7X_SUBLANES // N_MEM_HEADS if len(mem_tail) == 2 else 1
    vmem = (
        4 * _nbytes((tile, D_MODEL), F32)
        + 4 * mem_pad * _nbytes((seqs, N_MEM, MEM_WIDTH), F32)
        + 4 * _nbytes((D_MODEL, MEM_WIDTH), BF16)
        + 4 * _nbytes((tile, D_MODEL), F32)
        + 8 * _nbytes((tile, N_MEM), F32)
    )
    return pl.pallas_call(
        functools.partial(_cross_kernel, seqs=seqs, rows=rows),
        grid=(n_seq // seqs, nt),
        in_specs=[
            x_spec,
            _row_param_spec(layer, D_MODEL),
            pl.BlockSpec((None, D_MODEL, MEM_WIDTH), lambda g, i: (layer, 0, 0)),
            mem_spec,
            mem_spec,
            pl.BlockSpec((None, MEM_WIDTH, D_MODEL), lambda g, i: (layer, 0, 0)),
        ],
        out_specs=x_spec,
        out_shape=jax.ShapeDtypeStruct(x.shape, F32),
        compiler_params=_params(("parallel", "arbitrary"), vmem),
        name="cross_block",
    )(x, _row_param(gain), wq, mem_k, mem_v, wo)


def _log_sigmoid(x):
    return jnp.minimum(x, 0.0) - jnp.log1p(jnp.exp(-jnp.abs(x)))


def _one_minus_exp2(y, exp_y):
    return jnp.tanh(-y) * (1.0 + exp_y * exp_y)


def _lru_gates(xc, wa_ref, ba_ref, wx_ref, bx_ref, lam_ref):
    r_parts = []
    i_parts = []
    for n in range(LRU_BLOCKS):
        cs = slice(n * LRU_BLOCK_SIZE, (n + 1) * LRU_BLOCK_SIZE)
        xb = xc[:, cs].astype(BF16)
        r_parts.append(jnp.dot(xb, wa_ref[n], preferred_element_type=F32))
        i_parts.append(jnp.dot(xb, wx_ref[n], preferred_element_type=F32))
    r = jax.nn.sigmoid(jnp.concatenate(r_parts, axis=-1) + ba_ref[...])
    ig = jax.nn.sigmoid(jnp.concatenate(i_parts, axis=-1) + bx_ref[...])
    log_a = r * (LRU_C * _log_sigmoid(lam_ref[...]))
    a = jnp.exp(log_a)
    z = _one_minus_exp2(log_a, a)
    u = jnp.where(z > 0.0, z * lax.rsqrt(z), 0.0) * (ig * xc)
    return a, u


def _lru_prompt_kernel(xb_ref, gate_ref, x_ref, wc_ref, bc_ref, wa_ref, ba_ref, wx_ref, bx_ref, lam_ref,
                       wo_ref, out_ref, hlast_ref, xext_scr, a_scr, u_scr, h_scr):
    i = pl.program_id(1)
    tm = xb_ref.shape[0]
    pad = V7X_SUBLANES

    @pl.when(i == 0)
    def _():
        xext_scr[0:pad, :] = jnp.zeros((pad, LRU_WIDTH), F32)
        h_scr[...] = jnp.zeros((1, LRU_WIDTH), F32)

    xext_scr[pad:pad + tm, :] = xb_ref[...]
    xc = bc_ref[...] + wc_ref[CONV_WIDTH - 1:CONV_WIDTH, :] * xb_ref[...]
    for k in range(1, CONV_WIDTH):
        xc = xc + wc_ref[CONV_WIDTH - 1 - k:CONV_WIDTH - k, :] * xext_scr[pad - k:pad - k + tm, :]
    xext_scr[0:pad, :] = xext_scr[tm:tm + pad, :]

    a, u = _lru_gates(xc, wa_ref, ba_ref, wx_ref, bx_ref, lam_ref)
    a_scr[...] = a
    u_scr[...] = u

    def step(t, h):
        h = a_scr[pl.ds(t, 1), :] * h + u_scr[pl.ds(t, 1), :]
        u_scr[pl.ds(t, 1), :] = h
        return h

    h_scr[...] = lax.fori_loop(0, tm, step, h_scr[...], unroll=8)
    hlast_ref[...] = u_scr[tm - pad:tm, :]
    y = (u_scr[...] * gate_ref[...]).astype(BF16)
    out_ref[...] = x_ref[...] + jnp.dot(y, wo_ref[...], preferred_element_type=F32)


def _lru_weight_specs(layer_idx):
    def const3(*_):
        return (layer_idx, 0, 0)

    def const4(*_):
        return (layer_idx, 0, 0, 0)

    gate_w = pl.BlockSpec((None, LRU_BLOCKS, LRU_BLOCK_SIZE, LRU_BLOCK_SIZE), const4)
    vec = _row_param_spec(layer_idx, LRU_WIDTH)
    return [
        pl.BlockSpec((None, CONV_WIDTH, LRU_WIDTH), const3),
        vec,
        gate_w,
        vec,
        gate_w,
        vec,
        vec,
        pl.BlockSpec((None, LRU_WIDTH, D_MODEL), const3),
    ]


def _lru_weight_args(w_conv, b_conv, w_gate_a, b_gate_a, w_gate_x, b_gate_x, lam, w_out):
    return (w_conv, _row_param(b_conv), w_gate_a, _row_param(b_gate_a), w_gate_x, _row_param(b_gate_x),
            _row_param(lam), w_out)


def _lru_prompt(xb, gate, x, lw, j, *, tm):
    nt = SEQ // tm
    row_spec0 = pl.BlockSpec((tm, LRU_WIDTH), lambda b, i: (b * nt + i, 0))
    vmem = (
        8 * _nbytes((tm, LRU_WIDTH), F32)
        + 3 * _nbytes((tm + V7X_SUBLANES, LRU_WIDTH), F32)
        + 2 * _nbytes((LRU_WIDTH, D_MODEL), BF16)
        + 4 * _nbytes((LRU_BLOCKS, LRU_BLOCK_SIZE, LRU_BLOCK_SIZE), BF16)
        + 10 * _nbytes((tm, LRU_WIDTH), F32)
    )
    return pl.pallas_call(
        _lru_prompt_kernel,
        grid=(BATCH, nt),
        in_specs=[row_spec0, row_spec0, row_spec0] + _lru_weight_specs(j),
        out_specs=[
            row_spec0,
            pl.BlockSpec((None, V7X_SUBLANES, LRU_WIDTH), lambda b, i: (b, 0, 0)),
        ],
        out_shape=[
            jax.ShapeDtypeStruct((BATCH * SEQ, D_MODEL), F32),
            jax.ShapeDtypeStruct((BATCH, V7X_SUBLANES, LRU_WIDTH), F32),
        ],
        scratch_shapes=[
            pltpu.VMEM((tm + V7X_SUBLANES, LRU_WIDTH), F32),
            pltpu.VMEM((tm, LRU_WIDTH), F32),
            pltpu.VMEM((tm, LRU_WIDTH), F32),
            pltpu.VMEM((1, LRU_WIDTH), F32),
        ],
        compiler_params=_params(("arbitrary", "arbitrary"), vmem),
        name="lru_prompt",
    )(xb, gate, x, *lw)


def _lru_sample_kernel(xb_ref, gate_ref, x_ref, cbuf_ref, h0_ref, wc_ref, bc_ref, wa_ref, ba_ref, wx_ref, bx_ref,
                       lam_ref, wo_ref, out_ref, hs_ref):
    rows = xb_ref.shape[0]
    t = lax.broadcasted_iota(jnp.int32, (rows, LRU_WIDTH), 0) % DEC_SEQ
    xb = xb_ref[...]
    cbuf = cbuf_ref[...]
    xc = bc_ref[...] + wc_ref[CONV_WIDTH - 1:CONV_WIDTH, :] * xb
    for k in range(1, CONV_WIDTH):
        shift_c = CONV_WIDTH - 1 - k
        from_buf = cbuf if shift_c == 0 else pltpu.roll(cbuf, rows - shift_c, 0)
        xk = jnp.where(t >= k, pltpu.roll(xb, k, 0), from_buf)
        xc = xc + wc_ref[CONV_WIDTH - 1 - k:CONV_WIDTH - k, :] * xk

    a, u = _lru_gates(xc, wa_ref, ba_ref, wx_ref, bx_ref, lam_ref)
    k = 1
    while k < DEC_SEQ:
        m = t >= k
        u = jnp.where(m, a * pltpu.roll(u, k, 0) + u, u)
        a = jnp.where(m, a * pltpu.roll(a, k, 0), a)
        k *= 2
    hs = u + a * h0_ref[...]
    hs_ref[...] = hs
    y = (hs * gate_ref[...]).astype(BF16)
    out_ref[...] = x_ref[...] + jnp.dot(y, wo_ref[...], preferred_element_type=F32)


def _lru_sample(xb, gate, x, cbuf_rows, h0_rows, lw, j):
    rows = N_SAMPLE_ROWS
    row_spec0 = pl.BlockSpec((rows, LRU_WIDTH), lambda i: (0, 0))
    vmem = 40 * _nbytes((rows, LRU_WIDTH), F32) + 2 * _nbytes((LRU_WIDTH, D_MODEL), BF16)
    return pl.pallas_call(
        _lru_sample_kernel,
        grid=(1,),
        in_specs=[row_spec0] * 5 + _lru_weight_specs(j),
        out_specs=[row_spec0, row_spec0],
        out_shape=[
            jax.ShapeDtypeStruct((rows, D_MODEL), F32),
            jax.ShapeDtypeStruct((rows, LRU_WIDTH), F32),
        ],
        compiler_params=_params(("arbitrary",), vmem),
        name="lru_sample",
    )(xb, gate, x, cbuf_rows, h0_rows, *lw)


_FFN_TM = 1024
_FFN_TF = 512
_FFN_ROW_CHUNK = 512


def kernel(x_prompt, x_sample, cache_swa_k, cache_swa_v, state_lru_h, state_lru_conv, cache_mem_k, cache_mem_v,
           mem_prompt, ln_ffn1, ffn1_w_in, ffn1_w_out, ln_mix, swa_w_qkv, swa_w_o, swa_sinks, lru_w_in,
           lru_w_conv, lru_b_conv, lru_w_gate_a, lru_b_gate_a, lru_w_gate_x, lru_b_gate_x, lru_lambda,
           lru_w_out, ln_cross, ln_mem, cross_w_q, cross_w_kv, cross_w_o, ln_ffn2, ffn2_w_in, ffn2_w_out,
           ln_final):
    bf = lambda w: w.astype(BF16)
    ffn_seq = []
    for layer in range(DEPTH):
        ffn_seq += [(ffn1_w_in, ffn1_w_out, layer), (ffn2_w_in, ffn2_w_out, layer)]
    ffn_w = (bf(ffn1_w_in[0]), bf(ffn1_w_out[0]))

    def ffn(x, x_extra, gain, layer, final_gain=None):
        nonlocal ffn_w
        nxt = ffn_seq[1] if len(ffn_seq) > 1 else None
        out = _ffn(x, x_extra, gain, layer, *ffn_w, tm=_FFN_TM, tf=_FFN_TF, row_chunk=_FFN_ROW_CHUNK,
                   final_gain=final_gain, cast_next=nxt)
        del ffn_seq[0]
        if nxt is not None:
            ffn_w = out[2:]
        return out[0], out[1]

    swa_w_qkv, swa_w_o = bf(swa_w_qkv), bf(swa_w_o)
    lru_w_in, lru_w_out = bf(lru_w_in), bf(lru_w_out)
    lru_w_gate_a, lru_w_gate_x = bf(lru_w_gate_a), bf(lru_w_gate_x)
    cross_w_q, cross_w_kv, cross_w_o = bf(cross_w_q), bf(cross_w_kv), bf(cross_w_o)

    xp = x_prompt.reshape(N_PROMPT_ROWS, D_MODEL)
    xs = x_sample.reshape(N_SAMPLE_ROWS, D_MODEL)

    mem_rows = mem_prompt.reshape(BATCH * N_MEM, D_MODEL)
    mkv = [
        _norm_linear(mem_rows, ln_mem, layer, cross_w_kv, layer, tm=BATCH * N_MEM, tn=MEM_WIDTH, out_dtype=F32)
        for layer in range(DEPTH)
    ]
    mem_k_prompt = jnp.stack([m[:, :MEM_WIDTH] for m in mkv]).reshape(DEPTH, BATCH, N_MEM, MEM_WIDTH)
    mem_v_prompt = jnp.stack([m[:, MEM_WIDTH:] for m in mkv]).reshape(DEPTH, BATCH, N_MEM, MEM_WIDTH)

    cos_p, sin_p = _rope_tables(jnp.arange(SEQ, dtype=jnp.int32))
    cos_s, sin_s = _rope_tables(PAST_LEN + jnp.arange(N_SAMPLE_ROWS, dtype=jnp.int32) % DEC_SEQ)

    swa_k_p, swa_v_p, swa_k_s, swa_v_s = [], [], [], []
    lru_h_p, lru_conv_p, lru_h_s, lru_conv_s = [], [], [], []

    for layer in range(DEPTH):
        j = layer // 2
        xp, xs = ffn(xp, xs, ln_ffn1, layer)

        if layer % 2 == 0:
            qkv_tm = 512
            q_p, kv_last, kv_pair = _qkv(xp, ln_mix, layer, swa_w_qkv, j, cos_p, sin_p, tm=qkv_tm, q_dtype=BF16,
                                         seq_len=SEQ)
            o_p = _swa_prompt(q_p, kv_pair, swa_sinks[j])
            xp = _linear_residual(o_p, swa_w_o, j, xp, tm=512)
            kv_tail = kv_last[:, qkv_tm - WINDOW:, :]
            swa_k_p.append(kv_tail[..., :KV_WIDTH].reshape(BATCH, WINDOW, N_KV_HEADS, HEAD_DIM))
            swa_v_p.append(kv_tail[..., KV_WIDTH:].reshape(BATCH, WINDOW, N_KV_HEADS, HEAD_DIM))

            q_s, kv_s = _qkv(xs, ln_mix, layer, swa_w_qkv, j, cos_s, sin_s, tm=N_SAMPLE_ROWS, q_dtype=F32)
            o_s, nk, nv = _swa_sample(
                q_s, kv_s,
                cache_swa_k[j].reshape(DEC_BATCH, WINDOW, KV_WIDTH),
                cache_swa_v[j].reshape(DEC_BATCH, WINDOW, KV_WIDTH),
                swa_sinks[j],
            )
            xs = _linear_residual(o_s, swa_w_o, j, xs, tm=N_SAMPLE_ROWS)
            swa_k_s.append(nk.reshape(DEC_BATCH, WINDOW, N_KV_HEADS, HEAD_DIM))
            swa_v_s.append(nv.reshape(DEC_BATCH, WINDOW, N_KV_HEADS, HEAD_DIM))
        else:
            lw = _lru_weight_args(lru_w_conv, lru_b_conv, lru_w_gate_a, lru_b_gate_a, lru_w_gate_x, lru_b_gate_x,
                                  lru_lambda, lru_w_out)
            xb_p, gate_p = _lru_in(xp, ln_mix, layer, lru_w_in, j, tm=512, tn=1024)
            xp, h_last = _lru_prompt(xb_p, gate_p, xp, lw, j, tm=256)
            lru_h_p.append(h_last[:, V7X_SUBLANES - 1, :])
            lru_conv_p.append(xb_p.reshape(BATCH, SEQ, LRU_WIDTH)[:, SEQ - (CONV_WIDTH - 1):, :])

            xb_s, gate_s = _lru_in(xs, ln_mix, layer, lru_w_in, j, tm=N_SAMPLE_ROWS, tn=1024)
            cbuf_rows = jnp.pad(
                state_lru_conv[j], ((0, 0), (0, DEC_SEQ - (CONV_WIDTH - 1)), (0, 0))
            ).reshape(N_SAMPLE_ROWS, LRU_WIDTH)
            h0_rows = jnp.repeat(state_lru_h[j], DEC_SEQ, axis=0)
            xs, hs_s = _lru_sample(xb_s, gate_s, xs, cbuf_rows, h0_rows, lw, j)
            lru_h_s.append(hs_s.reshape(DEC_BATCH, DEC_SEQ, LRU_WIDTH)[:, DEC_SEQ - 1, :])
            xp_tail = jnp.concatenate(
                [state_lru_conv[j], xb_s.reshape(DEC_BATCH, DEC_SEQ, LRU_WIDTH)], axis=1)
            lru_conv_s.append(xp_tail[:, -(CONV_WIDTH - 1):, :])

        xp = _cross_block(xp, ln_cross, cross_w_q, mem_k_prompt, mem_v_prompt, cross_w_o, layer,
                          n_seq=BATCH, seqs=1, rows=512)
        xs = _cross_block(xs, ln_cross, cross_w_q, cache_mem_k, cache_mem_v, cross_w_o, layer,
                          n_seq=DEC_BATCH, seqs=8, rows=DEC_SEQ)

        fin = ln_final if layer == DEPTH - 1 else None
        xp, xs = ffn(xp, xs, ln_ffn2, layer, final_gain=fin)

    return (
        xp.reshape(BATCH, SEQ, D_MODEL),
        xs.reshape(DEC_BATCH, DEC_SEQ, D_MODEL),
        jnp.stack(swa_k_p), jnp.stack(swa_v_p), jnp.stack(swa_k_s), jnp.stack(swa_v_s),
        jnp.stack(lru_h_p), jnp.stack(lru_conv_p), jnp.stack(lru_h_s), jnp.stack(lru_conv_s),
        mem_k_prompt.reshape(DEPTH, BATCH, N_MEM, N_MEM_HEADS, MEM_HEAD_DIM),
        mem_v_prompt.reshape(DEPTH, BATCH, N_MEM, N_MEM_HEADS, MEM_HEAD_DIM),
    )
```

```python
import functools
import math

import jax
import jax.numpy as jnp
from jax import lax
from jax.experimental import pallas as pl
from jax.experimental.pallas import tpu as pltpu

F32 = jnp.float32
BF16 = jnp.bfloat16

D_MODEL = 2048
BATCH = 2
SEQ = 4096
DEPTH = 2
DEC_BATCH = 32
DEC_SEQ = 8
PAST_LEN = 16384
N_HEADS = 32
N_KV_HEADS = 8
HEAD_DIM = 64
GROUP = N_HEADS // N_KV_HEADS
WINDOW = 128
ROPE_THETA = 10000.0
LRU_WIDTH = D_MODEL
LRU_BLOCKS = 8
LRU_BLOCK_SIZE = LRU_WIDTH // LRU_BLOCKS
CONV_WIDTH = 4
LRU_C = 8.0
D_FF = 5632
N_MEM = 256
N_MEM_HEADS = 4
MEM_HEAD_DIM = 128
MEM_WIDTH = N_MEM_HEADS * MEM_HEAD_DIM
NORM_EPS = 1e-6
Q_WIDTH = N_HEADS * HEAD_DIM
KV_WIDTH = N_KV_HEADS * HEAD_DIM

V7X_LANES = 128
V7X_SUBLANES = 8
V7X_VMEM_BYTES = 64 * 1024 * 1024
V7X_VMEM_USABLE_BYTES = V7X_VMEM_BYTES - 2 * 1024 * 1024

N_PROMPT_ROWS = BATCH * SEQ
N_SAMPLE_ROWS = DEC_BATCH * DEC_SEQ


def _params(semantics, vmem_bytes):
    return pltpu.CompilerParams(
        dimension_semantics=semantics,
        vmem_limit_bytes=int(min(max(vmem_bytes, 32 * 1024 * 1024), V7X_VMEM_USABLE_BYTES)),
    )


def _nbytes(shape, dtype):
    return math.prod(shape) * jnp.dtype(dtype).itemsize


def _rms(x, gain):
    ms = jnp.mean(x * x, axis=-1, keepdims=True)
    return x * lax.rsqrt(ms + NORM_EPS) * gain


def _lane_block(c):
    return slice(c * V7X_LANES, (c + 1) * V7X_LANES)


def _row_param(p):
    return p.reshape(p.shape[0], 1, p.shape[-1])


def _row_param_spec(layer, width):
    return pl.BlockSpec((None, 1, width), lambda *_: (layer, 0, 0))


def _for_chunks(n_rows, chunk, body):
    n_chunks = n_rows // chunk
    if n_chunks == 1:
        body(pl.ds(0, n_rows))
        return

    def step(c, carry):
        body(pl.ds(pl.multiple_of(c * chunk, chunk), chunk))
        return carry

    lax.fori_loop(0, n_chunks, step, 0)


def _ffn_kernel(x_ref, gain_ref, wg_ref, wu_ref, wo_ref, xe_ref, *rest, row_chunk, final_norm, cast_next):
    rest = list(rest)
    gfin_ref = rest.pop(0) if final_norm else None
    next_in_ref, next_out_ref = (rest.pop(0), rest.pop(0)) if cast_next else (None, None)
    o_ref, oe_ref = rest.pop(0), rest.pop(0)
    next_in_bf_ref, next_out_bf_ref = (rest.pop(0), rest.pop(0)) if cast_next else (None, None)
    h_scr, he_scr = rest
    f = pl.program_id(1)

    if cast_next:
        next_in_bf_ref[...] = next_in_ref[...].astype(BF16)
        next_out_bf_ref[...] = next_out_ref[...].astype(BF16)

    def run(x_ref, o_ref, h_scr, chunk):
        n_rows = x_ref.shape[0]

        @pl.when(f == 0)
        def _():
            def prologue(rows):
                h_scr[rows, :] = _rms(x_ref[rows, :], gain_ref[...]).astype(BF16)
                o_ref[rows, :] = jnp.zeros((chunk, D_MODEL), F32)

            _for_chunks(n_rows, chunk, prologue)

        chunks = [pl.ds(c * chunk, chunk) for c in range(n_rows // chunk)]
        acts = []
        for rows in chunks:
            h = h_scr[rows, :]
            g = jnp.dot(h, wg_ref[...], preferred_element_type=F32)
            u = jnp.dot(h, wu_ref[...], preferred_element_type=F32)
            acts.append((g * jax.nn.sigmoid(g) * u).astype(BF16))
        for rows, a in zip(chunks, acts):
            o_ref[rows, :] += jnp.dot(a, wo_ref[...], preferred_element_type=F32)

        @pl.when(f == pl.num_programs(1) - 1)
        def _():
            def epilogue(rows):
                y = x_ref[rows, :] + 0.5 * o_ref[rows, :]
                if final_norm:
                    y = _rms(y, gfin_ref[...])
                o_ref[rows, :] = y

            _for_chunks(n_rows, chunk, epilogue)

    run(x_ref, o_ref, h_scr, row_chunk)

    @pl.when(pl.program_id(0) == 0)
    def _():
        run(xe_ref, oe_ref, he_scr, xe_ref.shape[0])


def _ffn(x, x_extra, gain, layer, w_in, w_out, *, tm, tf, row_chunk, final_gain=None, cast_next=None):
    m = x.shape[0]
    me = x_extra.shape[0]
    n_tiles = m // tm
    nf = D_FF // tf
    final_norm = final_gain is not None
    row_spec = pl.BlockSpec((tm, D_MODEL), lambda i, f: (i, 0))
    extra_spec = pl.BlockSpec((me, D_MODEL), lambda i, f: (0, 0))
    in_specs = [
        row_spec,
        _row_param_spec(layer, D_MODEL),
        pl.BlockSpec((D_MODEL, tf), lambda i, f: (0, f)),
        pl.BlockSpec((D_MODEL, tf), lambda i, f: (0, f + nf)),
        pl.BlockSpec((tf, D_MODEL), lambda i, f: (f, 0)),
        pl.BlockSpec((me, D_MODEL), lambda i, f: (0, 0), pipeline_mode=pl.Buffered(1)),
    ]
    args = [x, _row_param(gain), w_in, w_in, w_out, x_extra]
    out_specs = [row_spec, extra_spec]
    out_shape = [jax.ShapeDtypeStruct((m, D_MODEL), F32), jax.ShapeDtypeStruct((me, D_MODEL), F32)]
    if final_norm:
        in_specs.append(_row_param_spec(0, D_MODEL))
        args.append(final_gain.reshape(1, 1, D_MODEL))
    vmem = (
        4 * _nbytes((tm, D_MODEL), F32)
        + _nbytes((tm, D_MODEL), BF16)
        + 4 * _nbytes((me, D_MODEL), F32)
        + _nbytes((me, D_MODEL), BF16)
        + 2 * 3 * _nbytes((D_MODEL, tf), BF16)
        + 4 * _nbytes((row_chunk, tf), F32)
        + 2 * _nbytes((row_chunk, D_MODEL), F32)
    )
    if cast_next is not None:
        nw_in, nw_out, nl = cast_next
        in_blk = (D_MODEL // n_tiles, 2 * D_FF // nf)
        out_blk = (D_FF // nf, D_MODEL // n_tiles)
        in_specs += [
            pl.BlockSpec((None,) + in_blk, lambda i, f: (nl, i, f)),
            pl.BlockSpec((None,) + out_blk, lambda i, f: (nl, f, i)),
        ]
        args += [nw_in, nw_out]
        out_specs += [pl.BlockSpec(in_blk, lambda i, f: (i, f)), pl.BlockSpec(out_blk, lambda i, f: (f, i))]
        out_shape += [jax.ShapeDtypeStruct(nw_in.shape[1:], BF16), jax.ShapeDtypeStruct(nw_out.shape[1:], BF16)]
        vmem += 2 * (_nbytes(in_blk, F32) + _nbytes(in_blk, BF16) + _nbytes(out_blk, F32) + _nbytes(out_blk, BF16))
    return pl.pallas_call(
        functools.partial(_ffn_kernel, row_chunk=row_chunk, final_norm=final_norm, cast_next=cast_next is not None),
        grid=(n_tiles, nf),
        in_specs=in_specs,
        out_specs=out_specs,
        out_shape=out_shape,
        scratch_shapes=[pltpu.VMEM((tm, D_MODEL), BF16), pltpu.VMEM((me, D_MODEL), BF16)],
        compiler_params=_params(("arbitrary", "arbitrary"), vmem),
        name="ffn",
    )(*args)


def _gelu_tanh(y):
    return 0.5 * y * (1.0 + jnp.tanh(math.sqrt(2.0 / math.pi) * (y + 0.044715 * (y * y * y))))


def _norm_linear_kernel(x_ref, gain_ref, w_ref, o_ref, h_scr):
    @pl.when(pl.program_id(1) == 0)
    def _():
        h_scr[...] = _rms(x_ref[...], gain_ref[...]).astype(BF16)

    o_ref[...] = jnp.dot(h_scr[...], w_ref[...], preferred_element_type=F32).astype(o_ref.dtype)


def _lru_in_kernel(x_ref, gain_ref, wx_ref, wg_ref, xb_ref, gate_ref, h_scr):
    @pl.when(pl.program_id(1) == 0)
    def _():
        h_scr[...] = _rms(x_ref[...], gain_ref[...]).astype(BF16)

    h = h_scr[...]
    gate_ref[...] = _gelu_tanh(jnp.dot(h, wg_ref[...], preferred_element_type=F32))
    xb_ref[...] = jnp.dot(h, wx_ref[...], preferred_element_type=F32)


def _lru_in(x, gain, gain_layer, w, w_layer, *, tm, tn):
    m = x.shape[0]
    n_col = LRU_WIDTH // tn
    vmem = (
        2 * _nbytes((tm, D_MODEL), F32)
        + _nbytes((tm, D_MODEL), BF16)
        + 4 * _nbytes((D_MODEL, tn), BF16)
        + 10 * _nbytes((tm, tn), F32)
    )
    out_spec = pl.BlockSpec((tm, tn), lambda i, j: (i, j))
    return pl.pallas_call(
        _lru_in_kernel,
        grid=(m // tm, n_col),
        in_specs=[
            pl.BlockSpec((tm, D_MODEL), lambda i, j: (i, 0)),
            _row_param_spec(gain_layer, D_MODEL),
            pl.BlockSpec((None, D_MODEL, tn), lambda i, j: (w_layer, 0, j)),
            pl.BlockSpec((None, D_MODEL, tn), lambda i, j: (w_layer, 0, n_col + j)),
        ],
        out_specs=[out_spec, out_spec],
        out_shape=[jax.ShapeDtypeStruct((m, LRU_WIDTH), F32)] * 2,
        scratch_shapes=[pltpu.VMEM((tm, D_MODEL), BF16)],
        compiler_params=_params(("parallel", "arbitrary"), vmem),
        name="lru_in",
    )(x, _row_param(gain), w, w)


def _norm_linear(x, gain, gain_layer, w, w_layer, *, tm, tn, out_dtype):
    m = x.shape[0]
    n = w.shape[-1]
    vmem = (
        2 * _nbytes((tm, D_MODEL), F32)
        + _nbytes((tm, D_MODEL), BF16)
        + 2 * _nbytes((D_MODEL, tn), BF16)
        + 4 * _nbytes((tm, tn), F32)
    )
    return pl.pallas_call(
        _norm_linear_kernel,
        grid=(m // tm, n // tn),
        in_specs=[
            pl.BlockSpec((tm, D_MODEL), lambda i, j: (i, 0)),
            _row_param_spec(gain_layer, D_MODEL),
            pl.BlockSpec((None, D_MODEL, tn), lambda i, j: (w_layer, 0, j)),
        ],
        out_specs=pl.BlockSpec((tm, tn), lambda i, j: (i, j)),
        out_shape=jax.ShapeDtypeStruct((m, n), out_dtype),
        scratch_shapes=[pltpu.VMEM((tm, D_MODEL), BF16)],
        compiler_params=_params(("parallel", "arbitrary"), vmem),
        name="norm_linear",
    )(x, _row_param(gain), w)


_QKV_STEPS = 2
_Q_TN = Q_WIDTH // _QKV_STEPS
_ATTN_SCALE = HEAD_DIM ** -0.5
assert math.frexp(_ATTN_SCALE)[0] == 0.5
assert 2 * KV_WIDTH // _QKV_STEPS == KV_WIDTH


def _rope_tables(positions):
    half = HEAD_DIM // 2
    inv = ROPE_THETA ** (-jnp.arange(half, dtype=F32) * (2.0 / HEAD_DIM))
    ang = positions.astype(F32)[:, None] * inv[None, :]
    cos = jnp.cos(ang)
    sin = jnp.sin(ang)
    reps = V7X_LANES // HEAD_DIM
    cos_t = jnp.tile(jnp.concatenate([cos, cos], axis=-1), (1, reps))
    sin_t = jnp.tile(jnp.concatenate([-sin, sin], axis=-1), (1, reps))
    return jnp.stack([cos_t, jnp.ones_like(cos_t)]), jnp.stack([sin_t, jnp.zeros_like(sin_t)])


def _qkv_kernel(x_ref, gain_ref, wq_ref, wkv_ref, cos_ref, sin_ref, cos_kv_ref, sin_kv_ref, q_ref, kv_ref,
                pair_ref=None):
    tm = x_ref.shape[0]
    lane = lax.broadcasted_iota(jnp.int32, (tm, V7X_LANES), 1)
    first_half = (lane % HEAD_DIM) < (HEAD_DIM // 2)
    low_half = lane < HEAD_DIM

    def rope(yc, cos, sin):
        rot = jnp.where(
            first_half,
            pltpu.roll(yc, V7X_LANES - HEAD_DIM // 2, 1),
            pltpu.roll(yc, HEAD_DIM // 2, 1),
        )
        return yc * cos + rot * sin

    h = _rms(x_ref[...], gain_ref[...]).astype(BF16)
    yq = jnp.dot(h, wq_ref[...], preferred_element_type=F32)
    cos, sin = cos_ref[...], sin_ref[...]
    for c in range(_Q_TN // V7X_LANES):
        q_ref[:, _lane_block(c)] = (rope(yq[:, _lane_block(c)], cos, sin) * _ATTN_SCALE).astype(q_ref.dtype)

    ykv = jnp.dot(h, wkv_ref[...], preferred_element_type=F32)
    cos_kv, sin_kv = cos_kv_ref[...], sin_kv_ref[...]
    heads_per_block = V7X_LANES // HEAD_DIM
    for c in range(KV_WIDTH // V7X_LANES):
        r = rope(ykv[:, _lane_block(c)], cos_kv, sin_kv)
        kv_ref[:, _lane_block(c)] = r
        if pair_ref is not None:
            swapped = pltpu.roll(r, HEAD_DIM, 1)
            pair_ref[:, _lane_block(heads_per_block * c)] = jnp.where(low_half, r, swapped).astype(BF16)
            pair_ref[:, _lane_block(heads_per_block * c + 1)] = jnp.where(low_half, swapped, r).astype(BF16)


def _qkv(x, gain, gain_layer, w, w_layer, cos_t, sin_t, *, tm, q_dtype, seq_len=None):
    m = x.shape[0]
    n_tab = cos_t.shape[1] // tm
    paired_kv = seq_len is not None
    pair_width = N_KV_HEADS * V7X_LANES
    vmem = (
        2 * _nbytes((tm, D_MODEL), F32)
        + _nbytes((tm, D_MODEL), BF16)
        + 2 * _nbytes((D_MODEL, _Q_TN + KV_WIDTH), BF16)
        + 8 * _nbytes((tm, _Q_TN + KV_WIDTH), F32)
        + 4 * _nbytes((tm, pair_width), BF16)
    )
    out_specs = [pl.BlockSpec((tm, _Q_TN), lambda j, i: (i, j))]
    out_shape = [jax.ShapeDtypeStruct((m, Q_WIDTH), q_dtype)]
    if paired_kv:
        tiles_per_seq = seq_len // tm
        out_specs.append(pl.BlockSpec((None, tm, KV_WIDTH), lambda j, i: (i // tiles_per_seq, 0, j)))
        out_shape.append(jax.ShapeDtypeStruct((m // seq_len, tm, 2 * KV_WIDTH), F32))
        out_specs.append(pl.BlockSpec((tm, pair_width), lambda j, i: (i, j)))
        out_shape.append(jax.ShapeDtypeStruct((m, 2 * pair_width), BF16))
    else:
        out_specs.append(pl.BlockSpec((tm, KV_WIDTH), lambda j, i: (i, j)))
        out_shape.append(jax.ShapeDtypeStruct((m, 2 * KV_WIDTH), F32))
    rot_spec = pl.BlockSpec((None, tm, V7X_LANES), lambda j, i: (0, i % n_tab, 0))
    rot_or_id_spec = pl.BlockSpec((None, tm, V7X_LANES), lambda j, i: (j, i % n_tab, 0))
    return pl.pallas_call(
        _qkv_kernel,
        grid=(_QKV_STEPS, m // tm),
        in_specs=[
            pl.BlockSpec((tm, D_MODEL), lambda j, i: (i, 0)),
            _row_param_spec(gain_layer, D_MODEL),
            pl.BlockSpec((None, D_MODEL, _Q_TN), lambda j, i: (w_layer, 0, j)),
            pl.BlockSpec((None, D_MODEL, KV_WIDTH), lambda j, i: (w_layer, 0, Q_WIDTH // KV_WIDTH + j)),
            rot_spec, rot_spec, rot_or_id_spec, rot_or_id_spec,
        ],
        out_specs=out_specs,
        out_shape=out_shape,
        compiler_params=_params(("arbitrary", "arbitrary"), vmem),
        name="qkv_rope",
    )(x, _row_param(gain), w, w, cos_t, sin_t, cos_t, sin_t)


def _linear_residual_kernel(o_ref, w_ref, x_ref, out_ref):
    out_ref[...] = x_ref[...] + jnp.dot(o_ref[...].astype(BF16), w_ref[...], preferred_element_type=F32)


def _linear_residual(o, w, layer, x, *, tm):
    m, k = o.shape
    vmem = (
        2 * _nbytes((tm, k), o.dtype)
        + 2 * _nbytes((k, D_MODEL), BF16)
        + 6 * _nbytes((tm, D_MODEL), F32)
    )
    return pl.pallas_call(
        _linear_residual_kernel,
        grid=(m // tm,),
        in_specs=[
            pl.BlockSpec((tm, k), lambda i: (i, 0)),
            pl.BlockSpec((None, k, D_MODEL), lambda i: (layer, 0, 0)),
            pl.BlockSpec((tm, D_MODEL), lambda i: (i, 0)),
        ],
        out_specs=pl.BlockSpec((tm, D_MODEL), lambda i: (i, 0)),
        out_shape=jax.ShapeDtypeStruct((m, D_MODEL), F32),
        compiler_params=_params(("parallel",), vmem),
        name="linear_residual",
    )(o, w, x)


def _head_cols(h):
    return slice(h * HEAD_DIM, (h + 1) * HEAD_DIM)


def _group_sinks(sinks_ref, kvh, g_of_row):
    sink = jnp.full(g_of_row.shape, sinks_ref[kvh * GROUP], F32)
    for g in range(1, GROUP):
        sink = jnp.where(g_of_row == g, sinks_ref[kvh * GROUP + g], sink)
    return sink


def _sink_softmax_weights(s, sink):
    m = jnp.maximum(jnp.max(s, axis=-1, keepdims=True), sink)
    e = jnp.exp(s - m)
    denom = jnp.sum(e, axis=-1, keepdims=True) + jnp.exp(sink - m)
    return e, 1.0 / denom


def _swa_prompt_kernel(sinks_ref, q_ref, k_prev_ref, k_cur_ref, v_prev_ref, v_cur_ref, o_ref):
    n = pl.program_id(1)
    blk = WINDOW
    rows = GROUP * blk
    qi = lax.broadcasted_iota(jnp.int32, (blk, 2 * blk), 0)
    kj = lax.broadcasted_iota(jnp.int32, (blk, 2 * blk), 1)
    d = qi + blk - kj
    mask = ((d >= 0) & (d < WINDOW) & ((kj >= blk) | (n > 0))) | (kj == 0)
    bias = jnp.where(mask, 0.0, -jnp.inf)[None]
    low_half = lax.broadcasted_iota(jnp.int32, (blk, V7X_LANES), 1) < HEAD_DIM
    g_of_row = lax.broadcasted_iota(jnp.int32, (rows, V7X_LANES), 0) // blk
    col = lax.broadcasted_iota(jnp.int32, (rows, V7X_LANES), 1)
    q_onehot = jnp.where((col == g_of_row) | (col == g_of_row + GROUP), 1.0, 0.0).astype(BF16)
    keep_low = jnp.where(low_half, 1.0, 0.0).astype(BF16)
    keep_high = jnp.where(low_half, 0.0, 1.0).astype(BF16)
    head_rows = 2 * V7X_SUBLANES
    is_slot0 = lax.broadcasted_iota(jnp.int32, (head_rows, V7X_LANES), 0) == 0
    kcol = lax.broadcasted_iota(jnp.int32, (1, V7X_LANES), 1)
    ones = jnp.ones((2 * blk, V7X_LANES), BF16)
    no_feat = jnp.zeros((2 * blk - head_rows, V7X_LANES), BF16)

    def without_slot0(prev_ref, cur_ref, kvh):
        head = jnp.where(is_slot0, 0.0, prev_ref[:head_rows, _lane_block(kvh)].astype(F32)).astype(BF16)
        return jnp.concatenate([head, prev_ref[head_rows:, _lane_block(kvh)], cur_ref[:, _lane_block(kvh)]], axis=0)

    for kvh in range(N_KV_HEADS):
        sink_row = jnp.zeros((1, V7X_LANES), F32)
        for g in range(GROUP):
            sink_row = jnp.where((kcol == g) | (kcol == g + GROUP), sinks_ref[kvh * GROUP + g], sink_row)
        sink_hi = sink_row.astype(BF16).astype(F32)
        sink_feat = jnp.where(kcol < GROUP, sink_hi, sink_row - sink_hi)
        feat = jnp.concatenate([jnp.where(is_slot0, sink_feat, 0.0).astype(BF16), no_feat], axis=0)
        k_ext = jnp.concatenate([without_slot0(k_prev_ref, k_cur_ref, kvh), feat], axis=1)
        v_ones = jnp.concatenate([without_slot0(v_prev_ref, v_cur_ref, kvh), ones], axis=1)
        q_heads = []
        for pair in range(GROUP // 2):
            q_pair = q_ref[:, _lane_block(kvh * (GROUP // 2) + pair)]
            q_heads += [q_pair * keep_low, q_pair * keep_high]
        q_ext = jnp.concatenate([jnp.concatenate(q_heads, axis=0), q_onehot], axis=1)
        s = lax.dot_general(q_ext, k_ext, (((1,), (1,)), ((), ())), preferred_element_type=F32)
        s = (s.reshape(GROUP, blk, 2 * blk) + bias).reshape(rows, 2 * blk)
        e = jnp.exp(s - jnp.max(s, axis=-1, keepdims=True)).astype(BF16)
        o_sum = jnp.dot(e, v_ones, preferred_element_type=F32)
        o = o_sum[:, :V7X_LANES] * (1.0 / o_sum[:, V7X_LANES:])
        for pair in range(GROUP // 2):
            even = o[(2 * pair) * blk:(2 * pair + 1) * blk, :]
            odd = o[(2 * pair + 1) * blk:(2 * pair + 2) * blk, :]
            o_ref[:, _lane_block(kvh * (GROUP // 2) + pair)] = jnp.where(low_half, even, odd).astype(o_ref.dtype)


def _swa_prompt(q, kv_pair, sinks):
    nb = SEQ // WINDOW
    pair_width = N_KV_HEADS * V7X_LANES

    def cur(part):
        return pl.BlockSpec((WINDOW, pair_width), lambda b, n: (b * nb + n, part))

    def prev(part):
        return pl.BlockSpec((WINDOW, pair_width), lambda b, n: (b * nb + jnp.maximum(n - 1, 0), part))

    vmem = (
        4 * _nbytes((WINDOW, Q_WIDTH), BF16)
        + 8 * _nbytes((WINDOW, pair_width), BF16)
        + 24 * _nbytes((GROUP * WINDOW, 2 * WINDOW), F32)
    )
    return pl.pallas_call(
        _swa_prompt_kernel,
        grid=(BATCH, nb),
        in_specs=[
            pl.BlockSpec(memory_space=pltpu.SMEM),
            pl.BlockSpec((WINDOW, Q_WIDTH), lambda b, n: (b * nb + n, 0)),
            prev(0), cur(0), prev(1), cur(1),
        ],
        out_specs=pl.BlockSpec((WINDOW, Q_WIDTH), lambda b, n: (b * nb + n, 0)),
        out_shape=jax.ShapeDtypeStruct((BATCH * SEQ, Q_WIDTH), BF16),
        compiler_params=_params(("parallel", "arbitrary"), vmem),
        name="swa_prompt",
    )(sinks, q, kv_pair, kv_pair, kv_pair, kv_pair)


def _swa_sample_kernel(sinks_ref, q_ref, kv_ref, ck_ref, cv_ref, o_ref, nk_ref, nv_ref, kall_scr, vall_scr):
    nb = ck_ref.shape[0]
    t_new = DEC_SEQ
    s_len = WINDOW + t_new
    rows = GROUP * t_new
    kall_scr[:, :WINDOW, :] = ck_ref[...]
    vall_scr[:, :WINDOW, :] = cv_ref[...]
    kall_scr[:, WINDOW:, :] = kv_ref[:, :KV_WIDTH].reshape(nb, t_new, KV_WIDTH)
    vall_scr[:, WINDOW:, :] = kv_ref[:, KV_WIDTH:].reshape(nb, t_new, KV_WIDTH)
    nk_ref[...] = kall_scr[:, t_new:, :]
    nv_ref[...] = vall_scr[:, t_new:, :]

    row = lax.broadcasted_iota(jnp.int32, (1, rows, s_len), 1)
    kj = lax.broadcasted_iota(jnp.int32, (1, rows, s_len), 2)
    d = row % t_new + WINDOW - kj
    mask = (d >= 0) & (d < WINDOW)
    g_of_row = lax.broadcasted_iota(jnp.int32, (1, rows, 1), 1) // t_new
    q3 = q_ref[...].reshape(nb, t_new, Q_WIDTH)
    for kvh in range(N_KV_HEADS):
        k = kall_scr[:, :, _head_cols(kvh)].astype(BF16)
        v = vall_scr[:, :, _head_cols(kvh)].astype(BF16)
        q = jnp.concatenate([q3[:, :, _head_cols(kvh * GROUP + g)] for g in range(GROUP)], axis=1).astype(BF16)
        s = jnp.einsum("bqd,bkd->bqk", q, k, preferred_element_type=F32)
        s = jnp.where(mask, s, -jnp.inf)
        e, inv = _sink_softmax_weights(s, _group_sinks(sinks_ref, kvh, g_of_row))
        o = jnp.einsum("bqk,bkd->bqd", e.astype(BF16), v, preferred_element_type=F32) * inv
        for g in range(GROUP):
            o_ref[:, _head_cols(kvh * GROUP + g)] = o[:, g * t_new:(g + 1) * t_new, :].reshape(nb * t_new, HEAD_DIM)


_SWA_SAMPLE_SEQS_PER_STEP = 8


def _swa_sample(q, kv, cache_k, cache_v, sinks):
    nb = _SWA_SAMPLE_SEQS_PER_STEP
    cache_spec = pl.BlockSpec((nb, WINDOW, KV_WIDTH), lambda b: (b, 0, 0))
    all_keys = (nb, WINDOW + DEC_SEQ, KV_WIDTH)
    vmem = 8 * _nbytes((nb, WINDOW, KV_WIDTH), F32) + 2 * _nbytes(all_keys, F32) + 16 * 1024 * 1024
    return pl.pallas_call(
        _swa_sample_kernel,
        grid=(DEC_BATCH // nb,),
        in_specs=[
            pl.BlockSpec(memory_space=pltpu.SMEM),
            pl.BlockSpec((nb * DEC_SEQ, Q_WIDTH), lambda b: (b, 0)),
            pl.BlockSpec((nb * DEC_SEQ, 2 * KV_WIDTH), lambda b: (b, 0)),
            cache_spec,
            cache_spec,
        ],
        out_specs=[pl.BlockSpec((nb * DEC_SEQ, Q_WIDTH), lambda b: (b, 0)), cache_spec, cache_spec],
        out_shape=[
            jax.ShapeDtypeStruct((N_SAMPLE_ROWS, Q_WIDTH), F32),
            jax.ShapeDtypeStruct((DEC_BATCH, WINDOW, KV_WIDTH), F32),
            jax.ShapeDtypeStruct((DEC_BATCH, WINDOW, KV_WIDTH), F32),
        ],
        scratch_shapes=[pltpu.VMEM(all_keys, F32), pltpu.VMEM(all_keys, F32)],
        compiler_params=_params(("parallel",), vmem),
        name="swa_sample",
    )(sinks, q, kv, cache_k, cache_v)


def _cross_kernel(x_ref, gain_ref, wq_ref, mk_ref, mv_ref, wo_ref, out_ref, *, seqs, rows):
    scale = MEM_HEAD_DIM ** -0.5
    x = x_ref[...]
    h = _rms(x, gain_ref[...]).astype(BF16)
    q = jnp.dot(h, wq_ref[...], preferred_element_type=F32)
    heads = []
    for hd in range(N_MEM_HEADS):
        cs = slice(hd * MEM_HEAD_DIM, (hd + 1) * MEM_HEAD_DIM)
        qh = q[:, cs].reshape(seqs, rows, MEM_HEAD_DIM).astype(BF16)
        if mk_ref.ndim == 4:
            k = mk_ref[:, :, hd, :].astype(BF16)
            v = mv_ref[:, :, hd, :].astype(BF16)
        else:
            k = mk_ref[:, :, cs].astype(BF16)
            v = mv_ref[:, :, cs].astype(BF16)
        s = jnp.einsum("bqd,bkd->bqk", qh, k, preferred_element_type=F32) * scale
        e = jnp.exp(s - jnp.max(s, axis=-1, keepdims=True))
        inv = 1.0 / jnp.sum(e, axis=-1, keepdims=True)
        o = jnp.einsum("bqk,bkd->bqd", e.astype(BF16), v, preferred_element_type=F32) * inv
        heads.append(o.reshape(seqs * rows, MEM_HEAD_DIM))
    o_all = jnp.concatenate(heads, axis=-1).astype(BF16)
    out_ref[...] = x + jnp.dot(o_all, wo_ref[...], preferred_element_type=F32)


def _cross_block(x, gain, wq, mem_k, mem_v, wo, layer, *, n_seq, seqs, rows):
    seq_len = x.shape[0] // n_seq
    nt = seq_len // rows if seqs == 1 else 1
    tile = seqs * rows
    x_spec = pl.BlockSpec((tile, D_MODEL), lambda g, i: (g * nt + i, 0))
    mem_tail = mem_k.shape[3:]
    mem_spec = pl.BlockSpec((None, seqs, N_MEM) + mem_tail, lambda g, i: (layer, g, 0) + (0,) * len(mem_tail))
    mem_pad = V7X_SUBLANES // N_MEM_HEADS if len(mem_tail) == 2 else 1
    vmem = (
        4 * _nbytes((tile, D_MODEL), F32)
        + 4 * mem_pad * _nbytes((seqs, N_MEM, MEM_WIDTH), F32)
        + 4 * _nbytes((D_MODEL, MEM_WIDTH), BF16)
        + 4 * _nbytes((tile, D_MODEL), F32)
        + 8 * _nbytes((tile, N_MEM), F32)
    )
    return pl.pallas_call(
        functools.partial(_cross_kernel, seqs=seqs, rows=rows),
        grid=(n_seq // seqs, nt),
        in_specs=[
            x_spec,
            _row_param_spec(layer, D_MODEL),
            pl.BlockSpec((None, D_MODEL, MEM_WIDTH), lambda g, i: (layer, 0, 0)),
            mem_spec,
            mem_spec,
            pl.BlockSpec((None, MEM_WIDTH, D_MODEL), lambda g, i: (layer, 0, 0)),
        ],
        out_specs=x_spec,
        out_shape=jax.ShapeDtypeStruct(x.shape, F32),
        compiler_params=_params(("parallel", "arbitrary"), vmem),
        name="cross_block",
    )(x, _row_param(gain), wq, mem_k, mem_v, wo)


def _log_sigmoid(x):
    return jnp.minimum(x, 0.0) - jnp.log1p(jnp.exp(-jnp.abs(x)))


def _one_minus_exp2(y, exp_y):
    return jnp.tanh(-y) * (1.0 + exp_y * exp_y)


def _lru_gates(xc, wa_ref, ba_ref, wx_ref, bx_ref, lam_ref):
    r_parts = []
    i_parts = []
    for n in range(LRU_BLOCKS):
        cs = slice(n * LRU_BLOCK_SIZE, (n + 1) * LRU_BLOCK_SIZE)
        xb = xc[:, cs].astype(BF16)
        r_parts.append(jnp.dot(xb, wa_ref[n], preferred_element_type=F32))
        i_parts.append(jnp.dot(xb, wx_ref[n], preferred_element_type=F32))
    r = jax.nn.sigmoid(jnp.concatenate(r_parts, axis=-1) + ba_ref[...])
    ig = jax.nn.sigmoid(jnp.concatenate(i_parts, axis=-1) + bx_ref[...])
    log_a = r * (LRU_C * _log_sigmoid(lam_ref[...]))
    a = jnp.exp(log_a)
    z = _one_minus_exp2(log_a, a)
    u = jnp.where(z > 0.0, z * lax.rsqrt(z), 0.0) * (ig * xc)
    return a, u


def _lru_prompt_kernel(xb_ref, gate_ref, x_ref, wc_ref, bc_ref, wa_ref, ba_ref, wx_ref, bx_ref, lam_ref,
                       wo_ref, out_ref, hlast_ref, xext_scr, a_scr, u_scr, h_scr):
    i = pl.program_id(1)
    tm = xb_ref.shape[0]
    pad = V7X_SUBLANES

    @pl.when(i == 0)
    def _():
        xext_scr[0:pad, :] = jnp.zeros((pad, LRU_WIDTH), F32)
        h_scr[...] = jnp.zeros((1, LRU_WIDTH), F32)

    xext_scr[pad:pad + tm, :] = xb_ref[...]
    xc = bc_ref[...] + wc_ref[CONV_WIDTH - 1:CONV_WIDTH, :] * xb_ref[...]
    for k in range(1, CONV_WIDTH):
        xc = xc + wc_ref[CONV_WIDTH - 1 - k:CONV_WIDTH - k, :] * xext_scr[pad - k:pad - k + tm, :]
    xext_scr[0:pad, :] = xext_scr[tm:tm + pad, :]

    a, u = _lru_gates(xc, wa_ref, ba_ref, wx_ref, bx_ref, lam_ref)
    a_scr[...] = a
    u_scr[...] = u

    def step(t, h):
        h = a_scr[pl.ds(t, 1), :] * h + u_scr[pl.ds(t, 1), :]
        u_scr[pl.ds(t, 1), :] = h
        return h

    h_scr[...] = lax.fori_loop(0, tm, step, h_scr[...], unroll=8)
    hlast_ref[...] = u_scr[tm - pad:tm, :]
    y = (u_scr[...] * gate_ref[...]).astype(BF16)
    out_ref[...] = x_ref[...] + jnp.dot(y, wo_ref[...], preferred_element_type=F32)


def _lru_weight_specs(layer_idx):
    def const3(*_):
        return (layer_idx, 0, 0)

    def const4(*_):
        return (layer_idx, 0, 0, 0)

    gate_w = pl.BlockSpec((None, LRU_BLOCKS, LRU_BLOCK_SIZE, LRU_BLOCK_SIZE), const4)
    vec = _row_param_spec(layer_idx, LRU_WIDTH)
    return [
        pl.BlockSpec((None, CONV_WIDTH, LRU_WIDTH), const3),
        vec,
        gate_w,
        vec,
        gate_w,
        vec,
        vec,
        pl.BlockSpec((None, LRU_WIDTH, D_MODEL), const3),
    ]


def _lru_weight_args(w_conv, b_conv, w_gate_a, b_gate_a, w_gate_x, b_gate_x, lam, w_out):
    return (w_conv, _row_param(b_conv), w_gate_a, _row_param(b_gate_a), w_gate_x, _row_param(b_gate_x),
            _row_param(lam), w_out)


def _lru_prompt(xb, gate, x, lw, j, *, tm):
    nt = SEQ // tm
    row_spec0 = pl.BlockSpec((tm, LRU_WIDTH), lambda b, i: (b * nt + i, 0))
    vmem = (
        8 * _nbytes((tm, LRU_WIDTH), F32)
        + 3 * _nbytes((tm + V7X_SUBLANES, LRU_WIDTH), F32)
        + 2 * _nbytes((LRU_WIDTH, D_MODEL), BF16)
        + 4 * _nbytes((LRU_BLOCKS, LRU_BLOCK_SIZE, LRU_BLOCK_SIZE), BF16)
        + 10 * _nbytes((tm, LRU_WIDTH), F32)
    )
    return pl.pallas_call(
        _lru_prompt_kernel,
        grid=(BATCH, nt),
        in_specs=[row_spec0, row_spec0, row_spec0] + _lru_weight_specs(j),
        out_specs=[
            row_spec0,
            pl.BlockSpec((None, V7X_SUBLANES, LRU_WIDTH), lambda b, i: (b, 0, 0)),
        ],
        out_shape=[
            jax.ShapeDtypeStruct((BATCH * SEQ, D_MODEL), F32),
            jax.ShapeDtypeStruct((BATCH, V7X_SUBLANES, LRU_WIDTH), F32),
        ],
        scratch_shapes=[
            pltpu.VMEM((tm + V7X_SUBLANES, LRU_WIDTH), F32),
            pltpu.VMEM((tm, LRU_WIDTH), F32),
            pltpu.VMEM((tm, LRU_WIDTH), F32),
            pltpu.VMEM((1, LRU_WIDTH), F32),
        ],
        compiler_params=_params(("arbitrary", "arbitrary"), vmem),
        name="lru_prompt",
    )(xb, gate, x, *lw)


def _lru_sample_kernel(xb_ref, gate_ref, x_ref, cbuf_ref, h0_ref, wc_ref, bc_ref, wa_ref, ba_ref, wx_ref, bx_ref,
                       lam_ref, wo_ref, out_ref, hs_ref):
    rows = xb_ref.shape[0]
    t = lax.broadcasted_iota(jnp.int32, (rows, LRU_WIDTH), 0) % DEC_SEQ
    xb = xb_ref[...]
    cbuf = cbuf_ref[...]
    xc = bc_ref[...] + wc_ref[CONV_WIDTH - 1:CONV_WIDTH, :] * xb
    for k in range(1, CONV_WIDTH):
        shift_c = CONV_WIDTH - 1 - k
        from_buf = cbuf if shift_c == 0 else pltpu.roll(cbuf, rows - shift_c, 0)
        xk = jnp.where(t >= k, pltpu.roll(xb, k, 0), from_buf)
        xc = xc + wc_ref[CONV_WIDTH - 1 - k:CONV_WIDTH - k, :] * xk

    a, u = _lru_gates(xc, wa_ref, ba_ref, wx_ref, bx_ref, lam_ref)
    k = 1
    while k < DEC_SEQ:
        m = t >= k
        u = jnp.where(m, a * pltpu.roll(u, k, 0) + u, u)
        a = jnp.where(m, a * pltpu.roll(a, k, 0), a)
        k *= 2
    hs = u + a * h0_ref[...]
    hs_ref[...] = hs
    y = (hs * gate_ref[...]).astype(BF16)
    out_ref[...] = x_ref[...] + jnp.dot(y, wo_ref[...], preferred_element_type=F32)


def _lru_sample(xb, gate, x, cbuf_rows, h0_rows, lw, j):
    rows = N_SAMPLE_ROWS
    row_spec0 = pl.BlockSpec((rows, LRU_WIDTH), lambda i: (0, 0))
    vmem = 40 * _nbytes((rows, LRU_WIDTH), F32) + 2 * _nbytes((LRU_WIDTH, D_MODEL), BF16)
    return pl.pallas_call(
        _lru_sample_kernel,
        grid=(1,),
        in_specs=[row_spec0] * 5 + _lru_weight_specs(j),
        out_specs=[row_spec0, row_spec0],
        out_shape=[
            jax.ShapeDtypeStruct((rows, D_MODEL), F32),
            jax.ShapeDtypeStruct((rows, LRU_WIDTH), F32),
        ],
        compiler_params=_params(("arbitrary",), vmem),
        name="lru_sample",
    )(xb, gate, x, cbuf_rows, h0_rows, *lw)


_FFN_TM = 1024
_FFN_TF = 512
_FFN_ROW_CHUNK = 512


def kernel(x_prompt, x_sample, cache_swa_k, cache_swa_v, state_lru_h, state_lru_conv, cache_mem_k, cache_mem_v,
           mem_prompt, ln_ffn1, ffn1_w_in, ffn1_w_out, ln_mix, swa_w_qkv, swa_w_o, swa_sinks, lru_w_in,
           lru_w_conv, lru_b_conv, lru_w_gate_a, lru_b_gate_a, lru_w_gate_x, lru_b_gate_x, lru_lambda,
           lru_w_out, ln_cross, ln_mem, cross_w_q, cross_w_kv, cross_w_o, ln_ffn2, ffn2_w_in, ffn2_w_out,
           ln_final):
    bf = lambda w: w.astype(BF16)
    ffn_seq = []
    for layer in range(DEPTH):
        ffn_seq += [(ffn1_w_in, ffn1_w_out, layer), (ffn2_w_in, ffn2_w_out, layer)]
    ffn_w = (bf(ffn1_w_in[0]), bf(ffn1_w_out[0]))

    def ffn(x, x_extra, gain, layer, final_gain=None):
        nonlocal ffn_w
        nxt = ffn_seq[1] if len(ffn_seq) > 1 else None
        out = _ffn(x, x_extra, gain, layer, *ffn_w, tm=_FFN_TM, tf=_FFN_TF, row_chunk=_FFN_ROW_CHUNK,
                   final_gain=final_gain, cast_next=nxt)
        del ffn_seq[0]
        if nxt is not None:
            ffn_w = out[2:]
        return out[0], out[1]

    swa_w_qkv, swa_w_o = bf(swa_w_qkv), bf(swa_w_o)
    lru_w_in, lru_w_out = bf(lru_w_in), bf(lru_w_out)
    lru_w_gate_a, lru_w_gate_x = bf(lru_w_gate_a), bf(lru_w_gate_x)
    cross_w_q, cross_w_kv, cross_w_o = bf(cross_w_q), bf(cross_w_kv), bf(cross_w_o)

    xp = x_prompt.reshape(N_PROMPT_ROWS, D_MODEL)
    xs = x_sample.reshape(N_SAMPLE_ROWS, D_MODEL)

    mem_rows = mem_prompt.reshape(BATCH * N_MEM, D_MODEL)
    mkv = [
        _norm_linear(mem_rows, ln_mem, layer, cross_w_kv, layer, tm=BATCH * N_MEM, tn=MEM_WIDTH, out_dtype=F32)
        for layer in range(DEPTH)
    ]
    mem_k_prompt = jnp.stack([m[:, :MEM_WIDTH] for m in mkv]).reshape(DEPTH, BATCH, N_MEM, MEM_WIDTH)
    mem_v_prompt = jnp.stack([m[:, MEM_WIDTH:] for m in mkv]).reshape(DEPTH, BATCH, N_MEM, MEM_WIDTH)

    cos_p, sin_p = _rope_tables(jnp.arange(SEQ, dtype=jnp.int32))
    cos_s, sin_s = _rope_tables(PAST_LEN + jnp.arange(N_SAMPLE_ROWS, dtype=jnp.int32) % DEC_SEQ)

    swa_k_p, swa_v_p, swa_k_s, swa_v_s = [], [], [], []
    lru_h_p, lru_conv_p, lru_h_s, lru_conv_s = [], [], [], []

    for layer in range(DEPTH):
        j = layer // 2
        xp, xs = ffn(xp, xs, ln_ffn1, layer)

        if layer % 2 == 0:
            qkv_tm = 1024
            q_p, kv_last, kv_pair = _qkv(xp, ln_mix, layer, swa_w_qkv, j, cos_p, sin_p, tm=qkv_tm, q_dtype=BF16,
                                         seq_len=SEQ)
            o_p = _swa_prompt(q_p, kv_pair, swa_sinks[j])
            xp = _linear_residual(o_p, swa_w_o, j, xp, tm=512)
            kv_tail = kv_last[:, qkv_tm - WINDOW:, :]
            swa_k_p.append(kv_tail[..., :KV_WIDTH].reshape(BATCH, WINDOW, N_KV_HEADS, HEAD_DIM))
            swa_v_p.append(kv_tail[..., KV_WIDTH:].reshape(BATCH, WINDOW, N_KV_HEADS, HEAD_DIM))

            q_s, kv_s = _qkv(xs, ln_mix, layer, swa_w_qkv, j, cos_s, sin_s, tm=N_SAMPLE_ROWS, q_dtype=F32)
            o_s, nk, nv = _swa_sample(
                q_s, kv_s,
                cache_swa_k[j].reshape(DEC_BATCH, WINDOW, KV_WIDTH),
                cache_swa_v[j].reshape(DEC_BATCH, WINDOW, KV_WIDTH),
                swa_sinks[j],
            )
            xs = _linear_residual(o_s, swa_w_o, j, xs, tm=N_SAMPLE_ROWS)
            swa_k_s.append(nk.reshape(DEC_BATCH, WINDOW, N_KV_HEADS, HEAD_DIM))
            swa_v_s.append(nv.reshape(DEC_BATCH, WINDOW, N_KV_HEADS, HEAD_DIM))
        else:
            lw = _lru_weight_args(lru_w_conv, lru_b_conv, lru_w_gate_a, lru_b_gate_a, lru_w_gate_x, lru_b_gate_x,
                                  lru_lambda, lru_w_out)
            xb_p, gate_p = _lru_in(xp, ln_mix, layer, lru_w_in, j, tm=512, tn=LRU_WIDTH)
            xp, h_last = _lru_prompt(xb_p, gate_p, xp, lw, j, tm=256)
            lru_h_p.append(h_last[:, V7X_SUBLANES - 1, :])
            lru_conv_p.append(xb_p.reshape(BATCH, SEQ, LRU_WIDTH)[:, SEQ - (CONV_WIDTH - 1):, :])

            xb_s, gate_s = _lru_in(xs, ln_mix, layer, lru_w_in, j, tm=N_SAMPLE_ROWS, tn=LRU_WIDTH)
            cbuf_rows = jnp.pad(
                state_lru_conv[j], ((0, 0), (0, DEC_SEQ - (CONV_WIDTH - 1)), (0, 0))
            ).reshape(N_SAMPLE_ROWS, LRU_WIDTH)
            h0_rows = jnp.repeat(state_lru_h[j], DEC_SEQ, axis=0)
            xs, hs_s = _lru_sample(xb_s, gate_s, xs, cbuf_rows, h0_rows, lw, j)
            lru_h_s.append(hs_s.reshape(DEC_BATCH, DEC_SEQ, LRU_WIDTH)[:, DEC_SEQ - 1, :])
            xp_tail = jnp.concatenate(
                [state_lru_conv[j], xb_s.reshape(DEC_BATCH, DEC_SEQ, LRU_WIDTH)], axis=1)
            lru_conv_s.append(xp_tail[:, -(CONV_WIDTH - 1):, :])

        xp = _cross_block(xp, ln_cross, cross_w_q, mem_k_prompt, mem_v_prompt, cross_w_o, layer,
                          n_seq=BATCH, seqs=1, rows=512)
        xs = _cross_block(xs, ln_cross, cross_w_q, cache_mem_k, cache_mem_v, cross_w_o, layer,
                          n_seq=DEC_BATCH, seqs=8, rows=DEC_SEQ)

        fin = ln_final if layer == DEPTH - 1 else None
        xp, xs = ffn(xp, xs, ln_ffn2, layer, final_gain=fin)

    return (
        xp.reshape(BATCH, SEQ, D_MODEL),
        xs.reshape(DEC_BATCH, DEC_SEQ, D_MODEL),
        jnp.stack(swa_k_p), jnp.stack(swa_v_p), jnp.stack(swa_k_s), jnp.stack(swa_v_s),
        jnp.stack(lru_h_p), jnp.stack(lru_conv_p), jnp.stack(lru_h_s), jnp.stack(lru_conv_s),
        mem_k_prompt.reshape(DEPTH, BATCH, N_MEM, N_MEM_HEADS, MEM_HEAD_DIM),
        mem_v_prompt.reshape(DEPTH, BATCH, N_MEM, N_MEM_HEADS, MEM_HEAD_DIM),
    )
```

```python
import functools
import math

import jax
import jax.numpy as jnp
from jax import lax
from jax.experimental import pallas as pl
from jax.experimental.pallas import tpu as pltpu

F32 = jnp.float32
BF16 = jnp.bfloat16

D_MODEL = 2048
BATCH = 2
SEQ = 4096
DEPTH = 2
DEC_BATCH = 32
DEC_SEQ = 8
PAST_LEN = 16384
N_HEADS = 32
N_KV_HEADS = 8
HEAD_DIM = 64
GROUP = N_HEADS // N_KV_HEADS
WINDOW = 128
ROPE_THETA = 10000.0
LRU_WIDTH = D_MODEL
LRU_BLOCKS = 8
LRU_BLOCK_SIZE = LRU_WIDTH // LRU_BLOCKS
CONV_WIDTH = 4
LRU_C = 8.0
D_FF = 5632
N_MEM = 256
N_MEM_HEADS = 4
MEM_HEAD_DIM = 128
MEM_WIDTH = N_MEM_HEADS * MEM_HEAD_DIM
NORM_EPS = 1e-6
Q_WIDTH = N_HEADS * HEAD_DIM
KV_WIDTH = N_KV_HEADS * HEAD_DIM

V7X_LANES = 128
V7X_SUBLANES = 8
V7X_VMEM_BYTES = 64 * 1024 * 1024
V7X_VMEM_USABLE_BYTES = V7X_VMEM_BYTES - 2 * 1024 * 1024

N_PROMPT_ROWS = BATCH * SEQ
N_SAMPLE_ROWS = DEC_BATCH * DEC_SEQ


def _params(semantics, vmem_bytes):
    return pltpu.CompilerParams(
        dimension_semantics=semantics,
        vmem_limit_bytes=int(min(max(vmem_bytes, 32 * 1024 * 1024), V7X_VMEM_USABLE_BYTES)),
    )


def _nbytes(shape, dtype):
    return math.prod(shape) * jnp.dtype(dtype).itemsize


def _rms(x, gain):
    ms = jnp.mean(x * x, axis=-1, keepdims=True)
    return x * lax.rsqrt(ms + NORM_EPS) * gain


def _lane_block(c):
    return slice(c * V7X_LANES, (c + 1) * V7X_LANES)


def _row_param(p):
    return p.reshape(p.shape[0], 1, p.shape[-1])


def _row_param_spec(layer, width):
    return pl.BlockSpec((None, 1, width), lambda *_: (layer, 0, 0))


def _for_chunks(n_rows, chunk, body):
    n_chunks = n_rows // chunk
    if n_chunks == 1:
        body(pl.ds(0, n_rows))
        return

    def step(c, carry):
        body(pl.ds(pl.multiple_of(c * chunk, chunk), chunk))
        return carry

    lax.fori_loop(0, n_chunks, step, 0)


def _ffn_kernel(x_ref, gain_ref, wg_ref, wu_ref, wo_ref, xe_ref, *rest, row_chunk, final_norm, cast_next):
    rest = list(rest)
    gfin_ref = rest.pop(0) if final_norm else None
    next_in_ref, next_out_ref = (rest.pop(0), rest.pop(0)) if cast_next else (None, None)
    o_ref, oe_ref = rest.pop(0), rest.pop(0)
    next_in_bf_ref, next_out_bf_ref = (rest.pop(0), rest.pop(0)) if cast_next else (None, None)
    h_scr, he_scr = rest
    f = pl.program_id(1)

    def run(x_ref, o_ref, h_scr, chunk, side_job=None):
        n_rows = x_ref.shape[0]

        @pl.when(f == 0)
        def _():
            def prologue(rows):
                h_scr[rows, :] = _rms(x_ref[rows, :], gain_ref[...]).astype(BF16)
                o_ref[rows, :] = jnp.zeros((chunk, D_MODEL), F32)

            _for_chunks(n_rows, chunk, prologue)

        chunks = [pl.ds(c * chunk, chunk) for c in range(n_rows // chunk)]
        acts = []
        for rows in chunks:
            h = h_scr[rows, :]
            g = jnp.dot(h, wg_ref[...], preferred_element_type=F32)
            u = jnp.dot(h, wu_ref[...], preferred_element_type=F32)
            acts.append((g * jax.nn.sigmoid(g) * u).astype(BF16))
        for rows, a in zip(chunks, acts):
            o_ref[rows, :] += jnp.dot(a, wo_ref[...], preferred_element_type=F32)
        if side_job is not None:
            side_job()

        @pl.when(f == pl.num_programs(1) - 1)
        def _():
            def epilogue(rows):
                y = x_ref[rows, :] + 0.5 * o_ref[rows, :]
                if final_norm:
                    y = _rms(y, gfin_ref[...])
                o_ref[rows, :] = y

            _for_chunks(n_rows, chunk, epilogue)

    def round_next_weights():
        next_in_bf_ref[...] = next_in_ref[...].astype(BF16)
        next_out_bf_ref[...] = next_out_ref[...].astype(BF16)

    run(x_ref, o_ref, h_scr, row_chunk, side_job=round_next_weights if cast_next else None)

    @pl.when(pl.program_id(0) == 0)
    def _():
        run(xe_ref, oe_ref, he_scr, xe_ref.shape[0])


def _ffn(x, x_extra, gain, layer, w_in, w_out, *, tm, tf, row_chunk, final_gain=None, cast_next=None):
    m = x.shape[0]
    me = x_extra.shape[0]
    n_tiles = m // tm
    nf = D_FF // tf
    final_norm = final_gain is not None
    row_spec = pl.BlockSpec((tm, D_MODEL), lambda i, f: (i, 0))
    extra_spec = pl.BlockSpec((me, D_MODEL), lambda i, f: (0, 0))
    in_specs = [
        row_spec,
        _row_param_spec(layer, D_MODEL),
        pl.BlockSpec((D_MODEL, tf), lambda i, f: (0, f)),
        pl.BlockSpec((D_MODEL, tf), lambda i, f: (0, f + nf)),
        pl.BlockSpec((tf, D_MODEL), lambda i, f: (f, 0)),
        pl.BlockSpec((me, D_MODEL), lambda i, f: (0, 0), pipeline_mode=pl.Buffered(1)),
    ]
    args = [x, _row_param(gain), w_in, w_in, w_out, x_extra]
    out_specs = [row_spec, extra_spec]
    out_shape = [jax.ShapeDtypeStruct((m, D_MODEL), F32), jax.ShapeDtypeStruct((me, D_MODEL), F32)]
    if final_norm:
        in_specs.append(_row_param_spec(0, D_MODEL))
        args.append(final_gain.reshape(1, 1, D_MODEL))
    vmem = (
        4 * _nbytes((tm, D_MODEL), F32)
        + _nbytes((tm, D_MODEL), BF16)
        + 4 * _nbytes((me, D_MODEL), F32)
        + _nbytes((me, D_MODEL), BF16)
        + 2 * 3 * _nbytes((D_MODEL, tf), BF16)
        + 4 * _nbytes((row_chunk, tf), F32)
        + 2 * _nbytes((row_chunk, D_MODEL), F32)
    )
    if cast_next is not None:
        nw_in, nw_out, nl = cast_next
        in_blk = (D_MODEL // n_tiles, 2 * D_FF // nf)
        out_blk = (D_FF // nf, D_MODEL // n_tiles)
        in_specs += [
            pl.BlockSpec((None,) + in_blk, lambda i, f: (nl, i, f)),
            pl.BlockSpec((None,) + out_blk, lambda i, f: (nl, f, i)),
        ]
        args += [nw_in, nw_out]
        out_specs += [pl.BlockSpec(in_blk, lambda i, f: (i, f)), pl.BlockSpec(out_blk, lambda i, f: (f, i))]
        out_shape += [jax.ShapeDtypeStruct(nw_in.shape[1:], BF16), jax.ShapeDtypeStruct(nw_out.shape[1:], BF16)]
        vmem += 2 * (_nbytes(in_blk, F32) + _nbytes(in_blk, BF16) + _nbytes(out_blk, F32) + _nbytes(out_blk, BF16))
    return pl.pallas_call(
        functools.partial(_ffn_kernel, row_chunk=row_chunk, final_norm=final_norm, cast_next=cast_next is not None),
        grid=(n_tiles, nf),
        in_specs=in_specs,
        out_specs=out_specs,
        out_shape=out_shape,
        scratch_shapes=[pltpu.VMEM((tm, D_MODEL), BF16), pltpu.VMEM((me, D_MODEL), BF16)],
        compiler_params=_params(("arbitrary", "arbitrary"), vmem),
        name="ffn",
    )(*args)


def _gelu_tanh(y):
    return 0.5 * y * (1.0 + jnp.tanh(math.sqrt(2.0 / math.pi) * (y + 0.044715 * (y * y * y))))


def _norm_linear_kernel(x_ref, gain_ref, w_ref, o_ref, h_scr):
    @pl.when(pl.program_id(1) == 0)
    def _():
        h_scr[...] = _rms(x_ref[...], gain_ref[...]).astype(BF16)

    o_ref[...] = jnp.dot(h_scr[...], w_ref[...], preferred_element_type=F32).astype(o_ref.dtype)


def _lru_in_kernel(x_ref, gain_ref, wx_ref, wg_ref, xb_ref, gate_ref, h_scr):
    @pl.when(pl.program_id(1) == 0)
    def _():
        h_scr[...] = _rms(x_ref[...], gain_ref[...]).astype(BF16)

    h = h_scr[...]
    gate_ref[...] = _gelu_tanh(jnp.dot(h, wg_ref[...], preferred_element_type=F32))
    xb_ref[...] = jnp.dot(h, wx_ref[...], preferred_element_type=F32)


def _lru_in(x, gain, gain_layer, w, w_layer, *, tm, tn):
    m = x.shape[0]
    n_col = LRU_WIDTH // tn
    vmem = (
        2 * _nbytes((tm, D_MODEL), F32)
        + _nbytes((tm, D_MODEL), BF16)
        + 4 * _nbytes((D_MODEL, tn), BF16)
        + 10 * _nbytes((tm, tn), F32)
    )
    out_spec = pl.BlockSpec((tm, tn), lambda i, j: (i, j))
    return pl.pallas_call(
        _lru_in_kernel,
        grid=(m // tm, n_col),
        in_specs=[
            pl.BlockSpec((tm, D_MODEL), lambda i, j: (i, 0)),
            _row_param_spec(gain_layer, D_MODEL),
            pl.BlockSpec((None, D_MODEL, tn), lambda i, j: (w_layer, 0, j)),
            pl.BlockSpec((None, D_MODEL, tn), lambda i, j: (w_layer, 0, n_col + j)),
        ],
        out_specs=[out_spec, out_spec],
        out_shape=[jax.ShapeDtypeStruct((m, LRU_WIDTH), F32)] * 2,
        scratch_shapes=[pltpu.VMEM((tm, D_MODEL), BF16)],
        compiler_params=_params(("parallel", "arbitrary"), vmem),
        name="lru_in",
    )(x, _row_param(gain), w, w)


def _norm_linear(x, gain, gain_layer, w, w_layer, *, tm, tn, out_dtype):
    m = x.shape[0]
    n = w.shape[-1]
    vmem = (
        2 * _nbytes((tm, D_MODEL), F32)
        + _nbytes((tm, D_MODEL), BF16)
        + 2 * _nbytes((D_MODEL, tn), BF16)
        + 4 * _nbytes((tm, tn), F32)
    )
    return pl.pallas_call(
        _norm_linear_kernel,
        grid=(m // tm, n // tn),
        in_specs=[
            pl.BlockSpec((tm, D_MODEL), lambda i, j: (i, 0)),
            _row_param_spec(gain_layer, D_MODEL),
            pl.BlockSpec((None, D_MODEL, tn), lambda i, j: (w_layer, 0, j)),
        ],
        out_specs=pl.BlockSpec((tm, tn), lambda i, j: (i, j)),
        out_shape=jax.ShapeDtypeStruct((m, n), out_dtype),
        scratch_shapes=[pltpu.VMEM((tm, D_MODEL), BF16)],
        compiler_params=_params(("parallel", "arbitrary"), vmem),
        name="norm_linear",
    )(x, _row_param(gain), w)


_QKV_STEPS = 2
_Q_TN = Q_WIDTH // _QKV_STEPS
_ATTN_SCALE = HEAD_DIM ** -0.5
assert math.frexp(_ATTN_SCALE)[0] == 0.5
assert 2 * KV_WIDTH // _QKV_STEPS == KV_WIDTH


def _rope_tables(positions):
    half = HEAD_DIM // 2
    inv = ROPE_THETA ** (-jnp.arange(half, dtype=F32) * (2.0 / HEAD_DIM))
    ang = positions.astype(F32)[:, None] * inv[None, :]
    cos = jnp.cos(ang)
    sin = jnp.sin(ang)
    reps = V7X_LANES // HEAD_DIM
    cos_t = jnp.tile(jnp.concatenate([cos, cos], axis=-1), (1, reps))
    sin_t = jnp.tile(jnp.concatenate([-sin, sin], axis=-1), (1, reps))
    return jnp.stack([cos_t, jnp.ones_like(cos_t)]), jnp.stack([sin_t, jnp.zeros_like(sin_t)])


def _qkv_kernel(x_ref, gain_ref, wq_ref, wkv_ref, cos_ref, sin_ref, cos_kv_ref, sin_kv_ref, q_ref, kv_ref,
                pair_ref=None):
    tm = x_ref.shape[0]
    lane = lax.broadcasted_iota(jnp.int32, (tm, V7X_LANES), 1)
    first_half = (lane % HEAD_DIM) < (HEAD_DIM // 2)
    low_half = lane < HEAD_DIM

    def rope(yc, cos, sin):
        rot = jnp.where(
            first_half,
            pltpu.roll(yc, V7X_LANES - HEAD_DIM // 2, 1),
            pltpu.roll(yc, HEAD_DIM // 2, 1),
        )
        return yc * cos + rot * sin

    h = _rms(x_ref[...], gain_ref[...]).astype(BF16)
    yq = jnp.dot(h, wq_ref[...], preferred_element_type=F32)
    cos, sin = cos_ref[...], sin_ref[...]
    for c in range(_Q_TN // V7X_LANES):
        q_ref[:, _lane_block(c)] = (rope(yq[:, _lane_block(c)], cos, sin) * _ATTN_SCALE).astype(q_ref.dtype)

    ykv = jnp.dot(h, wkv_ref[...], preferred_element_type=F32)
    cos_kv, sin_kv = cos_kv_ref[...], sin_kv_ref[...]
    heads_per_block = V7X_LANES // HEAD_DIM
    for c in range(KV_WIDTH // V7X_LANES):
        r = rope(ykv[:, _lane_block(c)], cos_kv, sin_kv)
        kv_ref[:, _lane_block(c)] = r
        if pair_ref is not None:
            swapped = pltpu.roll(r, HEAD_DIM, 1)
            pair_ref[:, _lane_block(heads_per_block * c)] = jnp.where(low_half, r, swapped).astype(BF16)
            pair_ref[:, _lane_block(heads_per_block * c + 1)] = jnp.where(low_half, swapped, r).astype(BF16)


def _qkv(x, gain, gain_layer, w, w_layer, cos_t, sin_t, *, tm, q_dtype, seq_len=None):
    m = x.shape[0]
    n_tab = cos_t.shape[1] // tm
    paired_kv = seq_len is not None
    pair_width = N_KV_HEADS * V7X_LANES
    vmem = (
        2 * _nbytes((tm, D_MODEL), F32)
        + _nbytes((tm, D_MODEL), BF16)
        + 2 * _nbytes((D_MODEL, _Q_TN + KV_WIDTH), BF16)
        + 8 * _nbytes((tm, _Q_TN + KV_WIDTH), F32)
        + 4 * _nbytes((tm, pair_width), BF16)
    )
    out_specs = [pl.BlockSpec((tm, _Q_TN), lambda j, i: (i, j))]
    out_shape = [jax.ShapeDtypeStruct((m, Q_WIDTH), q_dtype)]
    if paired_kv:
        tiles_per_seq = seq_len // tm
        out_specs.append(pl.BlockSpec((None, tm, KV_WIDTH), lambda j, i: (i // tiles_per_seq, 0, j)))
        out_shape.append(jax.ShapeDtypeStruct((m // seq_len, tm, 2 * KV_WIDTH), F32))
        out_specs.append(pl.BlockSpec((tm, pair_width), lambda j, i: (i, j)))
        out_shape.append(jax.ShapeDtypeStruct((m, 2 * pair_width), BF16))
    else:
        out_specs.append(pl.BlockSpec((tm, KV_WIDTH), lambda j, i: (i, j)))
        out_shape.append(jax.ShapeDtypeStruct((m, 2 * KV_WIDTH), F32))
    rot_spec = pl.BlockSpec((None, tm, V7X_LANES), lambda j, i: (0, i % n_tab, 0))
    rot_or_id_spec = pl.BlockSpec((None, tm, V7X_LANES), lambda j, i: (j, i % n_tab, 0))
    return pl.pallas_call(
        _qkv_kernel,
        grid=(_QKV_STEPS, m // tm),
        in_specs=[
            pl.BlockSpec((tm, D_MODEL), lambda j, i: (i, 0)),
            _row_param_spec(gain_layer, D_MODEL),
            pl.BlockSpec((None, D_MODEL, _Q_TN), lambda j, i: (w_layer, 0, j)),
            pl.BlockSpec((None, D_MODEL, KV_WIDTH), lambda j, i: (w_layer, 0, Q_WIDTH // KV_WIDTH + j)),
            rot_spec, rot_spec, rot_or_id_spec, rot_or_id_spec,
        ],
        out_specs=out_specs,
        out_shape=out_shape,
        compiler_params=_params(("arbitrary", "arbitrary"), vmem),
        name="qkv_rope",
    )(x, _row_param(gain), w, w, cos_t, sin_t, cos_t, sin_t)


def _linear_residual_kernel(o_ref, w_ref, x_ref, out_ref):
    out_ref[...] = x_ref[...] + jnp.dot(o_ref[...].astype(BF16), w_ref[...], preferred_element_type=F32)


def _linear_residual(o, w, layer, x, *, tm):
    m, k = o.shape
    vmem = (
        2 * _nbytes((tm, k), o.dtype)
        + 2 * _nbytes((k, D_MODEL), BF16)
        + 6 * _nbytes((tm, D_MODEL), F32)
    )
    return pl.pallas_call(
        _linear_residual_kernel,
        grid=(m // tm,),
        in_specs=[
            pl.BlockSpec((tm, k), lambda i: (i, 0)),
            pl.BlockSpec((None, k, D_MODEL), lambda i: (layer, 0, 0)),
            pl.BlockSpec((tm, D_MODEL), lambda i: (i, 0)),
        ],
        out_specs=pl.BlockSpec((tm, D_MODEL), lambda i: (i, 0)),
        out_shape=jax.ShapeDtypeStruct((m, D_MODEL), F32),
        compiler_params=_params(("parallel",), vmem),
        name="linear_residual",
    )(o, w, x)


def _head_cols(h):
    return slice(h * HEAD_DIM, (h + 1) * HEAD_DIM)


def _group_sinks(sinks_ref, kvh, g_of_row):
    sink = jnp.full(g_of_row.shape, sinks_ref[kvh * GROUP], F32)
    for g in range(1, GROUP):
        sink = jnp.where(g_of_row == g, sinks_ref[kvh * GROUP + g], sink)
    return sink


def _sink_softmax_weights(s, sink):
    m = jnp.maximum(jnp.max(s, axis=-1, keepdims=True), sink)
    e = jnp.exp(s - m)
    denom = jnp.sum(e, axis=-1, keepdims=True) + jnp.exp(sink - m)
    return e, 1.0 / denom


def _swa_prompt_kernel(sinks_ref, q_ref, k_prev_ref, k_cur_ref, v_prev_ref, v_cur_ref, o_ref):
    n = pl.program_id(1)
    blk = WINDOW
    rows = GROUP * blk
    qi = lax.broadcasted_iota(jnp.int32, (blk, 2 * blk), 0)
    kj = lax.broadcasted_iota(jnp.int32, (blk, 2 * blk), 1)
    d = qi + blk - kj
    mask = ((d >= 0) & (d < WINDOW) & ((kj >= blk) | (n > 0))) | (kj == 0)
    bias = jnp.where(mask, 0.0, -jnp.inf)[None]
    low_half = lax.broadcasted_iota(jnp.int32, (blk, V7X_LANES), 1) < HEAD_DIM
    g_of_row = lax.broadcasted_iota(jnp.int32, (rows, V7X_LANES), 0) // blk
    col = lax.broadcasted_iota(jnp.int32, (rows, V7X_LANES), 1)
    q_onehot = jnp.where((col == g_of_row) | (col == g_of_row + GROUP), 1.0, 0.0).astype(BF16)
    keep_low = jnp.where(low_half, 1.0, 0.0).astype(BF16)
    keep_high = jnp.where(low_half, 0.0, 1.0).astype(BF16)
    head_rows = 2 * V7X_SUBLANES
    is_slot0 = lax.broadcasted_iota(jnp.int32, (head_rows, V7X_LANES), 0) == 0
    kcol = lax.broadcasted_iota(jnp.int32, (1, V7X_LANES), 1)
    ones = jnp.ones((2 * blk, V7X_LANES), BF16)
    no_feat = jnp.zeros((2 * blk - head_rows, V7X_LANES), BF16)

    def without_slot0(prev_ref, cur_ref, kvh):
        head = jnp.where(is_slot0, 0.0, prev_ref[:head_rows, _lane_block(kvh)].astype(F32)).astype(BF16)
        return jnp.concatenate([head, prev_ref[head_rows:, _lane_block(kvh)], cur_ref[:, _lane_block(kvh)]], axis=0)

    for kvh in range(N_KV_HEADS):
        sink_row = jnp.zeros((1, V7X_LANES), F32)
        for g in range(GROUP):
            sink_row = jnp.where((kcol == g) | (kcol == g + GROUP), sinks_ref[kvh * GROUP + g], sink_row)
        sink_hi = sink_row.astype(BF16).astype(F32)
        sink_feat = jnp.where(kcol < GROUP, sink_hi, sink_row - sink_hi)
        feat = jnp.concatenate([jnp.where(is_slot0, sink_feat, 0.0).astype(BF16), no_feat], axis=0)
        k_ext = jnp.concatenate([without_slot0(k_prev_ref, k_cur_ref, kvh), feat], axis=1)
        v_ones = jnp.concatenate([without_slot0(v_prev_ref, v_cur_ref, kvh), ones], axis=1)
        q_heads = []
        for pair in range(GROUP // 2):
            q_pair = q_ref[:, _lane_block(kvh * (GROUP // 2) + pair)]
            q_heads += [q_pair * keep_low, q_pair * keep_high]
        q_ext = jnp.concatenate([jnp.concatenate(q_heads, axis=0), q_onehot], axis=1)
        s = lax.dot_general(q_ext, k_ext, (((1,), (1,)), ((), ())), preferred_element_type=F32)
        s = (s.reshape(GROUP, blk, 2 * blk) + bias).reshape(rows, 2 * blk)
        e = jnp.exp(s - jnp.max(s, axis=-1, keepdims=True)).astype(BF16)
        o_sum = jnp.dot(e, v_ones, preferred_element_type=F32)
        o = o_sum[:, :V7X_LANES] * (1.0 / o_sum[:, V7X_LANES:])
        for pair in range(GROUP // 2):
            even = o[(2 * pair) * blk:(2 * pair + 1) * blk, :]
            odd = o[(2 * pair + 1) * blk:(2 * pair + 2) * blk, :]
            o_ref[:, _lane_block(kvh * (GROUP // 2) + pair)] = jnp.where(low_half, even, odd).astype(o_ref.dtype)


def _swa_prompt(q, kv_pair, sinks):
    nb = SEQ // WINDOW
    pair_width = N_KV_HEADS * V7X_LANES

    def cur(part):
        return pl.BlockSpec((WINDOW, pair_width), lambda b, n: (b * nb + n, part))

    def prev(part):
        return pl.BlockSpec((WINDOW, pair_width), lambda b, n: (b * nb + jnp.maximum(n - 1, 0), part))

    vmem = (
        4 * _nbytes((WINDOW, Q_WIDTH), BF16)
        + 8 * _nbytes((WINDOW, pair_width), BF16)
        + 24 * _nbytes((GROUP * WINDOW, 2 * WINDOW), F32)
    )
    return pl.pallas_call(
        _swa_prompt_kernel,
        grid=(BATCH, nb),
        in_specs=[
            pl.BlockSpec(memory_space=pltpu.SMEM),
            pl.BlockSpec((WINDOW, Q_WIDTH), lambda b, n: (b * nb + n, 0)),
            prev(0), cur(0), prev(1), cur(1),
        ],
        out_specs=pl.BlockSpec((WINDOW, Q_WIDTH), lambda b, n: (b * nb + n, 0)),
        out_shape=jax.ShapeDtypeStruct((BATCH * SEQ, Q_WIDTH), BF16),
        compiler_params=_params(("parallel", "arbitrary"), vmem),
        name="swa_prompt",
    )(sinks, q, kv_pair, kv_pair, kv_pair, kv_pair)


def _swa_sample_kernel(sinks_ref, q_ref, kv_ref, ck_ref, cv_ref, o_ref, nk_ref, nv_ref, kall_scr, vall_scr):
    nb = ck_ref.shape[0]
    t_new = DEC_SEQ
    s_len = WINDOW + t_new
    rows = GROUP * t_new
    kall_scr[:, :WINDOW, :] = ck_ref[...]
    vall_scr[:, :WINDOW, :] = cv_ref[...]
    kall_scr[:, WINDOW:, :] = kv_ref[:, :KV_WIDTH].reshape(nb, t_new, KV_WIDTH)
    vall_scr[:, WINDOW:, :] = kv_ref[:, KV_WIDTH:].reshape(nb, t_new, KV_WIDTH)
    nk_ref[...] = kall_scr[:, t_new:, :]
    nv_ref[...] = vall_scr[:, t_new:, :]

    row = lax.broadcasted_iota(jnp.int32, (1, rows, s_len), 1)
    kj = lax.broadcasted_iota(jnp.int32, (1, rows, s_len), 2)
    d = row % t_new + WINDOW - kj
    mask = (d >= 0) & (d < WINDOW)
    g_of_row = lax.broadcasted_iota(jnp.int32, (1, rows, 1), 1) // t_new
    q3 = q_ref[...].reshape(nb, t_new, Q_WIDTH)
    for kvh in range(N_KV_HEADS):
        k = kall_scr[:, :, _head_cols(kvh)].astype(BF16)
        v = vall_scr[:, :, _head_cols(kvh)].astype(BF16)
        q = jnp.concatenate([q3[:, :, _head_cols(kvh * GROUP + g)] for g in range(GROUP)], axis=1).astype(BF16)
        s = jnp.einsum("bqd,bkd->bqk", q, k, preferred_element_type=F32)
        s = jnp.where(mask, s, -jnp.inf)
        e, inv = _sink_softmax_weights(s, _group_sinks(sinks_ref, kvh, g_of_row))
        o = jnp.einsum("bqk,bkd->bqd", e.astype(BF16), v, preferred_element_type=F32) * inv
        for g in range(GROUP):
            o_ref[:, _head_cols(kvh * GROUP + g)] = o[:, g * t_new:(g + 1) * t_new, :].reshape(nb * t_new, HEAD_DIM)


_SWA_SAMPLE_SEQS_PER_STEP = 8


def _swa_sample(q, kv, cache_k, cache_v, sinks):
    nb = _SWA_SAMPLE_SEQS_PER_STEP
    cache_spec = pl.BlockSpec((nb, WINDOW, KV_WIDTH), lambda b: (b, 0, 0))
    all_keys = (nb, WINDOW + DEC_SEQ, KV_WIDTH)
    vmem = 8 * _nbytes((nb, WINDOW, KV_WIDTH), F32) + 2 * _nbytes(all_keys, F32) + 16 * 1024 * 1024
    return pl.pallas_call(
        _swa_sample_kernel,
        grid=(DEC_BATCH // nb,),
        in_specs=[
            pl.BlockSpec(memory_space=pltpu.SMEM),
            pl.BlockSpec((nb * DEC_SEQ, Q_WIDTH), lambda b: (b, 0)),
            pl.BlockSpec((nb * DEC_SEQ, 2 * KV_WIDTH), lambda b: (b, 0)),
            cache_spec,
            cache_spec,
        ],
        out_specs=[pl.BlockSpec((nb * DEC_SEQ, Q_WIDTH), lambda b: (b, 0)), cache_spec, cache_spec],
        out_shape=[
            jax.ShapeDtypeStruct((N_SAMPLE_ROWS, Q_WIDTH), F32),
            jax.ShapeDtypeStruct((DEC_BATCH, WINDOW, KV_WIDTH), F32),
            jax.ShapeDtypeStruct((DEC_BATCH, WINDOW, KV_WIDTH), F32),
        ],
        scratch_shapes=[pltpu.VMEM(all_keys, F32), pltpu.VMEM(all_keys, F32)],
        compiler_params=_params(("parallel",), vmem),
        name="swa_sample",
    )(sinks, q, kv, cache_k, cache_v)


def _cross_kernel(x_ref, gain_ref, wq_ref, mk_ref, mv_ref, wo_ref, out_ref, *, seqs, rows):
    scale = MEM_HEAD_DIM ** -0.5
    x = x_ref[...]
    h = _rms(x, gain_ref[...]).astype(BF16)
    q = jnp.dot(h, wq_ref[...], preferred_element_type=F32)
    heads = []
    for hd in range(N_MEM_HEADS):
        cs = slice(hd * MEM_HEAD_DIM, (hd + 1) * MEM_HEAD_DIM)
        qh = q[:, cs].reshape(seqs, rows, MEM_HEAD_DIM).astype(BF16)
        if mk_ref.ndim == 4:
            k = mk_ref[:, :, hd, :].astype(BF16)
            v = mv_ref[:, :, hd, :].astype(BF16)
        else:
            k = mk_ref[:, :, cs].astype(BF16)
            v = mv_ref[:, :, cs].astype(BF16)
        s = jnp.einsum("bqd,bkd->bqk", qh, k, preferred_element_type=F32) * scale
        e = jnp.exp(s - jnp.max(s, axis=-1, keepdims=True))
        inv = 1.0 / jnp.sum(e, axis=-1, keepdims=True)
        o = jnp.einsum("bqk,bkd->bqd", e.astype(BF16), v, preferred_element_type=F32) * inv
        heads.append(o.reshape(seqs * rows, MEM_HEAD_DIM))
    o_all = jnp.concatenate(heads, axis=-1).astype(BF16)
    out_ref[...] = x + jnp.dot(o_all, wo_ref[...], preferred_element_type=F32)


def _cross_block(x, gain, wq, mem_k, mem_v, wo, layer, *, n_seq, seqs, rows):
    seq_len = x.shape[0] // n_seq
    nt = seq_len // rows if seqs == 1 else 1
    tile = seqs * rows
    x_spec = pl.BlockSpec((tile, D_MODEL), lambda g, i: (g * nt + i, 0))
    mem_tail = mem_k.shape[3:]
    mem_spec = pl.BlockSpec((None, seqs, N_MEM) + mem_tail, lambda g, i: (layer, g, 0) + (0,) * len(mem_tail))
    mem_pad = V7X_SUBLANES // N_MEM_HEADS if len(mem_tail) == 2 else 1
    vmem = (
        4 * _nbytes((tile, D_MODEL), F32)
        + 4 * mem_pad * _nbytes((seqs, N_MEM, MEM_WIDTH), F32)
        + 4 * _nbytes((D_MODEL, MEM_WIDTH), BF16)
        + 4 * _nbytes((tile, D_MODEL), F32)
        + 8 * _nbytes((tile, N_MEM), F32)
    )
    return pl.pallas_call(
        functools.partial(_cross_kernel, seqs=seqs, rows=rows),
        grid=(n_seq // seqs, nt),
        in_specs=[
            x_spec,
            _row_param_spec(layer, D_MODEL),
            pl.BlockSpec((None, D_MODEL, MEM_WIDTH), lambda g, i: (layer, 0, 0)),
            mem_spec,
            mem_spec,
            pl.BlockSpec((None, MEM_WIDTH, D_MODEL), lambda g, i: (layer, 0, 0)),
        ],
        out_specs=x_spec,
        out_shape=jax.ShapeDtypeStruct(x.shape, F32),
        compiler_params=_params(("parallel", "arbitrary"), vmem),
        name="cross_block",
    )(x, _row_param(gain), wq, mem_k, mem_v, wo)


def _log_sigmoid(x):
    return jnp.minimum(x, 0.0) - jnp.log1p(jnp.exp(-jnp.abs(x)))


def _one_minus_exp2(y, exp_y):
    return jnp.tanh(-y) * (1.0 + exp_y * exp_y)


def _lru_gates(xc, wa_ref, ba_ref, wx_ref, bx_ref, lam_ref):
    r_parts = []
    i_parts = []
    for n in range(LRU_BLOCKS):
        cs = slice(n * LRU_BLOCK_SIZE, (n + 1) * LRU_BLOCK_SIZE)
        xb = xc[:, cs].astype(BF16)
        r_parts.append(jnp.dot(xb, wa_ref[n], preferred_element_type=F32))
        i_parts.append(jnp.dot(xb, wx_ref[n], preferred_element_type=F32))
    r = jax.nn.sigmoid(jnp.concatenate(r_parts, axis=-1) + ba_ref[...])
    ig = jax.nn.sigmoid(jnp.concatenate(i_parts, axis=-1) + bx_ref[...])
    log_a = r * (LRU_C * _log_sigmoid(lam_ref[...]))
    a = jnp.exp(log_a)
    z = _one_minus_exp2(log_a, a)
    u = jnp.where(z > 0.0, z * lax.rsqrt(z), 0.0) * (ig * xc)
    return a, u


def _lru_prompt_kernel(xb_ref, gate_ref, x_ref, wc_ref, bc_ref, wa_ref, ba_ref, wx_ref, bx_ref, lam_ref,
                       wo_ref, out_ref, hlast_ref, xext_scr, a_scr, u_scr, h_scr):
    i = pl.program_id(1)
    tm = xb_ref.shape[0]
    pad = V7X_SUBLANES

    @pl.when(i == 0)
    def _():
        xext_scr[0:pad, :] = jnp.zeros((pad, LRU_WIDTH), F32)
        h_scr[...] = jnp.zeros((1, LRU_WIDTH), F32)

    xext_scr[pad:pad + tm, :] = xb_ref[...]
    xc = bc_ref[...] + wc_ref[CONV_WIDTH - 1:CONV_WIDTH, :] * xb_ref[...]
    for k in range(1, CONV_WIDTH):
        xc = xc + wc_ref[CONV_WIDTH - 1 - k:CONV_WIDTH - k, :] * xext_scr[pad - k:pad - k + tm, :]
    xext_scr[0:pad, :] = xext_scr[tm:tm + pad, :]

    a, u = _lru_gates(xc, wa_ref, ba_ref, wx_ref, bx_ref, lam_ref)
    a_scr[...] = a
    u_scr[...] = u

    def step(t, h):
        h = a_scr[pl.ds(t, 1), :] * h + u_scr[pl.ds(t, 1), :]
        u_scr[pl.ds(t, 1), :] = h
        return h

    h_scr[...] = lax.fori_loop(0, tm, step, h_scr[...], unroll=8)
    hlast_ref[...] = u_scr[tm - pad:tm, :]
    y = (u_scr[...] * gate_ref[...]).astype(BF16)
    out_ref[...] = x_ref[...] + jnp.dot(y, wo_ref[...], preferred_element_type=F32)


def _lru_weight_specs(layer_idx):
    def const3(*_):
        return (layer_idx, 0, 0)

    def const4(*_):
        return (layer_idx, 0, 0, 0)

    gate_w = pl.BlockSpec((None, LRU_BLOCKS, LRU_BLOCK_SIZE, LRU_BLOCK_SIZE), const4)
    vec = _row_param_spec(layer_idx, LRU_WIDTH)
    return [
        pl.BlockSpec((None, CONV_WIDTH, LRU_WIDTH), const3),
        vec,
        gate_w,
        vec,
        gate_w,
        vec,
        vec,
        pl.BlockSpec((None, LRU_WIDTH, D_MODEL), const3),
    ]


def _lru_weight_args(w_conv, b_conv, w_gate_a, b_gate_a, w_gate_x, b_gate_x, lam, w_out):
    return (w_conv, _row_param(b_conv), w_gate_a, _row_param(b_gate_a), w_gate_x, _row_param(b_gate_x),
            _row_param(lam), w_out)


def _lru_prompt(xb, gate, x, lw, j, *, tm):
    nt = SEQ // tm
    row_spec0 = pl.BlockSpec((tm, LRU_WIDTH), lambda b, i: (b * nt + i, 0))
    vmem = (
        8 * _nbytes((tm, LRU_WIDTH), F32)
        + 3 * _nbytes((tm + V7X_SUBLANES, LRU_WIDTH), F32)
        + 2 * _nbytes((LRU_WIDTH, D_MODEL), BF16)
        + 4 * _nbytes((LRU_BLOCKS, LRU_BLOCK_SIZE, LRU_BLOCK_SIZE), BF16)
        + 10 * _nbytes((tm, LRU_WIDTH), F32)
    )
    return pl.pallas_call(
        _lru_prompt_kernel,
        grid=(BATCH, nt),
        in_specs=[row_spec0, row_spec0, row_spec0] + _lru_weight_specs(j),
        out_specs=[
            row_spec0,
            pl.BlockSpec((None, V7X_SUBLANES, LRU_WIDTH), lambda b, i: (b, 0, 0)),
        ],
        out_shape=[
            jax.ShapeDtypeStruct((BATCH * SEQ, D_MODEL), F32),
            jax.ShapeDtypeStruct((BATCH, V7X_SUBLANES, LRU_WIDTH), F32),
        ],
        scratch_shapes=[
            pltpu.VMEM((tm + V7X_SUBLANES, LRU_WIDTH), F32),
            pltpu.VMEM((tm, LRU_WIDTH), F32),
            pltpu.VMEM((tm, LRU_WIDTH), F32),
            pltpu.VMEM((1, LRU_WIDTH), F32),
        ],
        compiler_params=_params(("arbitrary", "arbitrary"), vmem),
        name="lru_prompt",
    )(xb, gate, x, *lw)


def _lru_sample_kernel(xb_ref, gate_ref, x_ref, cbuf_ref, h0_ref, wc_ref, bc_ref, wa_ref, ba_ref, wx_ref, bx_ref,
                       lam_ref, wo_ref, out_ref, hs_ref):
    rows = xb_ref.shape[0]
    t = lax.broadcasted_iota(jnp.int32, (rows, LRU_WIDTH), 0) % DEC_SEQ
    xb = xb_ref[...]
    cbuf = cbuf_ref[...]
    xc = bc_ref[...] + wc_ref[CONV_WIDTH - 1:CONV_WIDTH, :] * xb
    for k in range(1, CONV_WIDTH):
        shift_c = CONV_WIDTH - 1 - k
        from_buf = cbuf if shift_c == 0 else pltpu.roll(cbuf, rows - shift_c, 0)
        xk = jnp.where(t >= k, pltpu.roll(xb, k, 0), from_buf)
        xc = xc + wc_ref[CONV_WIDTH - 1 - k:CONV_WIDTH - k, :] * xk

    a, u = _lru_gates(xc, wa_ref, ba_ref, wx_ref, bx_ref, lam_ref)
    k = 1
    while k < DEC_SEQ:
        m = t >= k
        u = jnp.where(m, a * pltpu.roll(u, k, 0) + u, u)
        a = jnp.where(m, a * pltpu.roll(a, k, 0), a)
        k *= 2
    hs = u + a * h0_ref[...]
    hs_ref[...] = hs
    y = (hs * gate_ref[...]).astype(BF16)
    out_ref[...] = x_ref[...] + jnp.dot(y, wo_ref[...], preferred_element_type=F32)


def _lru_sample(xb, gate, x, cbuf_rows, h0_rows, lw, j):
    rows = N_SAMPLE_ROWS
    row_spec0 = pl.BlockSpec((rows, LRU_WIDTH), lambda i: (0, 0))
    vmem = 40 * _nbytes((rows, LRU_WIDTH), F32) + 2 * _nbytes((LRU_WIDTH, D_MODEL), BF16)
    return pl.pallas_call(
        _lru_sample_kernel,
        grid=(1,),
        in_specs=[row_spec0] * 5 + _lru_weight_specs(j),
        out_specs=[row_spec0, row_spec0],
        out_shape=[
            jax.ShapeDtypeStruct((rows, D_MODEL), F32),
            jax.ShapeDtypeStruct((rows, LRU_WIDTH), F32),
        ],
        compiler_params=_params(("arbitrary",), vmem),
        name="lru_sample",
    )(xb, gate, x, cbuf_rows, h0_rows, *lw)


_FFN_TM = 1024
_FFN_TF = 512
_FFN_ROW_CHUNK = 512


def kernel(x_prompt, x_sample, cache_swa_k, cache_swa_v, state_lru_h, state_lru_conv, cache_mem_k, cache_mem_v,
           mem_prompt, ln_ffn1, ffn1_w_in, ffn1_w_out, ln_mix, swa_w_qkv, swa_w_o, swa_sinks, lru_w_in,
           lru_w_conv, lru_b_conv, lru_w_gate_a, lru_b_gate_a, lru_w_gate_x, lru_b_gate_x, lru_lambda,
           lru_w_out, ln_cross, ln_mem, cross_w_q, cross_w_kv, cross_w_o, ln_ffn2, ffn2_w_in, ffn2_w_out,
           ln_final):
    bf = lambda w: w.astype(BF16)
    ffn_seq = []
    for layer in range(DEPTH):
        ffn_seq += [(ffn1_w_in, ffn1_w_out, layer), (ffn2_w_in, ffn2_w_out, layer)]
    ffn_w = (bf(ffn1_w_in[0]), bf(ffn1_w_out[0]))

    def ffn(x, x_extra, gain, layer, final_gain=None):
        nonlocal ffn_w
        nxt = ffn_seq[1] if len(ffn_seq) > 1 else None
        out = _ffn(x, x_extra, gain, layer, *ffn_w, tm=_FFN_TM, tf=_FFN_TF, row_chunk=_FFN_ROW_CHUNK,
                   final_gain=final_gain, cast_next=nxt)
        del ffn_seq[0]
        if nxt is not None:
            ffn_w = out[2:]
        return out[0], out[1]

    swa_w_qkv, swa_w_o = bf(swa_w_qkv), bf(swa_w_o)
    lru_w_in, lru_w_out = bf(lru_w_in), bf(lru_w_out)
    lru_w_gate_a, lru_w_gate_x = bf(lru_w_gate_a), bf(lru_w_gate_x)
    cross_w_q, cross_w_kv, cross_w_o = bf(cross_w_q), bf(cross_w_kv), bf(cross_w_o)

    xp = x_prompt.reshape(N_PROMPT_ROWS, D_MODEL)
    xs = x_sample.reshape(N_SAMPLE_ROWS, D_MODEL)

    mem_rows = mem_prompt.reshape(BATCH * N_MEM, D_MODEL)
    mkv = [
        _norm_linear(mem_rows, ln_mem, layer, cross_w_kv, layer, tm=BATCH * N_MEM, tn=MEM_WIDTH, out_dtype=F32)
        for layer in range(DEPTH)
    ]
    mem_k_prompt = jnp.stack([m[:, :MEM_WIDTH] for m in mkv]).reshape(DEPTH, BATCH, N_MEM, MEM_WIDTH)
    mem_v_prompt = jnp.stack([m[:, MEM_WIDTH:] for m in mkv]).reshape(DEPTH, BATCH, N_MEM, MEM_WIDTH)

    cos_p, sin_p = _rope_tables(jnp.arange(SEQ, dtype=jnp.int32))
    cos_s, sin_s = _rope_tables(PAST_LEN + jnp.arange(N_SAMPLE_ROWS, dtype=jnp.int32) % DEC_SEQ)

    swa_k_p, swa_v_p, swa_k_s, swa_v_s = [], [], [], []
    lru_h_p, lru_conv_p, lru_h_s, lru_conv_s = [], [], [], []

    for layer in range(DEPTH):
        j = layer // 2
        xp, xs = ffn(xp, xs, ln_ffn1, layer)

        if layer % 2 == 0:
            qkv_tm = 1024
            q_p, kv_last, kv_pair = _qkv(xp, ln_mix, layer, swa_w_qkv, j, cos_p, sin_p, tm=qkv_tm, q_dtype=BF16,
                                         seq_len=SEQ)
            o_p = _swa_prompt(q_p, kv_pair, swa_sinks[j])
            xp = _linear_residual(o_p, swa_w_o, j, xp, tm=1024)
            kv_tail = kv_last[:, qkv_tm - WINDOW:, :]
            swa_k_p.append(kv_tail[..., :KV_WIDTH].reshape(BATCH, WINDOW, N_KV_HEADS, HEAD_DIM))
            swa_v_p.append(kv_tail[..., KV_WIDTH:].reshape(BATCH, WINDOW, N_KV_HEADS, HEAD_DIM))

            q_s, kv_s = _qkv(xs, ln_mix, layer, swa_w_qkv, j, cos_s, sin_s, tm=N_SAMPLE_ROWS, q_dtype=F32)
            o_s, nk, nv = _swa_sample(
                q_s, kv_s,
                cache_swa_k[j].reshape(DEC_BATCH, WINDOW, KV_WIDTH),
                cache_swa_v[j].reshape(DEC_BATCH, WINDOW, KV_WIDTH),
                swa_sinks[j],
            )
            xs = _linear_residual(o_s, swa_w_o, j, xs, tm=N_SAMPLE_ROWS)
            swa_k_s.append(nk.reshape(DEC_BATCH, WINDOW, N_KV_HEADS, HEAD_DIM))
            swa_v_s.append(nv.reshape(DEC_BATCH, WINDOW, N_KV_HEADS, HEAD_DIM))
        else:
            lw = _lru_weight_args(lru_w_conv, lru_b_conv, lru_w_gate_a, lru_b_gate_a, lru_w_gate_x, lru_b_gate_x,
                                  lru_lambda, lru_w_out)
            xb_p, gate_p = _lru_in(xp, ln_mix, layer, lru_w_in, j, tm=512, tn=LRU_WIDTH)
            xp, h_last = _lru_prompt(xb_p, gate_p, xp, lw, j, tm=512)
            lru_h_p.append(h_last[:, V7X_SUBLANES - 1, :])
            lru_conv_p.append(xb_p.reshape(BATCH, SEQ, LRU_WIDTH)[:, SEQ - (CONV_WIDTH - 1):, :])

            xb_s, gate_s = _lru_in(xs, ln_mix, layer, lru_w_in, j, tm=N_SAMPLE_ROWS, tn=LRU_WIDTH)
            cbuf_rows = jnp.pad(
                state_lru_conv[j], ((0, 0), (0, DEC_SEQ - (CONV_WIDTH - 1)), (0, 0))
            ).reshape(N_SAMPLE_ROWS, LRU_WIDTH)
            h0_rows = jnp.repeat(state_lru_h[j], DEC_SEQ, axis=0)
            xs, hs_s = _lru_sample(xb_s, gate_s, xs, cbuf_rows, h0_rows, lw, j)
            lru_h_s.append(hs_s.reshape(DEC_BATCH, DEC_SEQ, LRU_WIDTH)[:, DEC_SEQ - 1, :])
            xp_tail = jnp.concatenate(
                [state_lru_conv[j], xb_s.reshape(DEC_BATCH, DEC_SEQ, LRU_WIDTH)], axis=1)
            lru_conv_s.append(xp_tail[:, -(CONV_WIDTH - 1):, :])

        xp = _cross_block(xp, ln_cross, cross_w_q, mem_k_prompt, mem_v_prompt, cross_w_o, layer,
                          n_seq=BATCH, seqs=1, rows=1024)
        xs = _cross_block(xs, ln_cross, cross_w_q, cache_mem_k, cache_mem_v, cross_w_o, layer,
                          n_seq=DEC_BATCH, seqs=8, rows=DEC_SEQ)

        fin = ln_final if layer == DEPTH - 1 else None
        xp, xs = ffn(xp, xs, ln_ffn2, layer, final_gain=fin)

    return (
        xp.reshape(BATCH, SEQ, D_MODEL),
        xs.reshape(DEC_BATCH, DEC_SEQ, D_MODEL),
        jnp.stack(swa_k_p), jnp.stack(swa_v_p), jnp.stack(swa_k_s), jnp.stack(swa_v_s),
        jnp.stack(lru_h_p), jnp.stack(lru_conv_p), jnp.stack(lru_h_s), jnp.stack(lru_conv_s),
        mem_k_prompt.reshape(DEPTH, BATCH, N_MEM, N_MEM_HEADS, MEM_HEAD_DIM),
        mem_v_prompt.reshape(DEPTH, BATCH, N_MEM, N_MEM_HEADS, MEM_HEAD_DIM),
    )
```

```python
import functools
import math

import jax
import jax.numpy as jnp
from jax import lax
from jax.experimental import pallas as pl
from jax.experimental.pallas import tpu as pltpu

F32 = jnp.float32
BF16 = jnp.bfloat16

D_MODEL = 2048
BATCH = 2
SEQ = 4096
DEPTH = 2
DEC_BATCH = 32
DEC_SEQ = 8
PAST_LEN = 16384
N_HEADS = 32
N_KV_HEADS = 8
HEAD_DIM = 64
GROUP = N_HEADS // N_KV_HEADS
WINDOW = 128
ROPE_THETA = 10000.0
LRU_WIDTH = D_MODEL
LRU_BLOCKS = 8
LRU_BLOCK_SIZE = LRU_WIDTH // LRU_BLOCKS
CONV_WIDTH = 4
LRU_C = 8.0
D_FF = 5632
N_MEM = 256
N_MEM_HEADS = 4
MEM_HEAD_DIM = 128
MEM_WIDTH = N_MEM_HEADS * MEM_HEAD_DIM
NORM_EPS = 1e-6
Q_WIDTH = N_HEADS * HEAD_DIM
KV_WIDTH = N_KV_HEADS * HEAD_DIM

V7X_LANES = 128
V7X_SUBLANES = 8
V7X_VMEM_BYTES = 64 * 1024 * 1024
V7X_VMEM_USABLE_BYTES = V7X_VMEM_BYTES - 2 * 1024 * 1024

N_PROMPT_ROWS = BATCH * SEQ
N_SAMPLE_ROWS = DEC_BATCH * DEC_SEQ


def _params(semantics, vmem_bytes):
    return pltpu.CompilerParams(
        dimension_semantics=semantics,
        vmem_limit_bytes=int(min(max(vmem_bytes, 32 * 1024 * 1024), V7X_VMEM_USABLE_BYTES)),
    )


def _nbytes(shape, dtype):
    return math.prod(shape) * jnp.dtype(dtype).itemsize


def _rms(x, gain):
    ms = jnp.mean(x * x, axis=-1, keepdims=True)
    return x * lax.rsqrt(ms + NORM_EPS) * gain


def _lane_block(c):
    return slice(c * V7X_LANES, (c + 1) * V7X_LANES)


def _row_param(p):
    return p.reshape(p.shape[0], 1, p.shape[-1])


def _row_param_spec(layer, width):
    return pl.BlockSpec((None, 1, width), lambda *_: (layer, 0, 0))


def _for_chunks(n_rows, chunk, body):
    n_chunks = n_rows // chunk
    if n_chunks == 1:
        body(pl.ds(0, n_rows))
        return

    def step(c, carry):
        body(pl.ds(pl.multiple_of(c * chunk, chunk), chunk))
        return carry

    lax.fori_loop(0, n_chunks, step, 0)


def _ffn_kernel(x_ref, gain_ref, wg_ref, wu_ref, wo_ref, xe_ref, *rest, row_chunk, final_norm, cast_next):
    rest = list(rest)
    gfin_ref = rest.pop(0) if final_norm else None
    next_in_ref, next_out_ref = (rest.pop(0), rest.pop(0)) if cast_next else (None, None)
    o_ref, oe_ref = rest.pop(0), rest.pop(0)
    next_in_bf_ref, next_out_bf_ref = (rest.pop(0), rest.pop(0)) if cast_next else (None, None)
    h_scr, he_scr = rest
    f = pl.program_id(1)

    def run(x_ref, o_ref, h_scr, chunk, side_job=None):
        n_rows = x_ref.shape[0]

        @pl.when(f == 0)
        def _():
            def prologue(rows):
                h_scr[rows, :] = _rms(x_ref[rows, :], gain_ref[...]).astype(BF16)
                o_ref[rows, :] = jnp.zeros((chunk, D_MODEL), F32)

            _for_chunks(n_rows, chunk, prologue)

        chunks = [pl.ds(c * chunk, chunk) for c in range(n_rows // chunk)]
        acts = []
        for rows in chunks:
            h = h_scr[rows, :]
            g = jnp.dot(h, wg_ref[...], preferred_element_type=F32)
            u = jnp.dot(h, wu_ref[...], preferred_element_type=F32)
            acts.append((g * jax.nn.sigmoid(g) * u).astype(BF16))
        for rows, a in zip(chunks, acts):
            o_ref[rows, :] += jnp.dot(a, wo_ref[...], preferred_element_type=F32)
        if side_job is not None:
            side_job()

        @pl.when(f == pl.num_programs(1) - 1)
        def _():
            def epilogue(rows):
                y = x_ref[rows, :] + 0.5 * o_ref[rows, :]
                if final_norm:
                    y = _rms(y, gfin_ref[...])
                o_ref[rows, :] = y

            _for_chunks(n_rows, chunk, epilogue)

    def round_next_weights():
        next_in_bf_ref[...] = next_in_ref[...].astype(BF16)
        next_out_bf_ref[...] = next_out_ref[...].astype(BF16)

    run(x_ref, o_ref, h_scr, row_chunk, side_job=round_next_weights if cast_next else None)

    @pl.when(pl.program_id(0) == 0)
    def _():
        run(xe_ref, oe_ref, he_scr, xe_ref.shape[0])


def _ffn(x, x_extra, gain, layer, w_in, w_out, *, tm, tf, row_chunk, final_gain=None, cast_next=None):
    m = x.shape[0]
    me = x_extra.shape[0]
    n_tiles = m // tm
    nf = D_FF // tf
    final_norm = final_gain is not None
    row_spec = pl.BlockSpec((tm, D_MODEL), lambda i, f: (i, 0))
    extra_spec = pl.BlockSpec((me, D_MODEL), lambda i, f: (0, 0))
    in_specs = [
        row_spec,
        _row_param_spec(layer, D_MODEL),
        pl.BlockSpec((D_MODEL, tf), lambda i, f: (0, f)),
        pl.BlockSpec((D_MODEL, tf), lambda i, f: (0, f + nf)),
        pl.BlockSpec((tf, D_MODEL), lambda i, f: (f, 0)),
        pl.BlockSpec((me, D_MODEL), lambda i, f: (0, 0), pipeline_mode=pl.Buffered(1)),
    ]
    args = [x, _row_param(gain), w_in, w_in, w_out, x_extra]
    out_specs = [row_spec, extra_spec]
    out_shape = [jax.ShapeDtypeStruct((m, D_MODEL), F32), jax.ShapeDtypeStruct((me, D_MODEL), F32)]
    if final_norm:
        in_specs.append(_row_param_spec(0, D_MODEL))
        args.append(final_gain.reshape(1, 1, D_MODEL))
    vmem = (
        4 * _nbytes((tm, D_MODEL), F32)
        + _nbytes((tm, D_MODEL), BF16)
        + 4 * _nbytes((me, D_MODEL), F32)
        + _nbytes((me, D_MODEL), BF16)
        + 2 * 3 * _nbytes((D_MODEL, tf), BF16)
        + 4 * _nbytes((row_chunk, tf), F32)
        + 2 * _nbytes((row_chunk, D_MODEL), F32)
    )
    if cast_next is not None:
        nw_in, nw_out, nl = cast_next
        in_blk = (D_MODEL // n_tiles, 2 * D_FF // nf)
        out_blk = (D_FF // nf, D_MODEL // n_tiles)
        in_specs += [
            pl.BlockSpec((None,) + in_blk, lambda i, f: (nl, i, f)),
            pl.BlockSpec((None,) + out_blk, lambda i, f: (nl, f, i)),
        ]
        args += [nw_in, nw_out]
        out_specs += [pl.BlockSpec(in_blk, lambda i, f: (i, f)), pl.BlockSpec(out_blk, lambda i, f: (f, i))]
        out_shape += [jax.ShapeDtypeStruct(nw_in.shape[1:], BF16), jax.ShapeDtypeStruct(nw_out.shape[1:], BF16)]
        vmem += 2 * (_nbytes(in_blk, F32) + _nbytes(in_blk, BF16) + _nbytes(out_blk, F32) + _nbytes(out_blk, BF16))
    return pl.pallas_call(
        functools.partial(_ffn_kernel, row_chunk=row_chunk, final_norm=final_norm, cast_next=cast_next is not None),
        grid=(n_tiles, nf),
        in_specs=in_specs,
        out_specs=out_specs,
        out_shape=out_shape,
        scratch_shapes=[pltpu.VMEM((tm, D_MODEL), BF16), pltpu.VMEM((me, D_MODEL), BF16)],
        compiler_params=_params(("arbitrary", "arbitrary"), vmem),
        name="ffn",
    )(*args)


def _gelu_tanh(y):
    return 0.5 * y * (1.0 + jnp.tanh(math.sqrt(2.0 / math.pi) * (y + 0.044715 * (y * y * y))))


def _norm_linear_kernel(x_ref, gain_ref, w_ref, o_ref, h_scr):
    @pl.when(pl.program_id(1) == 0)
    def _():
        h_scr[...] = _rms(x_ref[...], gain_ref[...]).astype(BF16)

    o_ref[...] = jnp.dot(h_scr[...], w_ref[...], preferred_element_type=F32).astype(o_ref.dtype)


def _lru_in_kernel(x_ref, gain_ref, wx_ref, wg_ref, xb_ref, gate_ref, h_scr):
    @pl.when(pl.program_id(1) == 0)
    def _():
        h_scr[...] = _rms(x_ref[...], gain_ref[...]).astype(BF16)

    h = h_scr[...]
    gate_ref[...] = _gelu_tanh(jnp.dot(h, wg_ref[...], preferred_element_type=F32))
    xb_ref[...] = jnp.dot(h, wx_ref[...], preferred_element_type=F32)


def _lru_in(x, gain, gain_layer, w, w_layer, *, tm, tn):
    m = x.shape[0]
    n_col = LRU_WIDTH // tn
    vmem = (
        2 * _nbytes((tm, D_MODEL), F32)
        + _nbytes((tm, D_MODEL), BF16)
        + 4 * _nbytes((D_MODEL, tn), BF16)
        + 10 * _nbytes((tm, tn), F32)
    )
    out_spec = pl.BlockSpec((tm, tn), lambda i, j: (i, j))
    return pl.pallas_call(
        _lru_in_kernel,
        grid=(m // tm, n_col),
        in_specs=[
            pl.BlockSpec((tm, D_MODEL), lambda i, j: (i, 0)),
            _row_param_spec(gain_layer, D_MODEL),
            pl.BlockSpec((None, D_MODEL, tn), lambda i, j: (w_layer, 0, j)),
            pl.BlockSpec((None, D_MODEL, tn), lambda i, j: (w_layer, 0, n_col + j)),
        ],
        out_specs=[out_spec, out_spec],
        out_shape=[jax.ShapeDtypeStruct((m, LRU_WIDTH), F32)] * 2,
        scratch_shapes=[pltpu.VMEM((tm, D_MODEL), BF16)],
        compiler_params=_params(("parallel", "arbitrary"), vmem),
        name="lru_in",
    )(x, _row_param(gain), w, w)


def _norm_linear(x, gain, gain_layer, w, w_layer, *, tm, tn, out_dtype):
    m = x.shape[0]
    n = w.shape[-1]
    vmem = (
        2 * _nbytes((tm, D_MODEL), F32)
        + _nbytes((tm, D_MODEL), BF16)
        + 2 * _nbytes((D_MODEL, tn), BF16)
        + 4 * _nbytes((tm, tn), F32)
    )
    return pl.pallas_call(
        _norm_linear_kernel,
        grid=(m // tm, n // tn),
        in_specs=[
            pl.BlockSpec((tm, D_MODEL), lambda i, j: (i, 0)),
            _row_param_spec(gain_layer, D_MODEL),
            pl.BlockSpec((None, D_MODEL, tn), lambda i, j: (w_layer, 0, j)),
        ],
        out_specs=pl.BlockSpec((tm, tn), lambda i, j: (i, j)),
        out_shape=jax.ShapeDtypeStruct((m, n), out_dtype),
        scratch_shapes=[pltpu.VMEM((tm, D_MODEL), BF16)],
        compiler_params=_params(("parallel", "arbitrary"), vmem),
        name="norm_linear",
    )(x, _row_param(gain), w)


_QKV_STEPS = 2
_Q_TN = Q_WIDTH // _QKV_STEPS
_ATTN_SCALE = HEAD_DIM ** -0.5
assert math.frexp(_ATTN_SCALE)[0] == 0.5
assert 2 * KV_WIDTH // _QKV_STEPS == KV_WIDTH


def _rope_tables(positions):
    half = HEAD_DIM // 2
    inv = ROPE_THETA ** (-jnp.arange(half, dtype=F32) * (2.0 / HEAD_DIM))
    ang = positions.astype(F32)[:, None] * inv[None, :]
    cos = jnp.cos(ang)
    sin = jnp.sin(ang)
    reps = V7X_LANES // HEAD_DIM
    cos_t = jnp.tile(jnp.concatenate([cos, cos], axis=-1), (1, reps))
    sin_t = jnp.tile(jnp.concatenate([-sin, sin], axis=-1), (1, reps))
    return jnp.stack([cos_t, jnp.ones_like(cos_t)]), jnp.stack([sin_t, jnp.zeros_like(sin_t)])


def _qkv_kernel(x_ref, gain_ref, wq_ref, wkv_ref, cos_ref, sin_ref, cos_kv_ref, sin_kv_ref, q_ref, kv_ref,
                pair_ref=None):
    tm = x_ref.shape[0]
    lane = lax.broadcasted_iota(jnp.int32, (tm, V7X_LANES), 1)
    first_half = (lane % HEAD_DIM) < (HEAD_DIM // 2)
    low_half = lane < HEAD_DIM

    def rope(yc, cos, sin):
        rot = jnp.where(
            first_half,
            pltpu.roll(yc, V7X_LANES - HEAD_DIM // 2, 1),
            pltpu.roll(yc, HEAD_DIM // 2, 1),
        )
        return yc * cos + rot * sin

    h = _rms(x_ref[...], gain_ref[...]).astype(BF16)
    yq = jnp.dot(h, wq_ref[...], preferred_element_type=F32)
    cos, sin = cos_ref[...], sin_ref[...]
    for c in range(_Q_TN // V7X_LANES):
        q_ref[:, _lane_block(c)] = (rope(yq[:, _lane_block(c)], cos, sin) * _ATTN_SCALE).astype(q_ref.dtype)

    ykv = jnp.dot(h, wkv_ref[...], preferred_element_type=F32)
    cos_kv, sin_kv = cos_kv_ref[...], sin_kv_ref[...]
    heads_per_block = V7X_LANES // HEAD_DIM
    for c in range(KV_WIDTH // V7X_LANES):
        r = rope(ykv[:, _lane_block(c)], cos_kv, sin_kv)
        kv_ref[:, _lane_block(c)] = r
        if pair_ref is not None:
            swapped = pltpu.roll(r, HEAD_DIM, 1)
            pair_ref[:, _lane_block(heads_per_block * c)] = jnp.where(low_half, r, swapped).astype(BF16)
            pair_ref[:, _lane_block(heads_per_block * c + 1)] = jnp.where(low_half, swapped, r).astype(BF16)


def _qkv(x, gain, gain_layer, w, w_layer, cos_t, sin_t, *, tm, q_dtype, seq_len=None):
    m = x.shape[0]
    n_tab = cos_t.shape[1] // tm
    paired_kv = seq_len is not None
    pair_width = N_KV_HEADS * V7X_LANES
    vmem = (
        2 * _nbytes((tm, D_MODEL), F32)
        + _nbytes((tm, D_MODEL), BF16)
        + 2 * _nbytes((D_MODEL, _Q_TN + KV_WIDTH), BF16)
        + 8 * _nbytes((tm, _Q_TN + KV_WIDTH), F32)
        + 4 * _nbytes((tm, pair_width), BF16)
    )
    out_specs = [pl.BlockSpec((tm, _Q_TN), lambda j, i: (i, j))]
    out_shape = [jax.ShapeDtypeStruct((m, Q_WIDTH), q_dtype)]
    if paired_kv:
        tiles_per_seq = seq_len // tm
        out_specs.append(pl.BlockSpec((None, tm, KV_WIDTH), lambda j, i: (i // tiles_per_seq, 0, j)))
        out_shape.append(jax.ShapeDtypeStruct((m // seq_len, tm, 2 * KV_WIDTH), F32))
        out_specs.append(pl.BlockSpec((tm, pair_width), lambda j, i: (i, j)))
        out_shape.append(jax.ShapeDtypeStruct((m, 2 * pair_width), BF16))
    else:
        out_specs.append(pl.BlockSpec((tm, KV_WIDTH), lambda j, i: (i, j)))
        out_shape.append(jax.ShapeDtypeStruct((m, 2 * KV_WIDTH), F32))
    rot_spec = pl.BlockSpec((None, tm, V7X_LANES), lambda j, i: (0, i % n_tab, 0))
    rot_or_id_spec = pl.BlockSpec((None, tm, V7X_LANES), lambda j, i: (j, i % n_tab, 0))
    return pl.pallas_call(
        _qkv_kernel,
        grid=(_QKV_STEPS, m // tm),
        in_specs=[
            pl.BlockSpec((tm, D_MODEL), lambda j, i: (i, 0)),
            _row_param_spec(gain_layer, D_MODEL),
            pl.BlockSpec((None, D_MODEL, _Q_TN), lambda j, i: (w_layer, 0, j)),
            pl.BlockSpec((None, D_MODEL, KV_WIDTH), lambda j, i: (w_layer, 0, Q_WIDTH // KV_WIDTH + j)),
            rot_spec, rot_spec, rot_or_id_spec, rot_or_id_spec,
        ],
        out_specs=out_specs,
        out_shape=out_shape,
        compiler_params=_params(("arbitrary", "arbitrary"), vmem),
        name="qkv_rope",
    )(x, _row_param(gain), w, w, cos_t, sin_t, cos_t, sin_t)


def _linear_residual_kernel(o_ref, w_ref, x_ref, out_ref):
    out_ref[...] = x_ref[...] + jnp.dot(o_ref[...].astype(BF16), w_ref[...], preferred_element_type=F32)


def _linear_residual(o, w, layer, x, *, tm):
    m, k = o.shape
    vmem = (
        2 * _nbytes((tm, k), o.dtype)
        + 2 * _nbytes((k, D_MODEL), BF16)
        + 6 * _nbytes((tm, D_MODEL), F32)
    )
    return pl.pallas_call(
        _linear_residual_kernel,
        grid=(m // tm,),
        in_specs=[
            pl.BlockSpec((tm, k), lambda i: (i, 0)),
            pl.BlockSpec((None, k, D_MODEL), lambda i: (layer, 0, 0)),
            pl.BlockSpec((tm, D_MODEL), lambda i: (i, 0)),
        ],
        out_specs=pl.BlockSpec((tm, D_MODEL), lambda i: (i, 0)),
        out_shape=jax.ShapeDtypeStruct((m, D_MODEL), F32),
        compiler_params=_params(("parallel",), vmem),
        name="linear_residual",
    )(o, w, x)


def _head_cols(h):
    return slice(h * HEAD_DIM, (h + 1) * HEAD_DIM)


def _group_sinks(sinks_ref, kvh, g_of_row):
    sink = jnp.full(g_of_row.shape, sinks_ref[kvh * GROUP], F32)
    for g in range(1, GROUP):
        sink = jnp.where(g_of_row == g, sinks_ref[kvh * GROUP + g], sink)
    return sink


def _sink_softmax_weights(s, sink):
    m = jnp.maximum(jnp.max(s, axis=-1, keepdims=True), sink)
    e = jnp.exp(s - m)
    denom = jnp.sum(e, axis=-1, keepdims=True) + jnp.exp(sink - m)
    return e, 1.0 / denom


def _swa_prompt_kernel(sinks_ref, q_ref, k_prev_ref, k_cur_ref, v_prev_ref, v_cur_ref, o_ref):
    n = pl.program_id(1)
    blk = WINDOW
    rows = GROUP * blk
    qi = lax.broadcasted_iota(jnp.int32, (blk, 2 * blk), 0)
    kj = lax.broadcasted_iota(jnp.int32, (blk, 2 * blk), 1)
    d = qi + blk - kj
    mask = ((d >= 0) & (d < WINDOW) & ((kj >= blk) | (n > 0))) | (kj == 0)
    bias = jnp.where(mask, 0.0, -jnp.inf)[None]
    low_half = lax.broadcasted_iota(jnp.int32, (blk, V7X_LANES), 1) < HEAD_DIM
    g_of_row = lax.broadcasted_iota(jnp.int32, (rows, V7X_LANES), 0) // blk
    col = lax.broadcasted_iota(jnp.int32, (rows, V7X_LANES), 1)
    q_onehot = jnp.where((col == g_of_row) | (col == g_of_row + GROUP), 1.0, 0.0).astype(BF16)
    keep_low = jnp.where(low_half, 1.0, 0.0).astype(BF16)
    keep_high = jnp.where(low_half, 0.0, 1.0).astype(BF16)
    head_rows = 2 * V7X_SUBLANES
    is_slot0 = lax.broadcasted_iota(jnp.int32, (head_rows, V7X_LANES), 0) == 0
    kcol = lax.broadcasted_iota(jnp.int32, (1, V7X_LANES), 1)
    ones = jnp.ones((2 * blk, V7X_LANES), BF16)
    no_feat = jnp.zeros((2 * blk - head_rows, V7X_LANES), BF16)

    def without_slot0(prev_ref, cur_ref, kvh):
        head = jnp.where(is_slot0, 0.0, prev_ref[:head_rows, _lane_block(kvh)].astype(F32)).astype(BF16)
        return jnp.concatenate([head, prev_ref[head_rows:, _lane_block(kvh)], cur_ref[:, _lane_block(kvh)]], axis=0)

    for kvh in range(N_KV_HEADS):
        sink_row = jnp.zeros((1, V7X_LANES), F32)
        for g in range(GROUP):
            sink_row = jnp.where((kcol == g) | (kcol == g + GROUP), sinks_ref[kvh * GROUP + g], sink_row)
        sink_hi = sink_row.astype(BF16).astype(F32)
        sink_feat = jnp.where(kcol < GROUP, sink_hi, sink_row - sink_hi)
        feat = jnp.concatenate([jnp.where(is_slot0, sink_feat, 0.0).astype(BF16), no_feat], axis=0)
        k_ext = jnp.concatenate([without_slot0(k_prev_ref, k_cur_ref, kvh), feat], axis=1)
        v_ones = jnp.concatenate([without_slot0(v_prev_ref, v_cur_ref, kvh), ones], axis=1)
        q_heads = []
        for pair in range(GROUP // 2):
            q_pair = q_ref[:, _lane_block(kvh * (GROUP // 2) + pair)]
            q_heads += [q_pair * keep_low, q_pair * keep_high]
        q_ext = jnp.concatenate([jnp.concatenate(q_heads, axis=0), q_onehot], axis=1)
        s = lax.dot_general(q_ext, k_ext, (((1,), (1,)), ((), ())), preferred_element_type=F32)
        s = (s.reshape(GROUP, blk, 2 * blk) + bias).reshape(rows, 2 * blk)
        e = jnp.exp(s - jnp.max(s, axis=-1, keepdims=True)).astype(BF16)
        o_sum = jnp.dot(e, v_ones, preferred_element_type=F32)
        o = o_sum[:, :V7X_LANES] * (1.0 / o_sum[:, V7X_LANES:])
        for pair in range(GROUP // 2):
            even = o[(2 * pair) * blk:(2 * pair + 1) * blk, :]
            odd = o[(2 * pair + 1) * blk:(2 * pair + 2) * blk, :]
            o_ref[:, _lane_block(kvh * (GROUP // 2) + pair)] = jnp.where(low_half, even, odd).astype(o_ref.dtype)


def _swa_prompt(q, kv_pair, sinks):
    nb = SEQ // WINDOW
    pair_width = N_KV_HEADS * V7X_LANES

    def cur(part):
        return pl.BlockSpec((WINDOW, pair_width), lambda b, n: (b * nb + n, part))

    def prev(part):
        return pl.BlockSpec((WINDOW, pair_width), lambda b, n: (b * nb + jnp.maximum(n - 1, 0), part))

    vmem = (
        4 * _nbytes((WINDOW, Q_WIDTH), BF16)
        + 8 * _nbytes((WINDOW, pair_width), BF16)
        + 24 * _nbytes((GROUP * WINDOW, 2 * WINDOW), F32)
    )
    return pl.pallas_call(
        _swa_prompt_kernel,
        grid=(BATCH, nb),
        in_specs=[
            pl.BlockSpec(memory_space=pltpu.SMEM),
            pl.BlockSpec((WINDOW, Q_WIDTH), lambda b, n: (b * nb + n, 0)),
            prev(0), cur(0), prev(1), cur(1),
        ],
        out_specs=pl.BlockSpec((WINDOW, Q_WIDTH), lambda b, n: (b * nb + n, 0)),
        out_shape=jax.ShapeDtypeStruct((BATCH * SEQ, Q_WIDTH), BF16),
        compiler_params=_params(("parallel", "arbitrary"), vmem),
        name="swa_prompt",
    )(sinks, q, kv_pair, kv_pair, kv_pair, kv_pair)


def _swa_sample_kernel(sinks_ref, q_ref, kv_ref, ck_ref, cv_ref, o_ref, nk_ref, nv_ref, kall_scr, vall_scr):
    nb = ck_ref.shape[0]
    t_new = DEC_SEQ
    s_len = WINDOW + t_new
    rows = GROUP * t_new
    kall_scr[:, :WINDOW, :] = ck_ref[...]
    vall_scr[:, :WINDOW, :] = cv_ref[...]
    kall_scr[:, WINDOW:, :] = kv_ref[:, :KV_WIDTH].reshape(nb, t_new, KV_WIDTH)
    vall_scr[:, WINDOW:, :] = kv_ref[:, KV_WIDTH:].reshape(nb, t_new, KV_WIDTH)
    nk_ref[...] = kall_scr[:, t_new:, :]
    nv_ref[...] = vall_scr[:, t_new:, :]

    row = lax.broadcasted_iota(jnp.int32, (1, rows, s_len), 1)
    kj = lax.broadcasted_iota(jnp.int32, (1, rows, s_len), 2)
    d = row % t_new + WINDOW - kj
    mask = (d >= 0) & (d < WINDOW)
    g_of_row = lax.broadcasted_iota(jnp.int32, (1, rows, 1), 1) // t_new
    q3 = q_ref[...].reshape(nb, t_new, Q_WIDTH)
    for kvh in range(N_KV_HEADS):
        k = kall_scr[:, :, _head_cols(kvh)].astype(BF16)
        v = vall_scr[:, :, _head_cols(kvh)].astype(BF16)
        q = jnp.concatenate([q3[:, :, _head_cols(kvh * GROUP + g)] for g in range(GROUP)], axis=1).astype(BF16)
        s = jnp.einsum("bqd,bkd->bqk", q, k, preferred_element_type=F32)
        s = jnp.where(mask, s, -jnp.inf)
        e, inv = _sink_softmax_weights(s, _group_sinks(sinks_ref, kvh, g_of_row))
        o = jnp.einsum("bqk,bkd->bqd", e.astype(BF16), v, preferred_element_type=F32) * inv
        for g in range(GROUP):
            o_ref[:, _head_cols(kvh * GROUP + g)] = o[:, g * t_new:(g + 1) * t_new, :].reshape(nb * t_new, HEAD_DIM)


_SWA_SAMPLE_SEQS_PER_STEP = 8


def _swa_sample(q, kv, cache_k, cache_v, sinks):
    nb = _SWA_SAMPLE_SEQS_PER_STEP
    cache_spec = pl.BlockSpec((nb, WINDOW, KV_WIDTH), lambda b: (b, 0, 0))
    all_keys = (nb, WINDOW + DEC_SEQ, KV_WIDTH)
    vmem = 8 * _nbytes((nb, WINDOW, KV_WIDTH), F32) + 2 * _nbytes(all_keys, F32) + 16 * 1024 * 1024
    return pl.pallas_call(
        _swa_sample_kernel,
        grid=(DEC_BATCH // nb,),
        in_specs=[
            pl.BlockSpec(memory_space=pltpu.SMEM),
            pl.BlockSpec((nb * DEC_SEQ, Q_WIDTH), lambda b: (b, 0)),
            pl.BlockSpec((nb * DEC_SEQ, 2 * KV_WIDTH), lambda b: (b, 0)),
            cache_spec,
            cache_spec,
        ],
        out_specs=[pl.BlockSpec((nb * DEC_SEQ, Q_WIDTH), lambda b: (b, 0)), cache_spec, cache_spec],
        out_shape=[
            jax.ShapeDtypeStruct((N_SAMPLE_ROWS, Q_WIDTH), F32),
            jax.ShapeDtypeStruct((DEC_BATCH, WINDOW, KV_WIDTH), F32),
            jax.ShapeDtypeStruct((DEC_BATCH, WINDOW, KV_WIDTH), F32),
        ],
        scratch_shapes=[pltpu.VMEM(all_keys, F32), pltpu.VMEM(all_keys, F32)],
        compiler_params=_params(("parallel",), vmem),
        name="swa_sample",
    )(sinks, q, kv, cache_k, cache_v)


def _cross_kernel(x_ref, gain_ref, wq_ref, mk_ref, mv_ref, wo_ref, out_ref, *, seqs, rows):
    scale = MEM_HEAD_DIM ** -0.5
    x = x_ref[...]
    h = _rms(x, gain_ref[...]).astype(BF16)
    q = jnp.dot(h, wq_ref[...], preferred_element_type=F32)
    heads = []
    for hd in range(N_MEM_HEADS):
        cs = slice(hd * MEM_HEAD_DIM, (hd + 1) * MEM_HEAD_DIM)
        qh = q[:, cs].reshape(seqs, rows, MEM_HEAD_DIM).astype(BF16)
        if mk_ref.ndim == 4:
            k = mk_ref[:, :, hd, :].astype(BF16)
            v = mv_ref[:, :, hd, :].astype(BF16)
        else:
            k = mk_ref[:, :, cs].astype(BF16)
            v = mv_ref[:, :, cs].astype(BF16)
        s = jnp.einsum("bqd,bkd->bqk", qh, k, preferred_element_type=F32) * scale
        e = jnp.exp(s - jnp.max(s, axis=-1, keepdims=True))
        inv = 1.0 / jnp.sum(e, axis=-1, keepdims=True)
        o = jnp.einsum("bqk,bkd->bqd", e.astype(BF16), v, preferred_element_type=F32) * inv
        heads.append(o.reshape(seqs * rows, MEM_HEAD_DIM))
    o_all = jnp.concatenate(heads, axis=-1).astype(BF16)
    out_ref[...] = x + jnp.dot(o_all, wo_ref[...], preferred_element_type=F32)


def _cross_block(x, gain, wq, mem_k, mem_v, wo, layer, *, n_seq, seqs, rows):
    seq_len = x.shape[0] // n_seq
    nt = seq_len // rows if seqs == 1 else 1
    tile = seqs * rows
    x_spec = pl.BlockSpec((tile, D_MODEL), lambda g, i: (g * nt + i, 0))
    mem_tail = mem_k.shape[3:]
    mem_spec = pl.BlockSpec((None, seqs, N_MEM) + mem_tail, lambda g, i: (layer, g, 0) + (0,) * len(mem_tail))
    mem_pad = V7X_SUBLANES // N_MEM_HEADS if len(mem_tail) == 2 else 1
    vmem = (
        4 * _nbytes((tile, D_MODEL), F32)
        + 4 * mem_pad * _nbytes((seqs, N_MEM, MEM_WIDTH), F32)
        + 4 * _nbytes((D_MODEL, MEM_WIDTH), BF16)
        + 4 * _nbytes((tile, D_MODEL), F32)
        + 8 * _nbytes((tile, N_MEM), F32)
    )
    return pl.pallas_call(
        functools.partial(_cross_kernel, seqs=seqs, rows=rows),
        grid=(n_seq // seqs, nt),
        in_specs=[
            x_spec,
            _row_param_spec(layer, D_MODEL),
            pl.BlockSpec((None, D_MODEL, MEM_WIDTH), lambda g, i: (layer, 0, 0)),
            mem_spec,
            mem_spec,
            pl.BlockSpec((None, MEM_WIDTH, D_MODEL), lambda g, i: (layer, 0, 0)),
        ],
        out_specs=x_spec,
        out_shape=jax.ShapeDtypeStruct(x.shape, F32),
        compiler_params=_params(("parallel", "arbitrary"), vmem),
        name="cross_block",
    )(x, _row_param(gain), wq, mem_k, mem_v, wo)


def _log_sigmoid(x):
    return jnp.minimum(x, 0.0) - jnp.log1p(jnp.exp(-jnp.abs(x)))


def _one_minus_exp2(y, exp_y):
    return jnp.tanh(-y) * (1.0 + exp_y * exp_y)


def _lru_block_cols(n):
    return slice(n * LRU_BLOCK_SIZE, (n + 1) * LRU_BLOCK_SIZE)


def _lru_gate_block(xc, n, wa_ref, ba_ref, wx_ref, bx_ref, lam_ref):
    cs = _lru_block_cols(n)
    xb = xc.astype(BF16)
    r = jax.nn.sigmoid(jnp.dot(xb, wa_ref[n], preferred_element_type=F32) + ba_ref[:, cs])
    ig = jax.nn.sigmoid(jnp.dot(xb, wx_ref[n], preferred_element_type=F32) + bx_ref[:, cs])
    log_a = r * (LRU_C * _log_sigmoid(lam_ref[:, cs]))
    a = jnp.exp(log_a)
    z = _one_minus_exp2(log_a, a)
    u = jnp.where(z > 0.0, z * lax.rsqrt(z), 0.0) * (ig * xc)
    return a, u


def _lru_gates(xc, wa_ref, ba_ref, wx_ref, bx_ref, lam_ref):
    parts = [_lru_gate_block(xc[:, _lru_block_cols(n)], n, wa_ref, ba_ref, wx_ref, bx_ref, lam_ref)
             for n in range(LRU_BLOCKS)]
    return jnp.concatenate([p[0] for p in parts], axis=-1), jnp.concatenate([p[1] for p in parts], axis=-1)


def _lru_prompt_kernel(xb_ref, gate_ref, x_prev_ref, wc_ref, bc_ref, wa_ref, ba_ref, wx_ref, bx_ref, lam_ref,
                       wo_ref, out_ref, hlast_ref, xext_scr, a_scr, u_scr, h_scr, y_scr, *, tiles_per_seq, n_tiles):
    s = pl.program_id(0)
    tm = xb_ref.shape[0]
    pad = V7X_SUBLANES

    @pl.when(s % tiles_per_seq == 0)
    def _():
        xext_scr[0:pad, :] = jnp.zeros((pad, LRU_WIDTH), F32)
        h_scr[...] = jnp.zeros((1, LRU_WIDTH), F32)

    @pl.when(s == 0)
    def _():
        y_scr[...] = jnp.zeros(y_scr.shape, BF16)

    xext_scr[pad:pad + tm, :] = xb_ref[...]
    for n in range(LRU_BLOCKS):
        cs = _lru_block_cols(n)
        out_ref[:, cs] = x_prev_ref[:, cs] + jnp.dot(y_scr[...], wo_ref[:, cs], preferred_element_type=F32)
        xc = bc_ref[:, cs] + wc_ref[CONV_WIDTH - 1:CONV_WIDTH, cs] * xext_scr[pad:pad + tm, cs]
        for k in range(1, CONV_WIDTH):
            xc = xc + wc_ref[CONV_WIDTH - 1 - k:CONV_WIDTH - k, cs] * xext_scr[pad - k:pad - k + tm, cs]
        a_scr[:, cs], u_scr[:, cs] = _lru_gate_block(xc, n, wa_ref, ba_ref, wx_ref, bx_ref, lam_ref)
    xext_scr[0:pad, :] = xext_scr[tm:tm + pad, :]

    def step(t, h):
        h = a_scr[pl.ds(t, 1), :] * h + u_scr[pl.ds(t, 1), :]
        u_scr[pl.ds(t, 1), :] = h
        return h

    h_scr[...] = lax.fori_loop(0, tm, step, h_scr[...], unroll=8)
    y_scr[...] = (u_scr[...] * gate_ref[...]).astype(BF16)

    @pl.when(s < n_tiles)
    def _():
        hlast_ref[...] = u_scr[tm - pad:tm, :]


def _lru_weight_specs(layer_idx):
    def const3(*_):
        return (layer_idx, 0, 0)

    def const4(*_):
        return (layer_idx, 0, 0, 0)

    gate_w = pl.BlockSpec((None, LRU_BLOCKS, LRU_BLOCK_SIZE, LRU_BLOCK_SIZE), const4)
    vec = _row_param_spec(layer_idx, LRU_WIDTH)
    return [
        pl.BlockSpec((None, CONV_WIDTH, LRU_WIDTH), const3),
        vec,
        gate_w,
        vec,
        gate_w,
        vec,
        vec,
        pl.BlockSpec((None, LRU_WIDTH, D_MODEL), const3),
    ]


def _lru_weight_args(w_conv, b_conv, w_gate_a, b_gate_a, w_gate_x, b_gate_x, lam, w_out):
    return (w_conv, _row_param(b_conv), w_gate_a, _row_param(b_gate_a), w_gate_x, _row_param(b_gate_x),
            _row_param(lam), w_out)


def _lru_prompt(xb, gate, x, lw, j, *, tm):
    tiles_per_seq = SEQ // tm
    n_tiles = BATCH * tiles_per_seq
    cur_spec = pl.BlockSpec((tm, LRU_WIDTH), lambda s: (jnp.minimum(s, n_tiles - 1), 0))
    prev_spec = pl.BlockSpec((tm, LRU_WIDTH), lambda s: (jnp.maximum(s - 1, 0), 0))
    vmem = (
        8 * _nbytes((tm, LRU_WIDTH), F32)
        + 3 * _nbytes((tm + V7X_SUBLANES, LRU_WIDTH), F32)
        + 2 * _nbytes((LRU_WIDTH, D_MODEL), BF16)
        + 4 * _nbytes((LRU_BLOCKS, LRU_BLOCK_SIZE, LRU_BLOCK_SIZE), BF16)
        + 10 * _nbytes((tm, LRU_WIDTH), F32)
    )
    return pl.pallas_call(
        functools.partial(_lru_prompt_kernel, tiles_per_seq=tiles_per_seq, n_tiles=n_tiles),
        grid=(n_tiles + 1,),
        in_specs=[cur_spec, cur_spec, prev_spec] + _lru_weight_specs(j),
        out_specs=[
            prev_spec,
            pl.BlockSpec((None, V7X_SUBLANES, LRU_WIDTH),
                         lambda s: (jnp.minimum(s, n_tiles - 1) // tiles_per_seq, 0, 0)),
        ],
        out_shape=[
            jax.ShapeDtypeStruct((BATCH * SEQ, D_MODEL), F32),
            jax.ShapeDtypeStruct((BATCH, V7X_SUBLANES, LRU_WIDTH), F32),
        ],
        scratch_shapes=[
            pltpu.VMEM((tm + V7X_SUBLANES, LRU_WIDTH), F32),
            pltpu.VMEM((tm, LRU_WIDTH), F32),
            pltpu.VMEM((tm, LRU_WIDTH), F32),
            pltpu.VMEM((1, LRU_WIDTH), F32),
            pltpu.VMEM((tm, LRU_WIDTH), BF16),
        ],
        compiler_params=_params(("arbitrary",), vmem),
        name="lru_prompt",
    )(xb, gate, x, *lw)


def _lru_sample_kernel(xb_ref, gate_ref, x_ref, cbuf_ref, h0_ref, wc_ref, bc_ref, wa_ref, ba_ref, wx_ref, bx_ref,
                       lam_ref, wo_ref, out_ref, hs_ref):
    rows = xb_ref.shape[0]
    t = lax.broadcasted_iota(jnp.int32, (rows, LRU_WIDTH), 0) % DEC_SEQ
    xb = xb_ref[...]
    cbuf = cbuf_ref[...]
    xc = bc_ref[...] + wc_ref[CONV_WIDTH - 1:CONV_WIDTH, :] * xb
    for k in range(1, CONV_WIDTH):
        shift_c = CONV_WIDTH - 1 - k
        from_buf = cbuf if shift_c == 0 else pltpu.roll(cbuf, rows - shift_c, 0)
        xk = jnp.where(t >= k, pltpu.roll(xb, k, 0), from_buf)
        xc = xc + wc_ref[CONV_WIDTH - 1 - k:CONV_WIDTH - k, :] * xk

    a, u = _lru_gates(xc, wa_ref, ba_ref, wx_ref, bx_ref, lam_ref)
    k = 1
    while k < DEC_SEQ:
        m = t >= k
        u = jnp.where(m, a * pltpu.roll(u, k, 0) + u, u)
        a = jnp.where(m, a * pltpu.roll(a, k, 0), a)
        k *= 2
    hs = u + a * h0_ref[...]
    hs_ref[...] = hs
    y = (hs * gate_ref[...]).astype(BF16)
    out_ref[...] = x_ref[...] + jnp.dot(y, wo_ref[...], preferred_element_type=F32)


def _lru_sample(xb, gate, x, cbuf_rows, h0_rows, lw, j):
    rows = N_SAMPLE_ROWS
    row_spec0 = pl.BlockSpec((rows, LRU_WIDTH), lambda i: (0, 0))
    vmem = 40 * _nbytes((rows, LRU_WIDTH), F32) + 2 * _nbytes((LRU_WIDTH, D_MODEL), BF16)
    return pl.pallas_call(
        _lru_sample_kernel,
        grid=(1,),
        in_specs=[row_spec0] * 5 + _lru_weight_specs(j),
        out_specs=[row_spec0, row_spec0],
        out_shape=[
            jax.ShapeDtypeStruct((rows, D_MODEL), F32),
            jax.ShapeDtypeStruct((rows, LRU_WIDTH), F32),
        ],
        compiler_params=_params(("arbitrary",), vmem),
        name="lru_sample",
    )(xb, gate, x, cbuf_rows, h0_rows, *lw)


_FFN_TM = 1024
_FFN_TF = 512
_FFN_ROW_CHUNK = 512


def kernel(x_prompt, x_sample, cache_swa_k, cache_swa_v, state_lru_h, state_lru_conv, cache_mem_k, cache_mem_v,
           mem_prompt, ln_ffn1, ffn1_w_in, ffn1_w_out, ln_mix, swa_w_qkv, swa_w_o, swa_sinks, lru_w_in,
           lru_w_conv, lru_b_conv, lru_w_gate_a, lru_b_gate_a, lru_w_gate_x, lru_b_gate_x, lru_lambda,
           lru_w_out, ln_cross, ln_mem, cross_w_q, cross_w_kv, cross_w_o, ln_ffn2, ffn2_w_in, ffn2_w_out,
           ln_final):
    bf = lambda w: w.astype(BF16)
    ffn_seq = []
    for layer in range(DEPTH):
        ffn_seq += [(ffn1_w_in, ffn1_w_out, layer), (ffn2_w_in, ffn2_w_out, layer)]
    ffn_w = (bf(ffn1_w_in[0]), bf(ffn1_w_out[0]))

    def ffn(x, x_extra, gain, layer, final_gain=None):
        nonlocal ffn_w
        nxt = ffn_seq[1] if len(ffn_seq) > 1 else None
        out = _ffn(x, x_extra, gain, layer, *ffn_w, tm=_FFN_TM, tf=_FFN_TF, row_chunk=_FFN_ROW_CHUNK,
                   final_gain=final_gain, cast_next=nxt)
        del ffn_seq[0]
        if nxt is not None:
            ffn_w = out[2:]
        return out[0], out[1]

    swa_w_qkv, swa_w_o = bf(swa_w_qkv), bf(swa_w_o)
    lru_w_in, lru_w_out = bf(lru_w_in), bf(lru_w_out)
    lru_w_gate_a, lru_w_gate_x = bf(lru_w_gate_a), bf(lru_w_gate_x)
    cross_w_q, cross_w_kv, cross_w_o = bf(cross_w_q), bf(cross_w_kv), bf(cross_w_o)

    xp = x_prompt.reshape(N_PROMPT_ROWS, D_MODEL)
    xs = x_sample.reshape(N_SAMPLE_ROWS, D_MODEL)

    mem_rows = mem_prompt.reshape(BATCH * N_MEM, D_MODEL)
    mkv = [
        _norm_linear(mem_rows, ln_mem, layer, cross_w_kv, layer, tm=BATCH * N_MEM, tn=MEM_WIDTH, out_dtype=F32)
        for layer in range(DEPTH)
    ]
    mem_k_prompt = jnp.stack([m[:, :MEM_WIDTH] for m in mkv]).reshape(DEPTH, BATCH, N_MEM, MEM_WIDTH)
    mem_v_prompt = jnp.stack([m[:, MEM_WIDTH:] for m in mkv]).reshape(DEPTH, BATCH, N_MEM, MEM_WIDTH)

    cos_p, sin_p = _rope_tables(jnp.arange(SEQ, dtype=jnp.int32))
    cos_s, sin_s = _rope_tables(PAST_LEN + jnp.arange(N_SAMPLE_ROWS, dtype=jnp.int32) % DEC_SEQ)

    swa_k_p, swa_v_p, swa_k_s, swa_v_s = [], [], [], []
    lru_h_p, lru_conv_p, lru_h_s, lru_conv_s = [], [], [], []

    for layer in range(DEPTH):
        j = layer // 2
        xp, xs = ffn(xp, xs, ln_ffn1, layer)

        if layer % 2 == 0:
            qkv_tm = 1024
            q_p, kv_last, kv_pair = _qkv(xp, ln_mix, layer, swa_w_qkv, j, cos_p, sin_p, tm=qkv_tm, q_dtype=BF16,
                                         seq_len=SEQ)
            o_p = _swa_prompt(q_p, kv_pair, swa_sinks[j])
            xp = _linear_residual(o_p, swa_w_o, j, xp, tm=512)
            kv_tail = kv_last[:, qkv_tm - WINDOW:, :]
            swa_k_p.append(kv_tail[..., :KV_WIDTH].reshape(BATCH, WINDOW, N_KV_HEADS, HEAD_DIM))
            swa_v_p.append(kv_tail[..., KV_WIDTH:].reshape(BATCH, WINDOW, N_KV_HEADS, HEAD_DIM))

            q_s, kv_s = _qkv(xs, ln_mix, layer, swa_w_qkv, j, cos_s, sin_s, tm=N_SAMPLE_ROWS, q_dtype=F32)
            o_s, nk, nv = _swa_sample(
                q_s, kv_s,
                cache_swa_k[j].reshape(DEC_BATCH, WINDOW, KV_WIDTH),
                cache_swa_v[j].reshape(DEC_BATCH, WINDOW, KV_WIDTH),
                swa_sinks[j],
            )
            xs = _linear_residual(o_s, swa_w_o, j, xs, tm=N_SAMPLE_ROWS)
            swa_k_s.append(nk.reshape(DEC_BATCH, WINDOW, N_KV_HEADS, HEAD_DIM))
            swa_v_s.append(nv.reshape(DEC_BATCH, WINDOW, N_KV_HEADS, HEAD_DIM))
        else:
            lw = _lru_weight_args(lru_w_conv, lru_b_conv, lru_w_gate_a, lru_b_gate_a, lru_w_gate_x, lru_b_gate_x,
                                  lru_lambda, lru_w_out)
            xb_p, gate_p = _lru_in(xp, ln_mix, layer, lru_w_in, j, tm=512, tn=LRU_WIDTH)
            xp, h_last = _lru_prompt(xb_p, gate_p, xp, lw, j, tm=256)
            lru_h_p.append(h_last[:, V7X_SUBLANES - 1, :])
            lru_conv_p.append(xb_p.reshape(BATCH, SEQ, LRU_WIDTH)[:, SEQ - (CONV_WIDTH - 1):, :])

            xb_s, gate_s = _lru_in(xs, ln_mix, layer, lru_w_in, j, tm=N_SAMPLE_ROWS, tn=LRU_WIDTH)
            cbuf_rows = jnp.pad(
                state_lru_conv[j], ((0, 0), (0, DEC_SEQ - (CONV_WIDTH - 1)), (0, 0))
            ).reshape(N_SAMPLE_ROWS, LRU_WIDTH)
            h0_rows = jnp.repeat(state_lru_h[j], DEC_SEQ, axis=0)
            xs, hs_s = _lru_sample(xb_s, gate_s, xs, cbuf_rows, h0_rows, lw, j)
            lru_h_s.append(hs_s.reshape(DEC_BATCH, DEC_SEQ, LRU_WIDTH)[:, DEC_SEQ - 1, :])
            xp_tail = jnp.concatenate(
                [state_lru_conv[j], xb_s.reshape(DEC_BATCH, DEC_SEQ, LRU_WIDTH)], axis=1)
            lru_conv_s.append(xp_tail[:, -(CONV_WIDTH - 1):, :])

        xp = _cross_block(xp, ln_cross, cross_w_q, mem_k_prompt, mem_v_prompt, cross_w_o, layer,
                          n_seq=BATCH, seqs=1, rows=1024)
        xs = _cross_block(xs, ln_cross, cross_w_q, cache_mem_k, cache_mem_v, cross_w_o, layer,
                          n_seq=DEC_BATCH, seqs=8, rows=DEC_SEQ)

        fin = ln_final if layer == DEPTH - 1 else None
        xp, xs = ffn(xp, xs, ln_ffn2, layer, final_gain=fin)

    return (
        xp.reshape(BATCH, SEQ, D_MODEL),
        xs.reshape(DEC_BATCH, DEC_SEQ, D_MODEL),
        jnp.stack(swa_k_p), jnp.stack(swa_v_p), jnp.stack(swa_k_s), jnp.stack(swa_v_s),
        jnp.stack(lru_h_p), jnp.stack(lru_conv_p), jnp.stack(lru_h_s), jnp.stack(lru_conv_s),
        mem_k_prompt.reshape(DEPTH, BATCH, N_MEM, N_MEM_HEADS, MEM_HEAD_DIM),
        mem_v_prompt.reshape(DEPTH, BATCH, N_MEM, N_MEM_HEADS, MEM_HEAD_DIM),
    )
```

```python
import functools
import math

import jax
import jax.numpy as jnp
from jax import lax
from jax.experimental import pallas as pl
from jax.experimental.pallas import tpu as pltpu

F32 = jnp.float32
BF16 = jnp.bfloat16

D_MODEL = 2048
BATCH = 2
SEQ = 4096
DEPTH = 2
DEC_BATCH = 32
DEC_SEQ = 8
PAST_LEN = 16384
N_HEADS = 32
N_KV_HEADS = 8
HEAD_DIM = 64
GROUP = N_HEADS // N_KV_HEADS
WINDOW = 128
ROPE_THETA = 10000.0
LRU_WIDTH = D_MODEL
LRU_BLOCKS = 8
LRU_BLOCK_SIZE = LRU_WIDTH // LRU_BLOCKS
CONV_WIDTH = 4
LRU_C = 8.0
D_FF = 5632
N_MEM = 256
N_MEM_HEADS = 4
MEM_HEAD_DIM = 128
MEM_WIDTH = N_MEM_HEADS * MEM_HEAD_DIM
NORM_EPS = 1e-6
Q_WIDTH = N_HEADS * HEAD_DIM
KV_WIDTH = N_KV_HEADS * HEAD_DIM

V7X_LANES = 128
V7X_SUBLANES = 8
V7X_VMEM_BYTES = 64 * 1024 * 1024
V7X_VMEM_USABLE_BYTES = V7X_VMEM_BYTES - 2 * 1024 * 1024

N_PROMPT_ROWS = BATCH * SEQ
N_SAMPLE_ROWS = DEC_BATCH * DEC_SEQ


def _params(semantics, vmem_bytes):
    return pltpu.CompilerParams(
        dimension_semantics=semantics,
        vmem_limit_bytes=int(min(max(vmem_bytes, 32 * 1024 * 1024), V7X_VMEM_USABLE_BYTES)),
    )


def _nbytes(shape, dtype):
    return math.prod(shape) * jnp.dtype(dtype).itemsize


def _rms(x, gain):
    ms = jnp.mean(x * x, axis=-1, keepdims=True)
    return x * lax.rsqrt(ms + NORM_EPS) * gain


def _lane_block(c):
    return slice(c * V7X_LANES, (c + 1) * V7X_LANES)


def _row_param(p):
    return p.reshape(p.shape[0], 1, p.shape[-1])


def _row_param_spec(layer, width):
    return pl.BlockSpec((None, 1, width), lambda *_: (layer, 0, 0))


def _for_chunks(n_rows, chunk, body):
    n_chunks = n_rows // chunk
    if n_chunks == 1:
        body(pl.ds(0, n_rows))
        return

    def step(c, carry):
        body(pl.ds(pl.multiple_of(c * chunk, chunk), chunk))
        return carry

    lax.fori_loop(0, n_chunks, step, 0)


def _ffn_kernel(x_ref, gain_ref, wg_ref, wu_ref, wo_ref, xe_ref, *rest, row_chunk, final_norm, cast_next):
    rest = list(rest)
    gfin_ref = rest.pop(0) if final_norm else None
    next_in_ref, next_out_ref = (rest.pop(0), rest.pop(0)) if cast_next else (None, None)
    o_ref, oe_ref = rest.pop(0), rest.pop(0)
    next_in_bf_ref, next_out_bf_ref = (rest.pop(0), rest.pop(0)) if cast_next else (None, None)
    h_scr, he_scr = rest
    f = pl.program_id(1)

    def run(x_ref, o_ref, h_scr, chunk, side_job=None):
        n_rows = x_ref.shape[0]

        @pl.when(f == 0)
        def _():
            def prologue(rows):
                h_scr[rows, :] = _rms(x_ref[rows, :], gain_ref[...]).astype(BF16)
                o_ref[rows, :] = jnp.zeros((chunk, D_MODEL), F32)

            _for_chunks(n_rows, chunk, prologue)

        chunks = [pl.ds(c * chunk, chunk) for c in range(n_rows // chunk)]
        acts = []
        for rows in chunks:
            h = h_scr[rows, :]
            g = jnp.dot(h, wg_ref[...], preferred_element_type=F32)
            u = jnp.dot(h, wu_ref[...], preferred_element_type=F32)
            acts.append((g * jax.nn.sigmoid(g) * u).astype(BF16))
        for rows, a in zip(chunks, acts):
            o_ref[rows, :] += jnp.dot(a, wo_ref[...], preferred_element_type=F32)
        if side_job is not None:
            side_job()

        @pl.when(f == pl.num_programs(1) - 1)
        def _():
            def epilogue(rows):
                y = x_ref[rows, :] + 0.5 * o_ref[rows, :]
                if final_norm:
                    y = _rms(y, gfin_ref[...])
                o_ref[rows, :] = y

            _for_chunks(n_rows, chunk, epilogue)

    def round_next_weights():
        next_in_bf_ref[...] = next_in_ref[...].astype(BF16)
        next_out_bf_ref[...] = next_out_ref[...].astype(BF16)

    run(x_ref, o_ref, h_scr, row_chunk, side_job=round_next_weights if cast_next else None)

    @pl.when(pl.program_id(0) == 0)
    def _():
        run(xe_ref, oe_ref, he_scr, xe_ref.shape[0])


def _ffn(x, x_extra, gain, layer, w_in, w_out, *, tm, tf, row_chunk, final_gain=None, cast_next=None):
    m = x.shape[0]
    me = x_extra.shape[0]
    n_tiles = m // tm
    nf = D_FF // tf
    final_norm = final_gain is not None
    row_spec = pl.BlockSpec((tm, D_MODEL), lambda i, f: (i, 0))
    extra_spec = pl.BlockSpec((me, D_MODEL), lambda i, f: (0, 0))
    in_specs = [
        row_spec,
        _row_param_spec(layer, D_MODEL),
        pl.BlockSpec((D_MODEL, tf), lambda i, f: (0, f)),
        pl.BlockSpec((D_MODEL, tf), lambda i, f: (0, f + nf)),
        pl.BlockSpec((tf, D_MODEL), lambda i, f: (f, 0)),
        pl.BlockSpec((me, D_MODEL), lambda i, f: (0, 0), pipeline_mode=pl.Buffered(1)),
    ]
    args = [x, _row_param(gain), w_in, w_in, w_out, x_extra]
    out_specs = [row_spec, extra_spec]
    out_shape = [jax.ShapeDtypeStruct((m, D_MODEL), F32), jax.ShapeDtypeStruct((me, D_MODEL), F32)]
    if final_norm:
        in_specs.append(_row_param_spec(0, D_MODEL))
        args.append(final_gain.reshape(1, 1, D_MODEL))
    vmem = (
        4 * _nbytes((tm, D_MODEL), F32)
        + _nbytes((tm, D_MODEL), BF16)
        + 4 * _nbytes((me, D_MODEL), F32)
        + _nbytes((me, D_MODEL), BF16)
        + 2 * 3 * _nbytes((D_MODEL, tf), BF16)
        + 4 * _nbytes((row_chunk, tf), F32)
        + 2 * _nbytes((row_chunk, D_MODEL), F32)
    )
    if cast_next is not None:
        nw_in, nw_out, nl = cast_next
        in_blk = (D_MODEL // n_tiles, 2 * D_FF // nf)
        out_blk = (D_FF // nf, D_MODEL // n_tiles)
        in_specs += [
            pl.BlockSpec((None,) + in_blk, lambda i, f: (nl, i, f)),
            pl.BlockSpec((None,) + out_blk, lambda i, f: (nl, f, i)),
        ]
        args += [nw_in, nw_out]
        out_specs += [pl.BlockSpec(in_blk, lambda i, f: (i, f)), pl.BlockSpec(out_blk, lambda i, f: (f, i))]
        out_shape += [jax.ShapeDtypeStruct(nw_in.shape[1:], BF16), jax.ShapeDtypeStruct(nw_out.shape[1:], BF16)]
        vmem += 2 * (_nbytes(in_blk, F32) + _nbytes(in_blk, BF16) + _nbytes(out_blk, F32) + _nbytes(out_blk, BF16))
    return pl.pallas_call(
        functools.partial(_ffn_kernel, row_chunk=row_chunk, final_norm=final_norm, cast_next=cast_next is not None),
        grid=(n_tiles, nf),
        in_specs=in_specs,
        out_specs=out_specs,
        out_shape=out_shape,
        scratch_shapes=[pltpu.VMEM((tm, D_MODEL), BF16), pltpu.VMEM((me, D_MODEL), BF16)],
        compiler_params=_params(("arbitrary", "arbitrary"), vmem),
        name="ffn",
    )(*args)


def _gelu_tanh(y):
    return 0.5 * y * (1.0 + jnp.tanh(math.sqrt(2.0 / math.pi) * (y + 0.044715 * (y * y * y))))


def _norm_linear_kernel(x_ref, gain_ref, w_ref, o_ref, h_scr):
    @pl.when(pl.program_id(1) == 0)
    def _():
        h_scr[...] = _rms(x_ref[...], gain_ref[...]).astype(BF16)

    o_ref[...] = jnp.dot(h_scr[...], w_ref[...], preferred_element_type=F32).astype(o_ref.dtype)


def _lru_in_kernel(x_ref, gain_ref, wx_ref, wg_ref, xb_ref, gate_ref, h_scr):
    @pl.when(pl.program_id(1) == 0)
    def _():
        h_scr[...] = _rms(x_ref[...], gain_ref[...]).astype(BF16)

    h = h_scr[...]
    gate_ref[...] = _gelu_tanh(jnp.dot(h, wg_ref[...], preferred_element_type=F32))
    xb_ref[...] = jnp.dot(h, wx_ref[...], preferred_element_type=F32)


def _lru_in(x, gain, gain_layer, w, w_layer, *, tm, tn):
    m = x.shape[0]
    n_col = LRU_WIDTH // tn
    vmem = (
        2 * _nbytes((tm, D_MODEL), F32)
        + _nbytes((tm, D_MODEL), BF16)
        + 4 * _nbytes((D_MODEL, tn), BF16)
        + 10 * _nbytes((tm, tn), F32)
    )
    out_spec = pl.BlockSpec((tm, tn), lambda i, j: (i, j))
    return pl.pallas_call(
        _lru_in_kernel,
        grid=(m // tm, n_col),
        in_specs=[
            pl.BlockSpec((tm, D_MODEL), lambda i, j: (i, 0)),
            _row_param_spec(gain_layer, D_MODEL),
            pl.BlockSpec((None, D_MODEL, tn), lambda i, j: (w_layer, 0, j)),
            pl.BlockSpec((None, D_MODEL, tn), lambda i, j: (w_layer, 0, n_col + j)),
        ],
        out_specs=[out_spec, out_spec],
        out_shape=[jax.ShapeDtypeStruct((m, LRU_WIDTH), F32)] * 2,
        scratch_shapes=[pltpu.VMEM((tm, D_MODEL), BF16)],
        compiler_params=_params(("parallel", "arbitrary"), vmem),
        name="lru_in",
    )(x, _row_param(gain), w, w)


def _norm_linear(x, gain, gain_layer, w, w_layer, *, tm, tn, out_dtype):
    m = x.shape[0]
    n = w.shape[-1]
    vmem = (
        2 * _nbytes((tm, D_MODEL), F32)
        + _nbytes((tm, D_MODEL), BF16)
        + 2 * _nbytes((D_MODEL, tn), BF16)
        + 4 * _nbytes((tm, tn), F32)
    )
    return pl.pallas_call(
        _norm_linear_kernel,
        grid=(m // tm, n // tn),
        in_specs=[
            pl.BlockSpec((tm, D_MODEL), lambda i, j: (i, 0)),
            _row_param_spec(gain_layer, D_MODEL),
            pl.BlockSpec((None, D_MODEL, tn), lambda i, j: (w_layer, 0, j)),
        ],
        out_specs=pl.BlockSpec((tm, tn), lambda i, j: (i, j)),
        out_shape=jax.ShapeDtypeStruct((m, n), out_dtype),
        scratch_shapes=[pltpu.VMEM((tm, D_MODEL), BF16)],
        compiler_params=_params(("parallel", "arbitrary"), vmem),
        name="norm_linear",
    )(x, _row_param(gain), w)


_QKV_STEPS = 2
_Q_TN = Q_WIDTH // _QKV_STEPS
_ATTN_SCALE = HEAD_DIM ** -0.5
assert math.frexp(_ATTN_SCALE)[0] == 0.5
assert 2 * KV_WIDTH // _QKV_STEPS == KV_WIDTH


def _rope_tables(positions):
    half = HEAD_DIM // 2
    inv = ROPE_THETA ** (-jnp.arange(half, dtype=F32) * (2.0 / HEAD_DIM))
    ang = positions.astype(F32)[:, None] * inv[None, :]
    cos = jnp.cos(ang)
    sin = jnp.sin(ang)
    reps = V7X_LANES // HEAD_DIM
    cos_t = jnp.tile(jnp.concatenate([cos, cos], axis=-1), (1, reps))
    sin_t = jnp.tile(jnp.concatenate([-sin, sin], axis=-1), (1, reps))
    return jnp.stack([cos_t, jnp.ones_like(cos_t)]), jnp.stack([sin_t, jnp.zeros_like(sin_t)])


def _qkv_kernel(x_ref, gain_ref, wq_ref, wkv_ref, cos_ref, sin_ref, cos_kv_ref, sin_kv_ref, q_ref, kv_ref,
                pair_ref=None):
    tm = x_ref.shape[0]
    lane = lax.broadcasted_iota(jnp.int32, (tm, V7X_LANES), 1)
    first_half = (lane % HEAD_DIM) < (HEAD_DIM // 2)
    low_half = lane < HEAD_DIM

    def rope(yc, cos, sin):
        rot = jnp.where(
            first_half,
            pltpu.roll(yc, V7X_LANES - HEAD_DIM // 2, 1),
            pltpu.roll(yc, HEAD_DIM // 2, 1),
        )
        return yc * cos + rot * sin

    h = _rms(x_ref[...], gain_ref[...]).astype(BF16)
    yq = jnp.dot(h, wq_ref[...], preferred_element_type=F32)
    cos, sin = cos_ref[...], sin_ref[...]
    for c in range(_Q_TN // V7X_LANES):
        q_ref[:, _lane_block(c)] = (rope(yq[:, _lane_block(c)], cos, sin) * _ATTN_SCALE).astype(q_ref.dtype)

    ykv = jnp.dot(h, wkv_ref[...], preferred_element_type=F32)
    cos_kv, sin_kv = cos_kv_ref[...], sin_kv_ref[...]
    heads_per_block = V7X_LANES // HEAD_DIM
    for c in range(KV_WIDTH // V7X_LANES):
        r = rope(ykv[:, _lane_block(c)], cos_kv, sin_kv)
        kv_ref[:, _lane_block(c)] = r
        if pair_ref is not None:
            swapped = pltpu.roll(r, HEAD_DIM, 1)
            pair_ref[:, _lane_block(heads_per_block * c)] = jnp.where(low_half, r, swapped).astype(BF16)
            pair_ref[:, _lane_block(heads_per_block * c + 1)] = jnp.where(low_half, swapped, r).astype(BF16)


def _qkv(x, gain, gain_layer, w, w_layer, cos_t, sin_t, *, tm, q_dtype, seq_len=None):
    m = x.shape[0]
    n_tab = cos_t.shape[1] // tm
    paired_kv = seq_len is not None
    pair_width = N_KV_HEADS * V7X_LANES
    vmem = (
        2 * _nbytes((tm, D_MODEL), F32)
        + _nbytes((tm, D_MODEL), BF16)
        + 2 * _nbytes((D_MODEL, _Q_TN + KV_WIDTH), BF16)
        + 8 * _nbytes((tm, _Q_TN + KV_WIDTH), F32)
        + 4 * _nbytes((tm, pair_width), BF16)
    )
    out_specs = [pl.BlockSpec((tm, _Q_TN), lambda j, i: (i, j))]
    out_shape = [jax.ShapeDtypeStruct((m, Q_WIDTH), q_dtype)]
    if paired_kv:
        tiles_per_seq = seq_len // tm
        out_specs.append(pl.BlockSpec((None, tm, KV_WIDTH), lambda j, i: (i // tiles_per_seq, 0, j)))
        out_shape.append(jax.ShapeDtypeStruct((m // seq_len, tm, 2 * KV_WIDTH), F32))
        out_specs.append(pl.BlockSpec((tm, pair_width), lambda j, i: (i, j)))
        out_shape.append(jax.ShapeDtypeStruct((m, 2 * pair_width), BF16))
    else:
        out_specs.append(pl.BlockSpec((tm, KV_WIDTH), lambda j, i: (i, j)))
        out_shape.append(jax.ShapeDtypeStruct((m, 2 * KV_WIDTH), F32))
    rot_spec = pl.BlockSpec((None, tm, V7X_LANES), lambda j, i: (0, i % n_tab, 0))
    rot_or_id_spec = pl.BlockSpec((None, tm, V7X_LANES), lambda j, i: (j, i % n_tab, 0))
    return pl.pallas_call(
        _qkv_kernel,
        grid=(_QKV_STEPS, m // tm),
        in_specs=[
            pl.BlockSpec((tm, D_MODEL), lambda j, i: (i, 0)),
            _row_param_spec(gain_layer, D_MODEL),
            pl.BlockSpec((None, D_MODEL, _Q_TN), lambda j, i: (w_layer, 0, j)),
            pl.BlockSpec((None, D_MODEL, KV_WIDTH), lambda j, i: (w_layer, 0, Q_WIDTH // KV_WIDTH + j)),
            rot_spec, rot_spec, rot_or_id_spec, rot_or_id_spec,
        ],
        out_specs=out_specs,
        out_shape=out_shape,
        compiler_params=_params(("arbitrary", "arbitrary"), vmem),
        name="qkv_rope",
    )(x, _row_param(gain), w, w, cos_t, sin_t, cos_t, sin_t)


def _linear_residual_kernel(o_ref, w_ref, x_ref, out_ref):
    out_ref[...] = x_ref[...] + jnp.dot(o_ref[...].astype(BF16), w_ref[...], preferred_element_type=F32)


def _linear_residual(o, w, layer, x, *, tm):
    m, k = o.shape
    vmem = (
        2 * _nbytes((tm, k), o.dtype)
        + 2 * _nbytes((k, D_MODEL), BF16)
        + 6 * _nbytes((tm, D_MODEL), F32)
    )
    return pl.pallas_call(
        _linear_residual_kernel,
        grid=(m // tm,),
        in_specs=[
            pl.BlockSpec((tm, k), lambda i: (i, 0)),
            pl.BlockSpec((None, k, D_MODEL), lambda i: (layer, 0, 0)),
            pl.BlockSpec((tm, D_MODEL), lambda i: (i, 0)),
        ],
        out_specs=pl.BlockSpec((tm, D_MODEL), lambda i: (i, 0)),
        out_shape=jax.ShapeDtypeStruct((m, D_MODEL), F32),
        compiler_params=_params(("parallel",), vmem),
        name="linear_residual",
    )(o, w, x)


def _head_cols(h):
    return slice(h * HEAD_DIM, (h + 1) * HEAD_DIM)


def _group_sinks(sinks_ref, kvh, g_of_row):
    sink = jnp.full(g_of_row.shape, sinks_ref[kvh * GROUP], F32)
    for g in range(1, GROUP):
        sink = jnp.where(g_of_row == g, sinks_ref[kvh * GROUP + g], sink)
    return sink


def _sink_softmax_weights(s, sink):
    m = jnp.maximum(jnp.max(s, axis=-1, keepdims=True), sink)
    e = jnp.exp(s - m)
    denom = jnp.sum(e, axis=-1, keepdims=True) + jnp.exp(sink - m)
    return e, 1.0 / denom


def _swa_prompt_kernel(sinks_ref, q_ref, k_prev_ref, k_cur_ref, v_prev_ref, v_cur_ref, o_ref):
    n = pl.program_id(1)
    blk = WINDOW
    rows = GROUP * blk
    qi = lax.broadcasted_iota(jnp.int32, (blk, 2 * blk), 0)
    kj = lax.broadcasted_iota(jnp.int32, (blk, 2 * blk), 1)
    d = qi + blk - kj
    mask = ((d >= 0) & (d < WINDOW) & ((kj >= blk) | (n > 0))) | (kj == 0)
    bias = jnp.where(mask, 0.0, -jnp.inf)[None]
    low_half = lax.broadcasted_iota(jnp.int32, (blk, V7X_LANES), 1) < HEAD_DIM
    g_of_row = lax.broadcasted_iota(jnp.int32, (rows, V7X_LANES), 0) // blk
    col = lax.broadcasted_iota(jnp.int32, (rows, V7X_LANES), 1)
    q_onehot = jnp.where((col == g_of_row) | (col == g_of_row + GROUP), 1.0, 0.0).astype(BF16)
    keep_low = jnp.where(low_half, 1.0, 0.0).astype(BF16)
    keep_high = jnp.where(low_half, 0.0, 1.0).astype(BF16)
    head_rows = 2 * V7X_SUBLANES
    is_slot0 = lax.broadcasted_iota(jnp.int32, (head_rows, V7X_LANES), 0) == 0
    kcol = lax.broadcasted_iota(jnp.int32, (1, V7X_LANES), 1)
    ones = jnp.ones((2 * blk, V7X_LANES), BF16)
    no_feat = jnp.zeros((2 * blk - head_rows, V7X_LANES), BF16)

    def without_slot0(prev_ref, cur_ref, kvh):
        head = jnp.where(is_slot0, 0.0, prev_ref[:head_rows, _lane_block(kvh)].astype(F32)).astype(BF16)
        return jnp.concatenate([head, prev_ref[head_rows:, _lane_block(kvh)], cur_ref[:, _lane_block(kvh)]], axis=0)

    for kvh in range(N_KV_HEADS):
        sink_row = jnp.zeros((1, V7X_LANES), F32)
        for g in range(GROUP):
            sink_row = jnp.where((kcol == g) | (kcol == g + GROUP), sinks_ref[kvh * GROUP + g], sink_row)
        sink_hi = sink_row.astype(BF16).astype(F32)
        sink_feat = jnp.where(kcol < GROUP, sink_hi, sink_row - sink_hi)
        feat = jnp.concatenate([jnp.where(is_slot0, sink_feat, 0.0).astype(BF16), no_feat], axis=0)
        k_ext = jnp.concatenate([without_slot0(k_prev_ref, k_cur_ref, kvh), feat], axis=1)
        v_ones = jnp.concatenate([without_slot0(v_prev_ref, v_cur_ref, kvh), ones], axis=1)
        q_heads = []
        for pair in range(GROUP // 2):
            q_pair = q_ref[:, _lane_block(kvh * (GROUP // 2) + pair)]
            q_heads += [q_pair * keep_low, q_pair * keep_high]
        q_ext = jnp.concatenate([jnp.concatenate(q_heads, axis=0), q_onehot], axis=1)
        s = lax.dot_general(q_ext, k_ext, (((1,), (1,)), ((), ())), preferred_element_type=F32)
        s = (s.reshape(GROUP, blk, 2 * blk) + bias).reshape(rows, 2 * blk)
        e = jnp.exp(s - jnp.max(s, axis=-1, keepdims=True)).astype(BF16)
        o_sum = jnp.dot(e, v_ones, preferred_element_type=F32)
        o = o_sum[:, :V7X_LANES] * (1.0 / o_sum[:, V7X_LANES:])
        for pair in range(GROUP // 2):
            even = o[(2 * pair) * blk:(2 * pair + 1) * blk, :]
            odd = o[(2 * pair + 1) * blk:(2 * pair + 2) * blk, :]
            o_ref[:, _lane_block(kvh * (GROUP // 2) + pair)] = jnp.where(low_half, even, odd).astype(o_ref.dtype)


def _swa_prompt(q, kv_pair, sinks):
    nb = SEQ // WINDOW
    pair_width = N_KV_HEADS * V7X_LANES

    def cur(part):
        return pl.BlockSpec((WINDOW, pair_width), lambda b, n: (b * nb + n, part))

    def prev(part):
        return pl.BlockSpec((WINDOW, pair_width), lambda b, n: (b * nb + jnp.maximum(n - 1, 0), part))

    vmem = (
        4 * _nbytes((WINDOW, Q_WIDTH), BF16)
        + 8 * _nbytes((WINDOW, pair_width), BF16)
        + 24 * _nbytes((GROUP * WINDOW, 2 * WINDOW), F32)
    )
    return pl.pallas_call(
        _swa_prompt_kernel,
        grid=(BATCH, nb),
        in_specs=[
            pl.BlockSpec(memory_space=pltpu.SMEM),
            pl.BlockSpec((WINDOW, Q_WIDTH), lambda b, n: (b * nb + n, 0)),
            prev(0), cur(0), prev(1), cur(1),
        ],
        out_specs=pl.BlockSpec((WINDOW, Q_WIDTH), lambda b, n: (b * nb + n, 0)),
        out_shape=jax.ShapeDtypeStruct((BATCH * SEQ, Q_WIDTH), BF16),
        compiler_params=_params(("parallel", "arbitrary"), vmem),
        name="swa_prompt",
    )(sinks, q, kv_pair, kv_pair, kv_pair, kv_pair)


def _swa_sample_kernel(sinks_ref, q_ref, kv_ref, ck_ref, cv_ref, o_ref, nk_ref, nv_ref, kall_scr, vall_scr):
    nb = ck_ref.shape[0]
    t_new = DEC_SEQ
    s_len = WINDOW + t_new
    rows = GROUP * t_new
    kall_scr[:, :WINDOW, :] = ck_ref[...]
    vall_scr[:, :WINDOW, :] = cv_ref[...]
    kall_scr[:, WINDOW:, :] = kv_ref[:, :KV_WIDTH].reshape(nb, t_new, KV_WIDTH)
    vall_scr[:, WINDOW:, :] = kv_ref[:, KV_WIDTH:].reshape(nb, t_new, KV_WIDTH)
    nk_ref[...] = kall_scr[:, t_new:, :]
    nv_ref[...] = vall_scr[:, t_new:, :]

    row = lax.broadcasted_iota(jnp.int32, (1, rows, s_len), 1)
    kj = lax.broadcasted_iota(jnp.int32, (1, rows, s_len), 2)
    d = row % t_new + WINDOW - kj
    mask = (d >= 0) & (d < WINDOW)
    g_of_row = lax.broadcasted_iota(jnp.int32, (1, rows, 1), 1) // t_new
    q3 = q_ref[...].reshape(nb, t_new, Q_WIDTH)
    for kvh in range(N_KV_HEADS):
        k = kall_scr[:, :, _head_cols(kvh)].astype(BF16)
        v = vall_scr[:, :, _head_cols(kvh)].astype(BF16)
        q = jnp.concatenate([q3[:, :, _head_cols(kvh * GROUP + g)] for g in range(GROUP)], axis=1).astype(BF16)
        s = jnp.einsum("bqd,bkd->bqk", q, k, preferred_element_type=F32)
        s = jnp.where(mask, s, -jnp.inf)
        e, inv = _sink_softmax_weights(s, _group_sinks(sinks_ref, kvh, g_of_row))
        o = jnp.einsum("bqk,bkd->bqd", e.astype(BF16), v, preferred_element_type=F32) * inv
        for g in range(GROUP):
            o_ref[:, _head_cols(kvh * GROUP + g)] = o[:, g * t_new:(g + 1) * t_new, :].reshape(nb * t_new, HEAD_DIM)


_SWA_SAMPLE_SEQS_PER_STEP = 8


def _swa_sample(q, kv, cache_k, cache_v, sinks):
    nb = _SWA_SAMPLE_SEQS_PER_STEP
    cache_spec = pl.BlockSpec((nb, WINDOW, KV_WIDTH), lambda b: (b, 0, 0))
    all_keys = (nb, WINDOW + DEC_SEQ, KV_WIDTH)
    vmem = 8 * _nbytes((nb, WINDOW, KV_WIDTH), F32) + 2 * _nbytes(all_keys, F32) + 16 * 1024 * 1024
    return pl.pallas_call(
        _swa_sample_kernel,
        grid=(DEC_BATCH // nb,),
        in_specs=[
            pl.BlockSpec(memory_space=pltpu.SMEM),
            pl.BlockSpec((nb * DEC_SEQ, Q_WIDTH), lambda b: (b, 0)),
            pl.BlockSpec((nb * DEC_SEQ, 2 * KV_WIDTH), lambda b: (b, 0)),
            cache_spec,
            cache_spec,
        ],
        out_specs=[pl.BlockSpec((nb * DEC_SEQ, Q_WIDTH), lambda b: (b, 0)), cache_spec, cache_spec],
        out_shape=[
            jax.ShapeDtypeStruct((N_SAMPLE_ROWS, Q_WIDTH), F32),
            jax.ShapeDtypeStruct((DEC_BATCH, WINDOW, KV_WIDTH), F32),
            jax.ShapeDtypeStruct((DEC_BATCH, WINDOW, KV_WIDTH), F32),
        ],
        scratch_shapes=[pltpu.VMEM(all_keys, F32), pltpu.VMEM(all_keys, F32)],
        compiler_params=_params(("parallel",), vmem),
        name="swa_sample",
    )(sinks, q, kv, cache_k, cache_v)


def _cross_kernel(x_ref, gain_ref, wq_ref, mk_ref, mv_ref, wo_ref, out_ref, *, seqs, rows):
    scale = MEM_HEAD_DIM ** -0.5
    x = x_ref[...]
    h = _rms(x, gain_ref[...]).astype(BF16)
    q = jnp.dot(h, wq_ref[...], preferred_element_type=F32)
    heads = []
    for hd in range(N_MEM_HEADS):
        cs = slice(hd * MEM_HEAD_DIM, (hd + 1) * MEM_HEAD_DIM)
        qh = q[:, cs].reshape(seqs, rows, MEM_HEAD_DIM).astype(BF16)
        if mk_ref.ndim == 4:
            k = mk_ref[:, :, hd, :].astype(BF16)
            v = mv_ref[:, :, hd, :].astype(BF16)
        else:
            k = mk_ref[:, :, cs].astype(BF16)
            v = mv_ref[:, :, cs].astype(BF16)
        s = jnp.einsum("bqd,bkd->bqk", qh, k, preferred_element_type=F32) * scale
        e = jnp.exp(s - jnp.max(s, axis=-1, keepdims=True))
        inv = 1.0 / jnp.sum(e, axis=-1, keepdims=True)
        o = jnp.einsum("bqk,bkd->bqd", e.astype(BF16), v, preferred_element_type=F32) * inv
        heads.append(o.reshape(seqs * rows, MEM_HEAD_DIM))
    o_all = jnp.concatenate(heads, axis=-1).astype(BF16)
    out_ref[...] = x + jnp.dot(o_all, wo_ref[...], preferred_element_type=F32)


def _cross_block(x, gain, wq, mem_k, mem_v, wo, layer, *, n_seq, seqs, rows):
    seq_len = x.shape[0] // n_seq
    nt = seq_len // rows if seqs == 1 else 1
    tile = seqs * rows
    x_spec = pl.BlockSpec((tile, D_MODEL), lambda g, i: (g * nt + i, 0))
    mem_tail = mem_k.shape[3:]
    mem_spec = pl.BlockSpec((None, seqs, N_MEM) + mem_tail, lambda g, i: (layer, g, 0) + (0,) * len(mem_tail))
    mem_pad = V7X_SUBLANES // N_MEM_HEADS if len(mem_tail) == 2 else 1
    vmem = (
        4 * _nbytes((tile, D_MODEL), F32)
        + 4 * mem_pad * _nbytes((seqs, N_MEM, MEM_WIDTH), F32)
        + 4 * _nbytes((D_MODEL, MEM_WIDTH), BF16)
        + 4 * _nbytes((tile, D_MODEL), F32)
        + 8 * _nbytes((tile, N_MEM), F32)
    )
    return pl.pallas_call(
        functools.partial(_cross_kernel, seqs=seqs, rows=rows),
        grid=(n_seq // seqs, nt),
        in_specs=[
            x_spec,
            _row_param_spec(layer, D_MODEL),
            pl.BlockSpec((None, D_MODEL, MEM_WIDTH), lambda g, i: (layer, 0, 0)),
            mem_spec,
            mem_spec,
            pl.BlockSpec((None, MEM_WIDTH, D_MODEL), lambda g, i: (layer, 0, 0)),
        ],
        out_specs=x_spec,
        out_shape=jax.ShapeDtypeStruct(x.shape, F32),
        compiler_params=_params(("parallel", "arbitrary"), vmem),
        name="cross_block",
    )(x, _row_param(gain), wq, mem_k, mem_v, wo)


def _log_sigmoid(x):
    return jnp.minimum(x, 0.0) - jnp.log1p(jnp.exp(-jnp.abs(x)))


def _one_minus_exp2(y, exp_y):
    return jnp.tanh(-y) * (1.0 + exp_y * exp_y)


def _lru_block_cols(n):
    return slice(n * LRU_BLOCK_SIZE, (n + 1) * LRU_BLOCK_SIZE)


def _lru_gate_block(xc, n, wa_ref, ba_ref, wx_ref, bx_ref, lam_ref):
    cs = _lru_block_cols(n)
    xb = xc.astype(BF16)
    r = jax.nn.sigmoid(jnp.dot(xb, wa_ref[n], preferred_element_type=F32) + ba_ref[:, cs])
    ig = jax.nn.sigmoid(jnp.dot(xb, wx_ref[n], preferred_element_type=F32) + bx_ref[:, cs])
    log_a = r * (LRU_C * _log_sigmoid(lam_ref[:, cs]))
    a = jnp.exp(log_a)
    z = _one_minus_exp2(log_a, a)
    u = jnp.where(z > 0.0, z * lax.rsqrt(z), 0.0) * (ig * xc)
    return a, u


def _lru_gates(xc, wa_ref, ba_ref, wx_ref, bx_ref, lam_ref):
    parts = [_lru_gate_block(xc[:, _lru_block_cols(n)], n, wa_ref, ba_ref, wx_ref, bx_ref, lam_ref)
             for n in range(LRU_BLOCKS)]
    return jnp.concatenate([p[0] for p in parts], axis=-1), jnp.concatenate([p[1] for p in parts], axis=-1)


def _lru_prompt_kernel(xb_ref, gate_ref, x_prev_ref, wc_ref, bc_ref, wa_ref, ba_ref, wx_ref, bx_ref, lam_ref,
                       wo_ref, out_ref, hlast_ref, xext_scr, a_scr, u_scr, h_scr, y_scr, *, tiles_per_seq, n_tiles):
    s = pl.program_id(0)
    tm = xb_ref.shape[0]
    pad = V7X_SUBLANES

    @pl.when(s % tiles_per_seq == 0)
    def _():
        xext_scr[0:pad, :] = jnp.zeros((pad, LRU_WIDTH), F32)
        h_scr[...] = jnp.zeros((1, LRU_WIDTH), F32)

    @pl.when(s == 0)
    def _():
        y_scr[...] = jnp.zeros(y_scr.shape, BF16)

    y_prev = y_scr.at[s % 2]
    y_next = y_scr.at[(s + 1) % 2]
    xext_scr[pad:pad + tm, :] = xb_ref[...]
    for n in range(LRU_BLOCKS):
        cs = _lru_block_cols(n)
        out_ref[:, cs] = x_prev_ref[:, cs] + jnp.dot(y_prev[...], wo_ref[:, cs], preferred_element_type=F32)
        xc = bc_ref[:, cs] + wc_ref[CONV_WIDTH - 1:CONV_WIDTH, cs] * xext_scr[pad:pad + tm, cs]
        for k in range(1, CONV_WIDTH):
            xc = xc + wc_ref[CONV_WIDTH - 1 - k:CONV_WIDTH - k, cs] * xext_scr[pad - k:pad - k + tm, cs]
        a_scr[:, cs], u_scr[:, cs] = _lru_gate_block(xc, n, wa_ref, ba_ref, wx_ref, bx_ref, lam_ref)
        h = h_scr[:, cs]
        for t in range(tm):
            h = a_scr[t:t + 1, cs] * h + u_scr[t:t + 1, cs]
            u_scr[t:t + 1, cs] = h
        h_scr[:, cs] = h
        y_next[:, cs] = (u_scr[:, cs] * gate_ref[:, cs]).astype(BF16)
    xext_scr[0:pad, :] = xext_scr[tm:tm + pad, :]

    @pl.when(s < n_tiles)
    def _():
        hlast_ref[...] = u_scr[tm - pad:tm, :]


def _lru_weight_specs(layer_idx):
    def const3(*_):
        return (layer_idx, 0, 0)

    def const4(*_):
        return (layer_idx, 0, 0, 0)

    gate_w = pl.BlockSpec((None, LRU_BLOCKS, LRU_BLOCK_SIZE, LRU_BLOCK_SIZE), const4)
    vec = _row_param_spec(layer_idx, LRU_WIDTH)
    return [
        pl.BlockSpec((None, CONV_WIDTH, LRU_WIDTH), const3),
        vec,
        gate_w,
        vec,
        gate_w,
        vec,
        vec,
        pl.BlockSpec((None, LRU_WIDTH, D_MODEL), const3),
    ]


def _lru_weight_args(w_conv, b_conv, w_gate_a, b_gate_a, w_gate_x, b_gate_x, lam, w_out):
    return (w_conv, _row_param(b_conv), w_gate_a, _row_param(b_gate_a), w_gate_x, _row_param(b_gate_x),
            _row_param(lam), w_out)


def _lru_prompt(xb, gate, x, lw, j, *, tm):
    tiles_per_seq = SEQ // tm
    n_tiles = BATCH * tiles_per_seq
    cur_spec = pl.BlockSpec((tm, LRU_WIDTH), lambda s: (jnp.minimum(s, n_tiles - 1), 0))
    prev_spec = pl.BlockSpec((tm, LRU_WIDTH), lambda s: (jnp.maximum(s - 1, 0), 0))
    vmem = (
        8 * _nbytes((tm, LRU_WIDTH), F32)
        + 3 * _nbytes((tm + V7X_SUBLANES, LRU_WIDTH), F32)
        + 2 * _nbytes((LRU_WIDTH, D_MODEL), BF16)
        + 4 * _nbytes((LRU_BLOCKS, LRU_BLOCK_SIZE, LRU_BLOCK_SIZE), BF16)
        + 10 * _nbytes((tm, LRU_WIDTH), F32)
    )
    return pl.pallas_call(
        functools.partial(_lru_prompt_kernel, tiles_per_seq=tiles_per_seq, n_tiles=n_tiles),
        grid=(n_tiles + 1,),
        in_specs=[cur_spec, cur_spec, prev_spec] + _lru_weight_specs(j),
        out_specs=[
            prev_spec,
            pl.BlockSpec((None, V7X_SUBLANES, LRU_WIDTH),
                         lambda s: (jnp.minimum(s, n_tiles - 1) // tiles_per_seq, 0, 0)),
        ],
        out_shape=[
            jax.ShapeDtypeStruct((BATCH * SEQ, D_MODEL), F32),
            jax.ShapeDtypeStruct((BATCH, V7X_SUBLANES, LRU_WIDTH), F32),
        ],
        scratch_shapes=[
            pltpu.VMEM((tm + V7X_SUBLANES, LRU_WIDTH), F32),
            pltpu.VMEM((tm, LRU_WIDTH), F32),
            pltpu.VMEM((tm, LRU_WIDTH), F32),
            pltpu.VMEM((1, LRU_WIDTH), F32),
            pltpu.VMEM((2, tm, LRU_WIDTH), BF16),
        ],
        compiler_params=_params(("arbitrary",), vmem),
        name="lru_prompt",
    )(xb, gate, x, *lw)


def _lru_sample_kernel(xb_ref, gate_ref, x_ref, cbuf_ref, h0_ref, wc_ref, bc_ref, wa_ref, ba_ref, wx_ref, bx_ref,
                       lam_ref, wo_ref, out_ref, hs_ref):
    rows = xb_ref.shape[0]
    t = lax.broadcasted_iota(jnp.int32, (rows, LRU_WIDTH), 0) % DEC_SEQ
    xb = xb_ref[...]
    cbuf = cbuf_ref[...]
    xc = bc_ref[...] + wc_ref[CONV_WIDTH - 1:CONV_WIDTH, :] * xb
    for k in range(1, CONV_WIDTH):
        shift_c = CONV_WIDTH - 1 - k
        from_buf = cbuf if shift_c == 0 else pltpu.roll(cbuf, rows - shift_c, 0)
        xk = jnp.where(t >= k, pltpu.roll(xb, k, 0), from_buf)
        xc = xc + wc_ref[CONV_WIDTH - 1 - k:CONV_WIDTH - k, :] * xk

    a, u = _lru_gates(xc, wa_ref, ba_ref, wx_ref, bx_ref, lam_ref)
    k = 1
    while k < DEC_SEQ:
        m = t >= k
        u = jnp.where(m, a * pltpu.roll(u, k, 0) + u, u)
        a = jnp.where(m, a * pltpu.roll(a, k, 0), a)
        k *= 2
    hs = u + a * h0_ref[...]
    hs_ref[...] = hs
    y = (hs * gate_ref[...]).astype(BF16)
    out_ref[...] = x_ref[...] + jnp.dot(y, wo_ref[...], preferred_element_type=F32)


def _lru_sample(xb, gate, x, cbuf_rows, h0_rows, lw, j):
    rows = N_SAMPLE_ROWS
    row_spec0 = pl.BlockSpec((rows, LRU_WIDTH), lambda i: (0, 0))
    vmem = 40 * _nbytes((rows, LRU_WIDTH), F32) + 2 * _nbytes((LRU_WIDTH, D_MODEL), BF16)
    return pl.pallas_call(
        _lru_sample_kernel,
        grid=(1,),
        in_specs=[row_spec0] * 5 + _lru_weight_specs(j),
        out_specs=[row_spec0, row_spec0],
        out_shape=[
            jax.ShapeDtypeStruct((rows, D_MODEL), F32),
            jax.ShapeDtypeStruct((rows, LRU_WIDTH), F32),
        ],
        compiler_params=_params(("arbitrary",), vmem),
        name="lru_sample",
    )(xb, gate, x, cbuf_rows, h0_rows, *lw)


_FFN_TM = 1024
_FFN_TF = 512
_FFN_ROW_CHUNK = 512


def kernel(x_prompt, x_sample, cache_swa_k, cache_swa_v, state_lru_h, state_lru_conv, cache_mem_k, cache_mem_v,
           mem_prompt, ln_ffn1, ffn1_w_in, ffn1_w_out, ln_mix, swa_w_qkv, swa_w_o, swa_sinks, lru_w_in,
           lru_w_conv, lru_b_conv, lru_w_gate_a, lru_b_gate_a, lru_w_gate_x, lru_b_gate_x, lru_lambda,
           lru_w_out, ln_cross, ln_mem, cross_w_q, cross_w_kv, cross_w_o, ln_ffn2, ffn2_w_in, ffn2_w_out,
           ln_final):
    bf = lambda w: w.astype(BF16)
    ffn_seq = []
    for layer in range(DEPTH):
        ffn_seq += [(ffn1_w_in, ffn1_w_out, layer), (ffn2_w_in, ffn2_w_out, layer)]
    ffn_w = (bf(ffn1_w_in[0]), bf(ffn1_w_out[0]))

    def ffn(x, x_extra, gain, layer, final_gain=None):
        nonlocal ffn_w
        nxt = ffn_seq[1] if len(ffn_seq) > 1 else None
        out = _ffn(x, x_extra, gain, layer, *ffn_w, tm=_FFN_TM, tf=_FFN_TF, row_chunk=_FFN_ROW_CHUNK,
                   final_gain=final_gain, cast_next=nxt)
        del ffn_seq[0]
        if nxt is not None:
            ffn_w = out[2:]
        return out[0], out[1]

    swa_w_qkv, swa_w_o = bf(swa_w_qkv), bf(swa_w_o)
    lru_w_in, lru_w_out = bf(lru_w_in), bf(lru_w_out)
    lru_w_gate_a, lru_w_gate_x = bf(lru_w_gate_a), bf(lru_w_gate_x)
    cross_w_q, cross_w_kv, cross_w_o = bf(cross_w_q), bf(cross_w_kv), bf(cross_w_o)

    xp = x_prompt.reshape(N_PROMPT_ROWS, D_MODEL)
    xs = x_sample.reshape(N_SAMPLE_ROWS, D_MODEL)

    mem_rows = mem_prompt.reshape(BATCH * N_MEM, D_MODEL)
    mkv = [
        _norm_linear(mem_rows, ln_mem, layer, cross_w_kv, layer, tm=BATCH * N_MEM, tn=MEM_WIDTH, out_dtype=F32)
        for layer in range(DEPTH)
    ]
    mem_k_prompt = jnp.stack([m[:, :MEM_WIDTH] for m in mkv]).reshape(DEPTH, BATCH, N_MEM, MEM_WIDTH)
    mem_v_prompt = jnp.stack([m[:, MEM_WIDTH:] for m in mkv]).reshape(DEPTH, BATCH, N_MEM, MEM_WIDTH)

    cos_p, sin_p = _rope_tables(jnp.arange(SEQ, dtype=jnp.int32))
    cos_s, sin_s = _rope_tables(PAST_LEN + jnp.arange(N_SAMPLE_ROWS, dtype=jnp.int32) % DEC_SEQ)

    swa_k_p, swa_v_p, swa_k_s, swa_v_s = [], [], [], []
    lru_h_p, lru_conv_p, lru_h_s, lru_conv_s = [], [], [], []

    for layer in range(DEPTH):
        j = layer // 2
        xp, xs = ffn(xp, xs, ln_ffn1, layer)

        if layer % 2 == 0:
            qkv_tm = 1024
            q_p, kv_last, kv_pair = _qkv(xp, ln_mix, layer, swa_w_qkv, j, cos_p, sin_p, tm=qkv_tm, q_dtype=BF16,
                                         seq_len=SEQ)
            o_p = _swa_prompt(q_p, kv_pair, swa_sinks[j])
            xp = _linear_residual(o_p, swa_w_o, j, xp, tm=512)
            kv_tail = kv_last[:, qkv_tm - WINDOW:, :]
            swa_k_p.append(kv_tail[..., :KV_WIDTH].reshape(BATCH, WINDOW, N_KV_HEADS, HEAD_DIM))
            swa_v_p.append(kv_tail[..., KV_WIDTH:].reshape(BATCH, WINDOW, N_KV_HEADS, HEAD_DIM))

            q_s, kv_s = _qkv(xs, ln_mix, layer, swa_w_qkv, j, cos_s, sin_s, tm=N_SAMPLE_ROWS, q_dtype=F32)
            o_s, nk, nv = _swa_sample(
                q_s, kv_s,
                cache_swa_k[j].reshape(DEC_BATCH, WINDOW, KV_WIDTH),
                cache_swa_v[j].reshape(DEC_BATCH, WINDOW, KV_WIDTH),
                swa_sinks[j],
            )
            xs = _linear_residual(o_s, swa_w_o, j, xs, tm=N_SAMPLE_ROWS)
            swa_k_s.append(nk.reshape(DEC_BATCH, WINDOW, N_KV_HEADS, HEAD_DIM))
            swa_v_s.append(nv.reshape(DEC_BATCH, WINDOW, N_KV_HEADS, HEAD_DIM))
        else:
            lw = _lru_weight_args(lru_w_conv, lru_b_conv, lru_w_gate_a, lru_b_gate_a, lru_w_gate_x, lru_b_gate_x,
                                  lru_lambda, lru_w_out)
            xb_p, gate_p = _lru_in(xp, ln_mix, layer, lru_w_in, j, tm=512, tn=LRU_WIDTH)
            xp, h_last = _lru_prompt(xb_p, gate_p, xp, lw, j, tm=256)
            lru_h_p.append(h_last[:, V7X_SUBLANES - 1, :])
            lru_conv_p.append(xb_p.reshape(BATCH, SEQ, LRU_WIDTH)[:, SEQ - (CONV_WIDTH - 1):, :])

            xb_s, gate_s = _lru_in(xs, ln_mix, layer, lru_w_in, j, tm=N_SAMPLE_ROWS, tn=LRU_WIDTH)
            cbuf_rows = jnp.pad(
                state_lru_conv[j], ((0, 0), (0, DEC_SEQ - (CONV_WIDTH - 1)), (0, 0))
            ).reshape(N_SAMPLE_ROWS, LRU_WIDTH)
            h0_rows = jnp.repeat(state_lru_h[j], DEC_SEQ, axis=0)
            xs, hs_s = _lru_sample(xb_s, gate_s, xs, cbuf_rows, h0_rows, lw, j)
            lru_h_s.append(hs_s.reshape(DEC_BATCH, DEC_SEQ, LRU_WIDTH)[:, DEC_SEQ - 1, :])
            xp_tail = jnp.concatenate(
                [state_lru_conv[j], xb_s.reshape(DEC_BATCH, DEC_SEQ, LRU_WIDTH)], axis=1)
            lru_conv_s.append(xp_tail[:, -(CONV_WIDTH - 1):, :])

        xp = _cross_block(xp, ln_cross, cross_w_q, mem_k_prompt, mem_v_prompt, cross_w_o, layer,
                          n_seq=BATCH, seqs=1, rows=1024)
        xs = _cross_block(xs, ln_cross, cross_w_q, cache_mem_k, cache_mem_v, cross_w_o, layer,
                          n_seq=DEC_BATCH, seqs=8, rows=DEC_SEQ)

        fin = ln_final if layer == DEPTH - 1 else None
        xp, xs = ffn(xp, xs, ln_ffn2, layer, final_gain=fin)

    return (
        xp.reshape(BATCH, SEQ, D_MODEL),
        xs.reshape(DEC_BATCH, DEC_SEQ, D_MODEL),
        jnp.stack(swa_k_p), jnp.stack(swa_v_p), jnp.stack(swa_k_s), jnp.stack(swa_v_s),
        jnp.stack(lru_h_p), jnp.stack(lru_conv_p), jnp.stack(lru_h_s), jnp.stack(lru_conv_s),
        mem_k_prompt.reshape(DEPTH, BATCH, N_MEM, N_MEM_HEADS, MEM_HEAD_DIM),
        mem_v_prompt.reshape(DEPTH, BATCH, N_MEM, N_MEM_HEADS, MEM_HEAD_DIM),
    )
```

```python
import functools
import math

import jax
import jax.numpy as jnp
from jax import lax
from jax.experimental import pallas as pl
from jax.experimental.pallas import tpu as pltpu

F32 = jnp.float32
BF16 = jnp.bfloat16

D_MODEL = 2048
BATCH = 2
SEQ = 4096
DEPTH = 2
DEC_BATCH = 32
DEC_SEQ = 8
PAST_LEN = 16384
N_HEADS = 32
N_KV_HEADS = 8
HEAD_DIM = 64
GROUP = N_HEADS // N_KV_HEADS
WINDOW = 128
ROPE_THETA = 10000.0
LRU_WIDTH = D_MODEL
LRU_BLOCKS = 8
LRU_BLOCK_SIZE = LRU_WIDTH // LRU_BLOCKS
CONV_WIDTH = 4
LRU_C = 8.0
D_FF = 5632
N_MEM = 256
N_MEM_HEADS = 4
MEM_HEAD_DIM = 128
MEM_WIDTH = N_MEM_HEADS * MEM_HEAD_DIM
NORM_EPS = 1e-6
Q_WIDTH = N_HEADS * HEAD_DIM
KV_WIDTH = N_KV_HEADS * HEAD_DIM

V7X_LANES = 128
V7X_SUBLANES = 8
V7X_VMEM_BYTES = 64 * 1024 * 1024
V7X_VMEM_USABLE_BYTES = V7X_VMEM_BYTES - 2 * 1024 * 1024

N_PROMPT_ROWS = BATCH * SEQ
N_SAMPLE_ROWS = DEC_BATCH * DEC_SEQ


def _params(semantics, vmem_bytes):
    return pltpu.CompilerParams(
        dimension_semantics=semantics,
        vmem_limit_bytes=int(min(max(vmem_bytes, 32 * 1024 * 1024), V7X_VMEM_USABLE_BYTES)),
    )


def _nbytes(shape, dtype):
    return math.prod(shape) * jnp.dtype(dtype).itemsize


def _rms(x, gain):
    ms = jnp.mean(x * x, axis=-1, keepdims=True)
    return x * lax.rsqrt(ms + NORM_EPS) * gain


def _lane_block(c):
    return slice(c * V7X_LANES, (c + 1) * V7X_LANES)


def _row_param(p):
    return p.reshape(p.shape[0], 1, p.shape[-1])


def _row_param_spec(layer, width):
    return pl.BlockSpec((None, 1, width), lambda *_: (layer, 0, 0))


def _for_chunks(n_rows, chunk, body):
    n_chunks = n_rows // chunk
    if n_chunks == 1:
        body(pl.ds(0, n_rows))
        return

    def step(c, carry):
        body(pl.ds(pl.multiple_of(c * chunk, chunk), chunk))
        return carry

    lax.fori_loop(0, n_chunks, step, 0)


def _ffn_kernel(x_ref, gain_ref, wg_ref, wu_ref, wo_ref, xe_ref, *rest, row_chunk, final_norm, cast_next):
    rest = list(rest)
    gfin_ref = rest.pop(0) if final_norm else None
    next_in_ref, next_out_ref = (rest.pop(0), rest.pop(0)) if cast_next else (None, None)
    o_ref, oe_ref = rest.pop(0), rest.pop(0)
    next_in_bf_ref, next_out_bf_ref = (rest.pop(0), rest.pop(0)) if cast_next else (None, None)
    h_scr, he_scr = rest
    f = pl.program_id(1)

    def run(x_ref, o_ref, h_scr, chunk, side_job=None):
        n_rows = x_ref.shape[0]

        @pl.when(f == 0)
        def _():
            def prologue(rows):
                h_scr[rows, :] = _rms(x_ref[rows, :], gain_ref[...]).astype(BF16)
                o_ref[rows, :] = jnp.zeros((chunk, D_MODEL), F32)

            _for_chunks(n_rows, chunk, prologue)

        chunks = [pl.ds(c * chunk, chunk) for c in range(n_rows // chunk)]
        acts = []
        for rows in chunks:
            h = h_scr[rows, :]
            g = jnp.dot(h, wg_ref[...], preferred_element_type=F32)
            u = jnp.dot(h, wu_ref[...], preferred_element_type=F32)
            acts.append((g * jax.nn.sigmoid(g) * u).astype(BF16))
        for rows, a in zip(chunks, acts):
            o_ref[rows, :] += jnp.dot(a, wo_ref[...], preferred_element_type=F32)
        if side_job is not None:
            side_job()

        @pl.when(f == pl.num_programs(1) - 1)
        def _():
            def epilogue(rows):
                y = x_ref[rows, :] + 0.5 * o_ref[rows, :]
                if final_norm:
                    y = _rms(y, gfin_ref[...])
                o_ref[rows, :] = y

            _for_chunks(n_rows, chunk, epilogue)

    def round_next_weights():
        next_in_bf_ref[...] = next_in_ref[...].astype(BF16)
        next_out_bf_ref[...] = next_out_ref[...].astype(BF16)

    run(x_ref, o_ref, h_scr, row_chunk, side_job=round_next_weights if cast_next else None)

    @pl.when(pl.program_id(0) == 0)
    def _():
        run(xe_ref, oe_ref, he_scr, xe_ref.shape[0])


def _ffn(x, x_extra, gain, layer, w_in, w_out, *, tm, tf, row_chunk, final_gain=None, cast_next=None):
    m = x.shape[0]
    me = x_extra.shape[0]
    n_tiles = m // tm
    nf = D_FF // tf
    final_norm = final_gain is not None
    row_spec = pl.BlockSpec((tm, D_MODEL), lambda i, f: (i, 0))
    extra_spec = pl.BlockSpec((me, D_MODEL), lambda i, f: (0, 0))
    in_specs = [
        row_spec,
        _row_param_spec(layer, D_MODEL),
        pl.BlockSpec((D_MODEL, tf), lambda i, f: (0, f)),
        pl.BlockSpec((D_MODEL, tf), lambda i, f: (0, f + nf)),
        pl.BlockSpec((tf, D_MODEL), lambda i, f: (f, 0)),
        pl.BlockSpec((me, D_MODEL), lambda i, f: (0, 0), pipeline_mode=pl.Buffered(1)),
    ]
    args = [x, _row_param(gain), w_in, w_in, w_out, x_extra]
    out_specs = [row_spec, extra_spec]
    out_shape = [jax.ShapeDtypeStruct((m, D_MODEL), F32), jax.ShapeDtypeStruct((me, D_MODEL), F32)]
    if final_norm:
        in_specs.append(_row_param_spec(0, D_MODEL))
        args.append(final_gain.reshape(1, 1, D_MODEL))
    vmem = (
        4 * _nbytes((tm, D_MODEL), F32)
        + _nbytes((tm, D_MODEL), BF16)
        + 4 * _nbytes((me, D_MODEL), F32)
        + _nbytes((me, D_MODEL), BF16)
        + 2 * 3 * _nbytes((D_MODEL, tf), BF16)
        + 4 * _nbytes((row_chunk, tf), F32)
        + 2 * _nbytes((row_chunk, D_MODEL), F32)
    )
    if cast_next is not None:
        nw_in, nw_out, nl = cast_next
        in_blk = (D_MODEL // n_tiles, 2 * D_FF // nf)
        out_blk = (D_FF // nf, D_MODEL // n_tiles)
        in_specs += [
            pl.BlockSpec((None,) + in_blk, lambda i, f: (nl, i, f)),
            pl.BlockSpec((None,) + out_blk, lambda i, f: (nl, f, i)),
        ]
        args += [nw_in, nw_out]
        out_specs += [pl.BlockSpec(in_blk, lambda i, f: (i, f)), pl.BlockSpec(out_blk, lambda i, f: (f, i))]
        out_shape += [jax.ShapeDtypeStruct(nw_in.shape[1:], BF16), jax.ShapeDtypeStruct(nw_out.shape[1:], BF16)]
        vmem += 2 * (_nbytes(in_blk, F32) + _nbytes(in_blk, BF16) + _nbytes(out_blk, F32) + _nbytes(out_blk, BF16))
    return pl.pallas_call(
        functools.partial(_ffn_kernel, row_chunk=row_chunk, final_norm=final_norm, cast_next=cast_next is not None),
        grid=(n_tiles, nf),
        in_specs=in_specs,
        out_specs=out_specs,
        out_shape=out_shape,
        scratch_shapes=[pltpu.VMEM((tm, D_MODEL), BF16), pltpu.VMEM((me, D_MODEL), BF16)],
        compiler_params=_params(("arbitrary", "arbitrary"), vmem),
        name="ffn",
    )(*args)


def _gelu_tanh(y):
    return 0.5 * y * (1.0 + jnp.tanh(math.sqrt(2.0 / math.pi) * (y + 0.044715 * (y * y * y))))


def _norm_linear_kernel(x_ref, gain_ref, w_ref, o_ref, h_scr):
    @pl.when(pl.program_id(1) == 0)
    def _():
        h_scr[...] = _rms(x_ref[...], gain_ref[...]).astype(BF16)

    o_ref[...] = jnp.dot(h_scr[...], w_ref[...], preferred_element_type=F32).astype(o_ref.dtype)


def _lru_in_kernel(x_ref, gain_ref, wx_ref, wg_ref, xb_ref, gate_ref, h_scr):
    @pl.when(pl.program_id(1) == 0)
    def _():
        h_scr[...] = _rms(x_ref[...], gain_ref[...]).astype(BF16)

    h = h_scr[...]
    gate_ref[...] = _gelu_tanh(jnp.dot(h, wg_ref[...], preferred_element_type=F32))
    xb_ref[...] = jnp.dot(h, wx_ref[...], preferred_element_type=F32)


def _lru_in(x, gain, gain_layer, w, w_layer, *, tm, tn):
    m = x.shape[0]
    n_col = LRU_WIDTH // tn
    vmem = (
        2 * _nbytes((tm, D_MODEL), F32)
        + _nbytes((tm, D_MODEL), BF16)
        + 4 * _nbytes((D_MODEL, tn), BF16)
        + 10 * _nbytes((tm, tn), F32)
    )
    out_spec = pl.BlockSpec((tm, tn), lambda i, j: (i, j))
    return pl.pallas_call(
        _lru_in_kernel,
        grid=(m // tm, n_col),
        in_specs=[
            pl.BlockSpec((tm, D_MODEL), lambda i, j: (i, 0)),
            _row_param_spec(gain_layer, D_MODEL),
            pl.BlockSpec((None, D_MODEL, tn), lambda i, j: (w_layer, 0, j)),
            pl.BlockSpec((None, D_MODEL, tn), lambda i, j: (w_layer, 0, n_col + j)),
        ],
        out_specs=[out_spec, out_spec],
        out_shape=[jax.ShapeDtypeStruct((m, LRU_WIDTH), F32)] * 2,
        scratch_shapes=[pltpu.VMEM((tm, D_MODEL), BF16)],
        compiler_params=_params(("parallel", "arbitrary"), vmem),
        name="lru_in",
    )(x, _row_param(gain), w, w)


def _norm_linear(x, gain, gain_layer, w, w_layer, *, tm, tn, out_dtype):
    m = x.shape[0]
    n = w.shape[-1]
    vmem = (
        2 * _nbytes((tm, D_MODEL), F32)
        + _nbytes((tm, D_MODEL), BF16)
        + 2 * _nbytes((D_MODEL, tn), BF16)
        + 4 * _nbytes((tm, tn), F32)
    )
    return pl.pallas_call(
        _norm_linear_kernel,
        grid=(m // tm, n // tn),
        in_specs=[
            pl.BlockSpec((tm, D_MODEL), lambda i, j: (i, 0)),
            _row_param_spec(gain_layer, D_MODEL),
            pl.BlockSpec((None, D_MODEL, tn), lambda i, j: (w_layer, 0, j)),
        ],
        out_specs=pl.BlockSpec((tm, tn), lambda i, j: (i, j)),
        out_shape=jax.ShapeDtypeStruct((m, n), out_dtype),
        scratch_shapes=[pltpu.VMEM((tm, D_MODEL), BF16)],
        compiler_params=_params(("parallel", "arbitrary"), vmem),
        name="norm_linear",
    )(x, _row_param(gain), w)


_QKV_STEPS = 2
_Q_TN = Q_WIDTH // _QKV_STEPS
_ATTN_SCALE = HEAD_DIM ** -0.5
assert math.frexp(_ATTN_SCALE)[0] == 0.5
assert 2 * KV_WIDTH // _QKV_STEPS == KV_WIDTH


def _rope_tables(positions):
    half = HEAD_DIM // 2
    inv = ROPE_THETA ** (-jnp.arange(half, dtype=F32) * (2.0 / HEAD_DIM))
    ang = positions.astype(F32)[:, None] * inv[None, :]
    cos = jnp.cos(ang)
    sin = jnp.sin(ang)
    reps = V7X_LANES // HEAD_DIM
    cos_t = jnp.tile(jnp.concatenate([cos, cos], axis=-1), (1, reps))
    sin_t = jnp.tile(jnp.concatenate([-sin, sin], axis=-1), (1, reps))
    return jnp.stack([cos_t, jnp.ones_like(cos_t)]), jnp.stack([sin_t, jnp.zeros_like(sin_t)])


def _qkv_kernel(x_ref, gain_ref, wq_ref, wkv_ref, cos_ref, sin_ref, cos_kv_ref, sin_kv_ref, q_ref, kv_ref,
                pair_ref=None):
    tm = x_ref.shape[0]
    lane = lax.broadcasted_iota(jnp.int32, (tm, V7X_LANES), 1)
    first_half = (lane % HEAD_DIM) < (HEAD_DIM // 2)
    low_half = lane < HEAD_DIM

    def rope(yc, cos, sin):
        rot = jnp.where(
            first_half,
            pltpu.roll(yc, V7X_LANES - HEAD_DIM // 2, 1),
            pltpu.roll(yc, HEAD_DIM // 2, 1),
        )
        return yc * cos + rot * sin

    h = _rms(x_ref[...], gain_ref[...]).astype(BF16)
    yq = jnp.dot(h, wq_ref[...], preferred_element_type=F32)
    cos, sin = cos_ref[...], sin_ref[...]
    for c in range(_Q_TN // V7X_LANES):
        q_ref[:, _lane_block(c)] = (rope(yq[:, _lane_block(c)], cos, sin) * _ATTN_SCALE).astype(q_ref.dtype)

    ykv = jnp.dot(h, wkv_ref[...], preferred_element_type=F32)
    cos_kv, sin_kv = cos_kv_ref[...], sin_kv_ref[...]
    heads_per_block = V7X_LANES // HEAD_DIM
    for c in range(KV_WIDTH // V7X_LANES):
        r = rope(ykv[:, _lane_block(c)], cos_kv, sin_kv)
        kv_ref[:, _lane_block(c)] = r
        if pair_ref is not None:
            swapped = pltpu.roll(r, HEAD_DIM, 1)
            pair_ref[:, _lane_block(heads_per_block * c)] = jnp.where(low_half, r, swapped).astype(BF16)
            pair_ref[:, _lane_block(heads_per_block * c + 1)] = jnp.where(low_half, swapped, r).astype(BF16)


def _qkv(x, gain, gain_layer, w, w_layer, cos_t, sin_t, *, tm, q_dtype, seq_len=None):
    m = x.shape[0]
    n_tab = cos_t.shape[1] // tm
    paired_kv = seq_len is not None
    pair_width = N_KV_HEADS * V7X_LANES
    vmem = (
        2 * _nbytes((tm, D_MODEL), F32)
        + _nbytes((tm, D_MODEL), BF16)
        + 2 * _nbytes((D_MODEL, _Q_TN + KV_WIDTH), BF16)
        + 8 * _nbytes((tm, _Q_TN + KV_WIDTH), F32)
        + 4 * _nbytes((tm, pair_width), BF16)
    )
    out_specs = [pl.BlockSpec((tm, _Q_TN), lambda j, i: (i, j))]
    out_shape = [jax.ShapeDtypeStruct((m, Q_WIDTH), q_dtype)]
    if paired_kv:
        tiles_per_seq = seq_len // tm
        out_specs.append(pl.BlockSpec((None, tm, KV_WIDTH), lambda j, i: (i // tiles_per_seq, 0, j)))
        out_shape.append(jax.ShapeDtypeStruct((m // seq_len, tm, 2 * KV_WIDTH), F32))
        out_specs.append(pl.BlockSpec((tm, pair_width), lambda j, i: (i, j)))
        out_shape.append(jax.ShapeDtypeStruct((m, 2 * pair_width), BF16))
    else:
        out_specs.append(pl.BlockSpec((tm, KV_WIDTH), lambda j, i: (i, j)))
        out_shape.append(jax.ShapeDtypeStruct((m, 2 * KV_WIDTH), F32))
    rot_spec = pl.BlockSpec((None, tm, V7X_LANES), lambda j, i: (0, i % n_tab, 0))
    rot_or_id_spec = pl.BlockSpec((None, tm, V7X_LANES), lambda j, i: (j, i % n_tab, 0))
    return pl.pallas_call(
        _qkv_kernel,
        grid=(_QKV_STEPS, m // tm),
        in_specs=[
            pl.BlockSpec((tm, D_MODEL), lambda j, i: (i, 0)),
            _row_param_spec(gain_layer, D_MODEL),
            pl.BlockSpec((None, D_MODEL, _Q_TN), lambda j, i: (w_layer, 0, j)),
            pl.BlockSpec((None, D_MODEL, KV_WIDTH), lambda j, i: (w_layer, 0, Q_WIDTH // KV_WIDTH + j)),
            rot_spec, rot_spec, rot_or_id_spec, rot_or_id_spec,
        ],
        out_specs=out_specs,
        out_shape=out_shape,
        compiler_params=_params(("arbitrary", "arbitrary"), vmem),
        name="qkv_rope",
    )(x, _row_param(gain), w, w, cos_t, sin_t, cos_t, sin_t)


def _linear_residual_kernel(o_ref, w_ref, x_ref, out_ref):
    out_ref[...] = x_ref[...] + jnp.dot(o_ref[...].astype(BF16), w_ref[...].astype(BF16),
                                        preferred_element_type=F32)


def _linear_residual(o, w, layer, x, *, tm):
    m, k = o.shape
    vmem = (
        2 * _nbytes((tm, k), o.dtype)
        + 2 * _nbytes((k, D_MODEL), w.dtype)
        + _nbytes((k, D_MODEL), BF16)
        + 6 * _nbytes((tm, D_MODEL), F32)
    )
    return pl.pallas_call(
        _linear_residual_kernel,
        grid=(m // tm,),
        in_specs=[
            pl.BlockSpec((tm, k), lambda i: (i, 0)),
            pl.BlockSpec((None, k, D_MODEL), lambda i: (layer, 0, 0)),
            pl.BlockSpec((tm, D_MODEL), lambda i: (i, 0)),
        ],
        out_specs=pl.BlockSpec((tm, D_MODEL), lambda i: (i, 0)),
        out_shape=jax.ShapeDtypeStruct((m, D_MODEL), F32),
        compiler_params=_params(("parallel",), vmem),
        name="linear_residual",
    )(o, w, x)


def _head_cols(h):
    return slice(h * HEAD_DIM, (h + 1) * HEAD_DIM)


def _group_sinks(sinks_ref, kvh, g_of_row):
    sink = jnp.full(g_of_row.shape, sinks_ref[kvh * GROUP], F32)
    for g in range(1, GROUP):
        sink = jnp.where(g_of_row == g, sinks_ref[kvh * GROUP + g], sink)
    return sink


def _sink_softmax_weights(s, sink):
    m = jnp.maximum(jnp.max(s, axis=-1, keepdims=True), sink)
    e = jnp.exp(s - m)
    denom = jnp.sum(e, axis=-1, keepdims=True) + jnp.exp(sink - m)
    return e, 1.0 / denom


def _swa_prompt_kernel(sinks_ref, q_ref, k_prev_ref, k_cur_ref, v_prev_ref, v_cur_ref, o_ref):
    n = pl.program_id(1)
    blk = WINDOW
    rows = GROUP * blk
    qi = lax.broadcasted_iota(jnp.int32, (blk, 2 * blk), 0)
    kj = lax.broadcasted_iota(jnp.int32, (blk, 2 * blk), 1)
    d = qi + blk - kj
    mask = ((d >= 0) & (d < WINDOW) & ((kj >= blk) | (n > 0))) | (kj == 0)
    bias = jnp.where(mask, 0.0, -jnp.inf)[None]
    low_half = lax.broadcasted_iota(jnp.int32, (blk, V7X_LANES), 1) < HEAD_DIM
    g_of_row = lax.broadcasted_iota(jnp.int32, (rows, V7X_LANES), 0) // blk
    col = lax.broadcasted_iota(jnp.int32, (rows, V7X_LANES), 1)
    q_onehot = jnp.where((col == g_of_row) | (col == g_of_row + GROUP), 1.0, 0.0).astype(BF16)
    keep_low = jnp.where(low_half, 1.0, 0.0).astype(BF16)
    keep_high = jnp.where(low_half, 0.0, 1.0).astype(BF16)
    head_rows = 2 * V7X_SUBLANES
    is_slot0 = lax.broadcasted_iota(jnp.int32, (head_rows, V7X_LANES), 0) == 0
    kcol = lax.broadcasted_iota(jnp.int32, (1, V7X_LANES), 1)
    ones = jnp.ones((2 * blk, V7X_LANES), BF16)
    no_feat = jnp.zeros((2 * blk - head_rows, V7X_LANES), BF16)

    def without_slot0(prev_ref, cur_ref, kvh):
        head = jnp.where(is_slot0, 0.0, prev_ref[:head_rows, _lane_block(kvh)].astype(F32)).astype(BF16)
        return jnp.concatenate([head, prev_ref[head_rows:, _lane_block(kvh)], cur_ref[:, _lane_block(kvh)]], axis=0)

    for kvh in range(N_KV_HEADS):
        sink_row = jnp.zeros((1, V7X_LANES), F32)
        for g in range(GROUP):
            sink_row = jnp.where((kcol == g) | (kcol == g + GROUP), sinks_ref[kvh * GROUP + g], sink_row)
        sink_hi = sink_row.astype(BF16).astype(F32)
        sink_feat = jnp.where(kcol < GROUP, sink_hi, sink_row - sink_hi)
        feat = jnp.concatenate([jnp.where(is_slot0, sink_feat, 0.0).astype(BF16), no_feat], axis=0)
        k_ext = jnp.concatenate([without_slot0(k_prev_ref, k_cur_ref, kvh), feat], axis=1)
        v_ones = jnp.concatenate([without_slot0(v_prev_ref, v_cur_ref, kvh), ones], axis=1)
        q_heads = []
        for pair in range(GROUP // 2):
            q_pair = q_ref[:, _lane_block(kvh * (GROUP // 2) + pair)]
            q_heads += [q_pair * keep_low, q_pair * keep_high]
        q_ext = jnp.concatenate([jnp.concatenate(q_heads, axis=0), q_onehot], axis=1)
        s = lax.dot_general(q_ext, k_ext, (((1,), (1,)), ((), ())), preferred_element_type=F32)
        s = (s.reshape(GROUP, blk, 2 * blk) + bias).reshape(rows, 2 * blk)
        e = jnp.exp(s - jnp.max(s, axis=-1, keepdims=True)).astype(BF16)
        o_sum = jnp.dot(e, v_ones, preferred_element_type=F32)
        o = o_sum[:, :V7X_LANES] * (1.0 / o_sum[:, V7X_LANES:])
        for pair in range(GROUP // 2):
            even = o[(2 * pair) * blk:(2 * pair + 1) * blk, :]
            odd = o[(2 * pair + 1) * blk:(2 * pair + 2) * blk, :]
            o_ref[:, _lane_block(kvh * (GROUP // 2) + pair)] = jnp.where(low_half, even, odd).astype(o_ref.dtype)


def _swa_prompt(q, kv_pair, sinks):
    nb = SEQ // WINDOW
    pair_width = N_KV_HEADS * V7X_LANES

    def cur(part):
        return pl.BlockSpec((WINDOW, pair_width), lambda b, n: (b * nb + n, part))

    def prev(part):
        return pl.BlockSpec((WINDOW, pair_width), lambda b, n: (b * nb + jnp.maximum(n - 1, 0), part))

    vmem = (
        4 * _nbytes((WINDOW, Q_WIDTH), BF16)
        + 8 * _nbytes((WINDOW, pair_width), BF16)
        + 24 * _nbytes((GROUP * WINDOW, 2 * WINDOW), F32)
    )
    return pl.pallas_call(
        _swa_prompt_kernel,
        grid=(BATCH, nb),
        in_specs=[
            pl.BlockSpec(memory_space=pltpu.SMEM),
            pl.BlockSpec((WINDOW, Q_WIDTH), lambda b, n: (b * nb + n, 0)),
            prev(0), cur(0), prev(1), cur(1),
        ],
        out_specs=pl.BlockSpec((WINDOW, Q_WIDTH), lambda b, n: (b * nb + n, 0)),
        out_shape=jax.ShapeDtypeStruct((BATCH * SEQ, Q_WIDTH), BF16),
        compiler_params=_params(("parallel", "arbitrary"), vmem),
        name="swa_prompt",
    )(sinks, q, kv_pair, kv_pair, kv_pair, kv_pair)


def _swa_sample_kernel(sinks_ref, q_ref, kv_ref, ck_ref, cv_ref, o_ref, nk_ref, nv_ref, kall_scr, vall_scr):
    nb = ck_ref.shape[0]
    t_new = DEC_SEQ
    s_len = WINDOW + t_new
    rows = GROUP * t_new
    kall_scr[:, :WINDOW, :] = ck_ref[...]
    vall_scr[:, :WINDOW, :] = cv_ref[...]
    kall_scr[:, WINDOW:, :] = kv_ref[:, :KV_WIDTH].reshape(nb, t_new, KV_WIDTH)
    vall_scr[:, WINDOW:, :] = kv_ref[:, KV_WIDTH:].reshape(nb, t_new, KV_WIDTH)
    nk_ref[...] = kall_scr[:, t_new:, :]
    nv_ref[...] = vall_scr[:, t_new:, :]

    row = lax.broadcasted_iota(jnp.int32, (1, rows, s_len), 1)
    kj = lax.broadcasted_iota(jnp.int32, (1, rows, s_len), 2)
    d = row % t_new + WINDOW - kj
    mask = (d >= 0) & (d < WINDOW)
    g_of_row = lax.broadcasted_iota(jnp.int32, (1, rows, 1), 1) // t_new
    q3 = q_ref[...].reshape(nb, t_new, Q_WIDTH)
    for kvh in range(N_KV_HEADS):
        k = kall_scr[:, :, _head_cols(kvh)].astype(BF16)
        v = vall_scr[:, :, _head_cols(kvh)].astype(BF16)
        q = jnp.concatenate([q3[:, :, _head_cols(kvh * GROUP + g)] for g in range(GROUP)], axis=1).astype(BF16)
        s = jnp.einsum("bqd,bkd->bqk", q, k, preferred_element_type=F32)
        s = jnp.where(mask, s, -jnp.inf)
        e, inv = _sink_softmax_weights(s, _group_sinks(sinks_ref, kvh, g_of_row))
        o = jnp.einsum("bqk,bkd->bqd", e.astype(BF16), v, preferred_element_type=F32) * inv
        for g in range(GROUP):
            o_ref[:, _head_cols(kvh * GROUP + g)] = o[:, g * t_new:(g + 1) * t_new, :].reshape(nb * t_new, HEAD_DIM)


_SWA_SAMPLE_SEQS_PER_STEP = 8


def _swa_sample(q, kv, cache_k, cache_v, sinks):
    nb = _SWA_SAMPLE_SEQS_PER_STEP
    cache_spec = pl.BlockSpec((nb, WINDOW, KV_WIDTH), lambda b: (b, 0, 0))
    all_keys = (nb, WINDOW + DEC_SEQ, KV_WIDTH)
    vmem = 8 * _nbytes((nb, WINDOW, KV_WIDTH), F32) + 2 * _nbytes(all_keys, F32) + 16 * 1024 * 1024
    return pl.pallas_call(
        _swa_sample_kernel,
        grid=(DEC_BATCH // nb,),
        in_specs=[
            pl.BlockSpec(memory_space=pltpu.SMEM),
            pl.BlockSpec((nb * DEC_SEQ, Q_WIDTH), lambda b: (b, 0)),
            pl.BlockSpec((nb * DEC_SEQ, 2 * KV_WIDTH), lambda b: (b, 0)),
            cache_spec,
            cache_spec,
        ],
        out_specs=[pl.BlockSpec((nb * DEC_SEQ, Q_WIDTH), lambda b: (b, 0)), cache_spec, cache_spec],
        out_shape=[
            jax.ShapeDtypeStruct((N_SAMPLE_ROWS, Q_WIDTH), F32),
            jax.ShapeDtypeStruct((DEC_BATCH, WINDOW, KV_WIDTH), F32),
            jax.ShapeDtypeStruct((DEC_BATCH, WINDOW, KV_WIDTH), F32),
        ],
        scratch_shapes=[pltpu.VMEM(all_keys, F32), pltpu.VMEM(all_keys, F32)],
        compiler_params=_params(("parallel",), vmem),
        name="swa_sample",
    )(sinks, q, kv, cache_k, cache_v)


def _cross_kernel(x_ref, gain_ref, wq_ref, mk_ref, mv_ref, wo_ref, out_ref, *, seqs, rows):
    scale = MEM_HEAD_DIM ** -0.5
    x = x_ref[...]
    h = _rms(x, gain_ref[...]).astype(BF16)
    q = jnp.dot(h, wq_ref[...].astype(BF16), preferred_element_type=F32)
    heads = []
    for hd in range(N_MEM_HEADS):
        cs = slice(hd * MEM_HEAD_DIM, (hd + 1) * MEM_HEAD_DIM)
        qh = q[:, cs].reshape(seqs, rows, MEM_HEAD_DIM).astype(BF16)
        if mk_ref.ndim == 4:
            k = mk_ref[:, :, hd, :].astype(BF16)
            v = mv_ref[:, :, hd, :].astype(BF16)
        else:
            k = mk_ref[:, :, cs].astype(BF16)
            v = mv_ref[:, :, cs].astype(BF16)
        s = jnp.einsum("bqd,bkd->bqk", qh, k, preferred_element_type=F32) * scale
        e = jnp.exp(s - jnp.max(s, axis=-1, keepdims=True))
        inv = 1.0 / jnp.sum(e, axis=-1, keepdims=True)
        o = jnp.einsum("bqk,bkd->bqd", e.astype(BF16), v, preferred_element_type=F32) * inv
        heads.append(o.reshape(seqs * rows, MEM_HEAD_DIM))
    o_all = jnp.concatenate(heads, axis=-1).astype(BF16)
    out_ref[...] = x + jnp.dot(o_all, wo_ref[...].astype(BF16), preferred_element_type=F32)


def _cross_block(x, gain, wq, mem_k, mem_v, wo, layer, *, n_seq, seqs, rows):
    seq_len = x.shape[0] // n_seq
    nt = seq_len // rows if seqs == 1 else 1
    tile = seqs * rows
    x_spec = pl.BlockSpec((tile, D_MODEL), lambda g, i: (g * nt + i, 0))
    mem_tail = mem_k.shape[3:]
    mem_spec = pl.BlockSpec((None, seqs, N_MEM) + mem_tail, lambda g, i: (layer, g, 0) + (0,) * len(mem_tail))
    mem_pad = V7X_SUBLANES // N_MEM_HEADS if len(mem_tail) == 2 else 1
    vmem = (
        4 * _nbytes((tile, D_MODEL), F32)
        + 4 * mem_pad * _nbytes((seqs, N_MEM, MEM_WIDTH), F32)
        + 4 * _nbytes((D_MODEL, MEM_WIDTH), F32)
        + 2 * _nbytes((D_MODEL, MEM_WIDTH), BF16)
        + 4 * _nbytes((tile, D_MODEL), F32)
        + 8 * _nbytes((tile, N_MEM), F32)
    )
    return pl.pallas_call(
        functools.partial(_cross_kernel, seqs=seqs, rows=rows),
        grid=(n_seq // seqs, nt),
        in_specs=[
            x_spec,
            _row_param_spec(layer, D_MODEL),
            pl.BlockSpec((None, D_MODEL, MEM_WIDTH), lambda g, i: (layer, 0, 0)),
            mem_spec,
            mem_spec,
            pl.BlockSpec((None, MEM_WIDTH, D_MODEL), lambda g, i: (layer, 0, 0)),
        ],
        out_specs=x_spec,
        out_shape=jax.ShapeDtypeStruct(x.shape, F32),
        compiler_params=_params(("parallel", "arbitrary"), vmem),
        name="cross_block",
    )(x, _row_param(gain), wq, mem_k, mem_v, wo)


def _log_sigmoid(x):
    return jnp.minimum(x, 0.0) - jnp.log1p(jnp.exp(-jnp.abs(x)))


def _one_minus_exp2(y, exp_y):
    return jnp.tanh(-y) * (1.0 + exp_y * exp_y)


def _lru_block_cols(n):
    return slice(n * LRU_BLOCK_SIZE, (n + 1) * LRU_BLOCK_SIZE)


def _lru_gate_block(xc, n, wa_ref, ba_ref, wx_ref, bx_ref, lam_ref):
    cs = _lru_block_cols(n)
    xb = xc.astype(BF16)
    r = jax.nn.sigmoid(jnp.dot(xb, wa_ref[n], preferred_element_type=F32) + ba_ref[:, cs])
    ig = jax.nn.sigmoid(jnp.dot(xb, wx_ref[n], preferred_element_type=F32) + bx_ref[:, cs])
    log_a = r * (LRU_C * _log_sigmoid(lam_ref[:, cs]))
    a = jnp.exp(log_a)
    z = _one_minus_exp2(log_a, a)
    u = jnp.where(z > 0.0, z * lax.rsqrt(z), 0.0) * (ig * xc)
    return a, u


def _lru_gates(xc, wa_ref, ba_ref, wx_ref, bx_ref, lam_ref):
    parts = [_lru_gate_block(xc[:, _lru_block_cols(n)], n, wa_ref, ba_ref, wx_ref, bx_ref, lam_ref)
             for n in range(LRU_BLOCKS)]
    return jnp.concatenate([p[0] for p in parts], axis=-1), jnp.concatenate([p[1] for p in parts], axis=-1)


def _lru_prompt_kernel(xb_ref, gate_ref, x_prev_ref, wc_ref, bc_ref, wa_ref, ba_ref, wx_ref, bx_ref, lam_ref,
                       wo_ref, out_ref, hlast_ref, xext_scr, a_scr, u_scr, h_scr, y_scr, *, tiles_per_seq, n_tiles):
    s = pl.program_id(0)
    tm = xb_ref.shape[0]
    pad = V7X_SUBLANES

    @pl.when(s % tiles_per_seq == 0)
    def _():
        xext_scr[0:pad, :] = jnp.zeros((pad, LRU_WIDTH), F32)
        h_scr[...] = jnp.zeros((1, LRU_WIDTH), F32)

    @pl.when(s == 0)
    def _():
        y_scr[...] = jnp.zeros(y_scr.shape, BF16)

    xext_scr[pad:pad + tm, :] = xb_ref[...]
    for n in range(LRU_BLOCKS):
        cs = _lru_block_cols(n)
        out_ref[:, cs] = x_prev_ref[:, cs] + jnp.dot(y_scr[...], wo_ref[:, cs], preferred_element_type=F32)
        xc = bc_ref[:, cs] + wc_ref[CONV_WIDTH - 1:CONV_WIDTH, cs] * xext_scr[pad:pad + tm, cs]
        for k in range(1, CONV_WIDTH):
            xc = xc + wc_ref[CONV_WIDTH - 1 - k:CONV_WIDTH - k, cs] * xext_scr[pad - k:pad - k + tm, cs]
        a_scr[:, cs], u_scr[:, cs] = _lru_gate_block(xc, n, wa_ref, ba_ref, wx_ref, bx_ref, lam_ref)
    xext_scr[0:pad, :] = xext_scr[tm:tm + pad, :]

    def step(t, h):
        h = a_scr[pl.ds(t, 1), :] * h + u_scr[pl.ds(t, 1), :]
        u_scr[pl.ds(t, 1), :] = h
        return h

    h_scr[...] = lax.fori_loop(0, tm, step, h_scr[...], unroll=8)
    y_scr[...] = (u_scr[...] * gate_ref[...]).astype(BF16)

    @pl.when(s < n_tiles)
    def _():
        hlast_ref[...] = u_scr[tm - pad:tm, :]


def _lru_weight_specs(layer_idx):
    def const3(*_):
        return (layer_idx, 0, 0)

    def const4(*_):
        return (layer_idx, 0, 0, 0)

    gate_w = pl.BlockSpec((None, LRU_BLOCKS, LRU_BLOCK_SIZE, LRU_BLOCK_SIZE), const4)
    vec = _row_param_spec(layer_idx, LRU_WIDTH)
    return [
        pl.BlockSpec((None, CONV_WIDTH, LRU_WIDTH), const3),
        vec,
        gate_w,
        vec,
        gate_w,
        vec,
        vec,
        pl.BlockSpec((None, LRU_WIDTH, D_MODEL), const3),
    ]


def _lru_weight_args(w_conv, b_conv, w_gate_a, b_gate_a, w_gate_x, b_gate_x, lam, w_out):
    return (w_conv, _row_param(b_conv), w_gate_a, _row_param(b_gate_a), w_gate_x, _row_param(b_gate_x),
            _row_param(lam), w_out)


def _lru_prompt(xb, gate, x, lw, j, *, tm):
    tiles_per_seq = SEQ // tm
    n_tiles = BATCH * tiles_per_seq
    cur_spec = pl.BlockSpec((tm, LRU_WIDTH), lambda s: (jnp.minimum(s, n_tiles - 1), 0))
    prev_spec = pl.BlockSpec((tm, LRU_WIDTH), lambda s: (jnp.maximum(s - 1, 0), 0))
    vmem = (
        8 * _nbytes((tm, LRU_WIDTH), F32)
        + 3 * _nbytes((tm + V7X_SUBLANES, LRU_WIDTH), F32)
        + 2 * _nbytes((LRU_WIDTH, D_MODEL), BF16)
        + 4 * _nbytes((LRU_BLOCKS, LRU_BLOCK_SIZE, LRU_BLOCK_SIZE), BF16)
        + 10 * _nbytes((tm, LRU_WIDTH), F32)
    )
    return pl.pallas_call(
        functools.partial(_lru_prompt_kernel, tiles_per_seq=tiles_per_seq, n_tiles=n_tiles),
        grid=(n_tiles + 1,),
        in_specs=[cur_spec, cur_spec, prev_spec] + _lru_weight_specs(j),
        out_specs=[
            prev_spec,
            pl.BlockSpec((None, V7X_SUBLANES, LRU_WIDTH),
                         lambda s: (jnp.minimum(s, n_tiles - 1) // tiles_per_seq, 0, 0)),
        ],
        out_shape=[
            jax.ShapeDtypeStruct((BATCH * SEQ, D_MODEL), F32),
            jax.ShapeDtypeStruct((BATCH, V7X_SUBLANES, LRU_WIDTH), F32),
        ],
        scratch_shapes=[
            pltpu.VMEM((tm + V7X_SUBLANES, LRU_WIDTH), F32),
            pltpu.VMEM((tm, LRU_WIDTH), F32),
            pltpu.VMEM((tm, LRU_WIDTH), F32),
            pltpu.VMEM((1, LRU_WIDTH), F32),
            pltpu.VMEM((tm, LRU_WIDTH), BF16),
        ],
        compiler_params=_params(("arbitrary",), vmem),
        name="lru_prompt",
    )(xb, gate, x, *lw)


def _lru_sample_kernel(xb_ref, gate_ref, x_ref, cbuf_ref, h0_ref, wc_ref, bc_ref, wa_ref, ba_ref, wx_ref, bx_ref,
                       lam_ref, wo_ref, out_ref, hs_ref):
    rows = xb_ref.shape[0]
    t = lax.broadcasted_iota(jnp.int32, (rows, LRU_WIDTH), 0) % DEC_SEQ
    xb = xb_ref[...]
    cbuf = cbuf_ref[...]
    xc = bc_ref[...] + wc_ref[CONV_WIDTH - 1:CONV_WIDTH, :] * xb
    for k in range(1, CONV_WIDTH):
        shift_c = CONV_WIDTH - 1 - k
        from_buf = cbuf if shift_c == 0 else pltpu.roll(cbuf, rows - shift_c, 0)
        xk = jnp.where(t >= k, pltpu.roll(xb, k, 0), from_buf)
        xc = xc + wc_ref[CONV_WIDTH - 1 - k:CONV_WIDTH - k, :] * xk

    a, u = _lru_gates(xc, wa_ref, ba_ref, wx_ref, bx_ref, lam_ref)
    k = 1
    while k < DEC_SEQ:
        m = t >= k
        u = jnp.where(m, a * pltpu.roll(u, k, 0) + u, u)
        a = jnp.where(m, a * pltpu.roll(a, k, 0), a)
        k *= 2
    hs = u + a * h0_ref[...]
    hs_ref[...] = hs
    y = (hs * gate_ref[...]).astype(BF16)
    out_ref[...] = x_ref[...] + jnp.dot(y, wo_ref[...], preferred_element_type=F32)


def _lru_sample(xb, gate, x, cbuf_rows, h0_rows, lw, j):
    rows = N_SAMPLE_ROWS
    row_spec0 = pl.BlockSpec((rows, LRU_WIDTH), lambda i: (0, 0))
    vmem = 40 * _nbytes((rows, LRU_WIDTH), F32) + 2 * _nbytes((LRU_WIDTH, D_MODEL), BF16)
    return pl.pallas_call(
        _lru_sample_kernel,
        grid=(1,),
        in_specs=[row_spec0] * 5 + _lru_weight_specs(j),
        out_specs=[row_spec0, row_spec0],
        out_shape=[
            jax.ShapeDtypeStruct((rows, D_MODEL), F32),
            jax.ShapeDtypeStruct((rows, LRU_WIDTH), F32),
        ],
        compiler_params=_params(("arbitrary",), vmem),
        name="lru_sample",
    )(xb, gate, x, cbuf_rows, h0_rows, *lw)


_FFN_TM = 1024
_FFN_TF = 512
_FFN_ROW_CHUNK = 512


def kernel(x_prompt, x_sample, cache_swa_k, cache_swa_v, state_lru_h, state_lru_conv, cache_mem_k, cache_mem_v,
           mem_prompt, ln_ffn1, ffn1_w_in, ffn1_w_out, ln_mix, swa_w_qkv, swa_w_o, swa_sinks, lru_w_in,
           lru_w_conv, lru_b_conv, lru_w_gate_a, lru_b_gate_a, lru_w_gate_x, lru_b_gate_x, lru_lambda,
           lru_w_out, ln_cross, ln_mem, cross_w_q, cross_w_kv, cross_w_o, ln_ffn2, ffn2_w_in, ffn2_w_out,
           ln_final):
    bf = lambda w: w.astype(BF16)
    ffn_seq = []
    for layer in range(DEPTH):
        ffn_seq += [(ffn1_w_in, ffn1_w_out, layer), (ffn2_w_in, ffn2_w_out, layer)]
    ffn_w = (bf(ffn1_w_in[0]), bf(ffn1_w_out[0]))

    def ffn(x, x_extra, gain, layer, final_gain=None):
        nonlocal ffn_w
        nxt = ffn_seq[1] if len(ffn_seq) > 1 else None
        out = _ffn(x, x_extra, gain, layer, *ffn_w, tm=_FFN_TM, tf=_FFN_TF, row_chunk=_FFN_ROW_CHUNK,
                   final_gain=final_gain, cast_next=nxt)
        del ffn_seq[0]
        if nxt is not None:
            ffn_w = out[2:]
        return out[0], out[1]

    swa_w_qkv = bf(swa_w_qkv)
    lru_w_in, lru_w_out = bf(lru_w_in), bf(lru_w_out)
    lru_w_gate_a, lru_w_gate_x = bf(lru_w_gate_a), bf(lru_w_gate_x)
    cross_w_kv = bf(cross_w_kv)

    xp = x_prompt.reshape(N_PROMPT_ROWS, D_MODEL)
    xs = x_sample.reshape(N_SAMPLE_ROWS, D_MODEL)

    mem_rows = mem_prompt.reshape(BATCH * N_MEM, D_MODEL)
    mkv = [
        _norm_linear(mem_rows, ln_mem, layer, cross_w_kv, layer, tm=BATCH * N_MEM, tn=MEM_WIDTH, out_dtype=F32)
        for layer in range(DEPTH)
    ]
    mem_k_prompt = jnp.stack([m[:, :MEM_WIDTH] for m in mkv]).reshape(DEPTH, BATCH, N_MEM, MEM_WIDTH)
    mem_v_prompt = jnp.stack([m[:, MEM_WIDTH:] for m in mkv]).reshape(DEPTH, BATCH, N_MEM, MEM_WIDTH)

    cos_p, sin_p = _rope_tables(jnp.arange(SEQ, dtype=jnp.int32))
    cos_s, sin_s = _rope_tables(PAST_LEN + jnp.arange(N_SAMPLE_ROWS, dtype=jnp.int32) % DEC_SEQ)

    swa_k_p, swa_v_p, swa_k_s, swa_v_s = [], [], [], []
    lru_h_p, lru_conv_p, lru_h_s, lru_conv_s = [], [], [], []

    for layer in range(DEPTH):
        j = layer // 2
        xp, xs = ffn(xp, xs, ln_ffn1, layer)

        if layer % 2 == 0:
            qkv_tm = 1024
            q_p, kv_last, kv_pair = _qkv(xp, ln_mix, layer, swa_w_qkv, j, cos_p, sin_p, tm=qkv_tm, q_dtype=BF16,
                                         seq_len=SEQ)
            o_p = _swa_prompt(q_p, kv_pair, swa_sinks[j])
            xp = _linear_residual(o_p, swa_w_o, j, xp, tm=512)
            kv_tail = kv_last[:, qkv_tm - WINDOW:, :]
            swa_k_p.append(kv_tail[..., :KV_WIDTH].reshape(BATCH, WINDOW, N_KV_HEADS, HEAD_DIM))
            swa_v_p.append(kv_tail[..., KV_WIDTH:].reshape(BATCH, WINDOW, N_KV_HEADS, HEAD_DIM))

            q_s, kv_s = _qkv(xs, ln_mix, layer, swa_w_qkv, j, cos_s, sin_s, tm=N_SAMPLE_ROWS, q_dtype=F32)
            o_s, nk, nv = _swa_sample(
                q_s, kv_s,
                cache_swa_k[j].reshape(DEC_BATCH, WINDOW, KV_WIDTH),
                cache_swa_v[j].reshape(DEC_BATCH, WINDOW, KV_WIDTH),
                swa_sinks[j],
            )
            xs = _linear_residual(o_s, swa_w_o, j, xs, tm=N_SAMPLE_ROWS)
            swa_k_s.append(nk.reshape(DEC_BATCH, WINDOW, N_KV_HEADS, HEAD_DIM))
            swa_v_s.append(nv.reshape(DEC_BATCH, WINDOW, N_KV_HEADS, HEAD_DIM))
        else:
            lw = _lru_weight_args(lru_w_conv, lru_b_conv, lru_w_gate_a, lru_b_gate_a, lru_w_gate_x, lru_b_gate_x,
                                  lru_lambda, lru_w_out)
            xb_p, gate_p = _lru_in(xp, ln_mix, layer, lru_w_in, j, tm=512, tn=LRU_WIDTH)
            xp, h_last = _lru_prompt(xb_p, gate_p, xp, lw, j, tm=256)
            lru_h_p.append(h_last[:, V7X_SUBLANES - 1, :])
            lru_conv_p.append(xb_p.reshape(BATCH, SEQ, LRU_WIDTH)[:, SEQ - (CONV_WIDTH - 1):, :])

            xb_s, gate_s = _lru_in(xs, ln_mix, layer, lru_w_in, j, tm=N_SAMPLE_ROWS, tn=LRU_WIDTH)
            cbuf_rows = jnp.pad(
                state_lru_conv[j], ((0, 0), (0, DEC_SEQ - (CONV_WIDTH - 1)), (0, 0))
            ).reshape(N_SAMPLE_ROWS, LRU_WIDTH)
            h0_rows = jnp.repeat(state_lru_h[j], DEC_SEQ, axis=0)
            xs, hs_s = _lru_sample(xb_s, gate_s, xs, cbuf_rows, h0_rows, lw, j)
            lru_h_s.append(hs_s.reshape(DEC_BATCH, DEC_SEQ, LRU_WIDTH)[:, DEC_SEQ - 1, :])
            xp_tail = jnp.concatenate(
                [state_lru_conv[j], xb_s.reshape(DEC_BATCH, DEC_SEQ, LRU_WIDTH)], axis=1)
            lru_conv_s.append(xp_tail[:, -(CONV_WIDTH - 1):, :])

        xp = _cross_block(xp, ln_cross, cross_w_q, mem_k_prompt, mem_v_prompt, cross_w_o, layer,
                          n_seq=BATCH, seqs=1, rows=1024)
        xs = _cross_block(xs, ln_cross, cross_w_q, cache_mem_k, cache_mem_v, cross_w_o, layer,
                          n_seq=DEC_BATCH, seqs=8, rows=DEC_SEQ)

        fin = ln_final if layer == DEPTH - 1 else None
        xp, xs = ffn(xp, xs, ln_ffn2, layer, final_gain=fin)

    return (
        xp.reshape(BATCH, SEQ, D_MODEL),
        xs.reshape(DEC_BATCH, DEC_SEQ, D_MODEL),
        jnp.stack(swa_k_p), jnp.stack(swa_v_p), jnp.stack(swa_k_s), jnp.stack(swa_v_s),
        jnp.stack(lru_h_p), jnp.stack(lru_conv_p), jnp.stack(lru_h_s), jnp.stack(lru_conv_s),
        mem_k_prompt.reshape(DEPTH, BATCH, N_MEM, N_MEM_HEADS, MEM_HEAD_DIM),
        mem_v_prompt.reshape(DEPTH, BATCH, N_MEM, N_MEM_HEADS, MEM_HEAD_DIM),
    )
```

```python
import functools
import math

import jax
import jax.numpy as jnp
from jax import lax
from jax.experimental import pallas as pl
from jax.experimental.pallas import tpu as pltpu

F32 = jnp.float32
BF16 = jnp.bfloat16

D_MODEL = 2048
BATCH = 2
SEQ = 4096
DEPTH = 2
DEC_BATCH = 32
DEC_SEQ = 8
PAST_LEN = 16384
N_HEADS = 32
N_KV_HEADS = 8
HEAD_DIM = 64
GROUP = N_HEADS // N_KV_HEADS
WINDOW = 128
ROPE_THETA = 10000.0
LRU_WIDTH = D_MODEL
LRU_BLOCKS = 8
LRU_BLOCK_SIZE = LRU_WIDTH // LRU_BLOCKS
CONV_WIDTH = 4
LRU_C = 8.0
D_FF = 5632
N_MEM = 256
N_MEM_HEADS = 4
MEM_HEAD_DIM = 128
MEM_WIDTH = N_MEM_HEADS * MEM_HEAD_DIM
NORM_EPS = 1e-6
Q_WIDTH = N_HEADS * HEAD_DIM
KV_WIDTH = N_KV_HEADS * HEAD_DIM

V7X_LANES = 128
V7X_SUBLANES = 8
V7X_VMEM_BYTES = 64 * 1024 * 1024
V7X_VMEM_USABLE_BYTES = V7X_VMEM_BYTES - 2 * 1024 * 1024

N_PROMPT_ROWS = BATCH * SEQ
N_SAMPLE_ROWS = DEC_BATCH * DEC_SEQ


def _params(semantics, vmem_bytes):
    return pltpu.CompilerParams(
        dimension_semantics=semantics,
        vmem_limit_bytes=int(min(max(vmem_bytes, 32 * 1024 * 1024), V7X_VMEM_USABLE_BYTES)),
    )


def _nbytes(shape, dtype):
    return math.prod(shape) * jnp.dtype(dtype).itemsize


def _rms(x, gain):
    ms = jnp.mean(x * x, axis=-1, keepdims=True)
    return x * lax.rsqrt(ms + NORM_EPS) * gain


def _lane_block(c):
    return slice(c * V7X_LANES, (c + 1) * V7X_LANES)


def _row_param(p):
    return p.reshape(p.shape[0], 1, p.shape[-1])


def _row_param_spec(layer, width):
    return pl.BlockSpec((None, 1, width), lambda *_: (layer, 0, 0))


def _for_chunks(n_rows, chunk, body):
    n_chunks = n_rows // chunk
    if n_chunks == 1:
        body(pl.ds(0, n_rows))
        return

    def step(c, carry):
        body(pl.ds(pl.multiple_of(c * chunk, chunk), chunk))
        return carry

    lax.fori_loop(0, n_chunks, step, 0)


def _ffn_kernel(x_ref, gain_ref, wg_ref, wu_ref, wo_ref, xe_ref, *rest, row_chunk, final_norm, cast_next):
    rest = list(rest)
    gfin_ref = rest.pop(0) if final_norm else None
    next_in_ref, next_out_ref = (rest.pop(0), rest.pop(0)) if cast_next else (None, None)
    o_ref, oe_ref = rest.pop(0), rest.pop(0)
    next_in_bf_ref, next_out_bf_ref = (rest.pop(0), rest.pop(0)) if cast_next else (None, None)
    h_scr, he_scr = rest
    f = pl.program_id(1)

    def run(x_ref, o_ref, h_scr, chunk, side_job=None):
        n_rows = x_ref.shape[0]

        @pl.when(f == 0)
        def _():
            def prologue(rows):
                h_scr[rows, :] = _rms(x_ref[rows, :], gain_ref[...]).astype(BF16)
                o_ref[rows, :] = jnp.zeros((chunk, D_MODEL), F32)

            _for_chunks(n_rows, chunk, prologue)

        chunks = [pl.ds(c * chunk, chunk) for c in range(n_rows // chunk)]
        acts = []
        for rows in chunks:
            h = h_scr[rows, :]
            g = jnp.dot(h, wg_ref[...], preferred_element_type=F32)
            u = jnp.dot(h, wu_ref[...], preferred_element_type=F32)
            acts.append((g * jax.nn.sigmoid(g) * u).astype(BF16))
        for rows, a in zip(chunks, acts):
            o_ref[rows, :] += jnp.dot(a, wo_ref[...], preferred_element_type=F32)
        if side_job is not None:
            side_job()

        @pl.when(f == pl.num_programs(1) - 1)
        def _():
            def epilogue(rows):
                y = x_ref[rows, :] + 0.5 * o_ref[rows, :]
                if final_norm:
                    y = _rms(y, gfin_ref[...])
                o_ref[rows, :] = y

            _for_chunks(n_rows, chunk, epilogue)

    def round_next_weights():
        next_in_bf_ref[...] = next_in_ref[...].astype(BF16)
        next_out_bf_ref[...] = next_out_ref[...].astype(BF16)

    run(x_ref, o_ref, h_scr, row_chunk, side_job=round_next_weights if cast_next else None)

    @pl.when(pl.program_id(0) == 0)
    def _():
        run(xe_ref, oe_ref, he_scr, xe_ref.shape[0])


def _ffn(x, x_extra, gain, layer, w_in, w_out, *, tm, tf, row_chunk, final_gain=None, cast_next=None):
    m = x.shape[0]
    me = x_extra.shape[0]
    n_tiles = m // tm
    nf = D_FF // tf
    final_norm = final_gain is not None
    row_spec = pl.BlockSpec((tm, D_MODEL), lambda i, f: (i, 0))
    extra_spec = pl.BlockSpec((me, D_MODEL), lambda i, f: (0, 0))
    in_specs = [
        row_spec,
        _row_param_spec(layer, D_MODEL),
        pl.BlockSpec((D_MODEL, tf), lambda i, f: (0, f)),
        pl.BlockSpec((D_MODEL, tf), lambda i, f: (0, f + nf)),
        pl.BlockSpec((tf, D_MODEL), lambda i, f: (f, 0)),
        pl.BlockSpec((me, D_MODEL), lambda i, f: (0, 0), pipeline_mode=pl.Buffered(1)),
    ]
    args = [x, _row_param(gain), w_in, w_in, w_out, x_extra]
    out_specs = [row_spec, extra_spec]
    out_shape = [jax.ShapeDtypeStruct((m, D_MODEL), F32), jax.ShapeDtypeStruct((me, D_MODEL), F32)]
    if final_norm:
        in_specs.append(_row_param_spec(0, D_MODEL))
        args.append(final_gain.reshape(1, 1, D_MODEL))
    vmem = (
        4 * _nbytes((tm, D_MODEL), F32)
        + _nbytes((tm, D_MODEL), BF16)
        + 4 * _nbytes((me, D_MODEL), F32)
        + _nbytes((me, D_MODEL), BF16)
        + 2 * 3 * _nbytes((D_MODEL, tf), BF16)
        + 4 * _nbytes((row_chunk, tf), F32)
        + 2 * _nbytes((row_chunk, D_MODEL), F32)
    )
    if cast_next is not None:
        nw_in, nw_out, nl = cast_next
        in_blk = (D_MODEL // n_tiles, 2 * D_FF // nf)
        out_blk = (D_FF // nf, D_MODEL // n_tiles)
        in_specs += [
            pl.BlockSpec((None,) + in_blk, lambda i, f: (nl, i, f)),
            pl.BlockSpec((None,) + out_blk, lambda i, f: (nl, f, i)),
        ]
        args += [nw_in, nw_out]
        out_specs += [pl.BlockSpec(in_blk, lambda i, f: (i, f)), pl.BlockSpec(out_blk, lambda i, f: (f, i))]
        out_shape += [jax.ShapeDtypeStruct(nw_in.shape[1:], BF16), jax.ShapeDtypeStruct(nw_out.shape[1:], BF16)]
        vmem += 2 * (_nbytes(in_blk, F32) + _nbytes(in_blk, BF16) + _nbytes(out_blk, F32) + _nbytes(out_blk, BF16))
    return pl.pallas_call(
        functools.partial(_ffn_kernel, row_chunk=row_chunk, final_norm=final_norm, cast_next=cast_next is not None),
        grid=(n_tiles, nf),
        in_specs=in_specs,
        out_specs=out_specs,
        out_shape=out_shape,
        scratch_shapes=[pltpu.VMEM((tm, D_MODEL), BF16), pltpu.VMEM((me, D_MODEL), BF16)],
        compiler_params=_params(("arbitrary", "arbitrary"), vmem),
        name="ffn",
    )(*args)


def _gelu_tanh(y):
    return 0.5 * y * (1.0 + jnp.tanh(math.sqrt(2.0 / math.pi) * (y + 0.044715 * (y * y * y))))


def _norm_linear_kernel(x_ref, gain_ref, w_ref, o_ref, h_scr):
    @pl.when(pl.program_id(1) == 0)
    def _():
        h_scr[...] = _rms(x_ref[...], gain_ref[...]).astype(BF16)

    o_ref[...] = jnp.dot(h_scr[...], w_ref[...], preferred_element_type=F32).astype(o_ref.dtype)


def _lru_in_kernel(x_ref, gain_ref, wx_ref, wg_ref, xb_ref, gate_ref, h_scr):
    @pl.when(pl.program_id(1) == 0)
    def _():
        h_scr[...] = _rms(x_ref[...], gain_ref[...]).astype(BF16)

    h = h_scr[...]
    gate_ref[...] = _gelu_tanh(jnp.dot(h, wg_ref[...], preferred_element_type=F32))
    xb_ref[...] = jnp.dot(h, wx_ref[...], preferred_element_type=F32)


def _lru_in(x, gain, gain_layer, w, w_layer, *, tm, tn):
    m = x.shape[0]
    n_col = LRU_WIDTH // tn
    vmem = (
        2 * _nbytes((tm, D_MODEL), F32)
        + _nbytes((tm, D_MODEL), BF16)
        + 4 * _nbytes((D_MODEL, tn), BF16)
        + 10 * _nbytes((tm, tn), F32)
    )
    out_spec = pl.BlockSpec((tm, tn), lambda i, j: (i, j))
    return pl.pallas_call(
        _lru_in_kernel,
        grid=(m // tm, n_col),
        in_specs=[
            pl.BlockSpec((tm, D_MODEL), lambda i, j: (i, 0)),
            _row_param_spec(gain_layer, D_MODEL),
            pl.BlockSpec((None, D_MODEL, tn), lambda i, j: (w_layer, 0, j)),
            pl.BlockSpec((None, D_MODEL, tn), lambda i, j: (w_layer, 0, n_col + j)),
        ],
        out_specs=[out_spec, out_spec],
        out_shape=[jax.ShapeDtypeStruct((m, LRU_WIDTH), F32)] * 2,
        scratch_shapes=[pltpu.VMEM((tm, D_MODEL), BF16)],
        compiler_params=_params(("parallel", "arbitrary"), vmem),
        name="lru_in",
    )(x, _row_param(gain), w, w)


def _norm_linear(x, gain, gain_layer, w, w_layer, *, tm, tn, out_dtype):
    m = x.shape[0]
    n = w.shape[-1]
    vmem = (
        2 * _nbytes((tm, D_MODEL), F32)
        + _nbytes((tm, D_MODEL), BF16)
        + 2 * _nbytes((D_MODEL, tn), BF16)
        + 4 * _nbytes((tm, tn), F32)
    )
    return pl.pallas_call(
        _norm_linear_kernel,
        grid=(m // tm, n // tn),
        in_specs=[
            pl.BlockSpec((tm, D_MODEL), lambda i, j: (i, 0)),
            _row_param_spec(gain_layer, D_MODEL),
            pl.BlockSpec((None, D_MODEL, tn), lambda i, j: (w_layer, 0, j)),
        ],
        out_specs=pl.BlockSpec((tm, tn), lambda i, j: (i, j)),
        out_shape=jax.ShapeDtypeStruct((m, n), out_dtype),
        scratch_shapes=[pltpu.VMEM((tm, D_MODEL), BF16)],
        compiler_params=_params(("parallel", "arbitrary"), vmem),
        name="norm_linear",
    )(x, _row_param(gain), w)


_QKV_STEPS = 2
_Q_TN = Q_WIDTH // _QKV_STEPS
_ATTN_SCALE = HEAD_DIM ** -0.5
assert math.frexp(_ATTN_SCALE)[0] == 0.5
assert 2 * KV_WIDTH // _QKV_STEPS == KV_WIDTH


def _rope_tables(positions):
    half = HEAD_DIM // 2
    inv = ROPE_THETA ** (-jnp.arange(half, dtype=F32) * (2.0 / HEAD_DIM))
    ang = positions.astype(F32)[:, None] * inv[None, :]
    cos = jnp.cos(ang)
    sin = jnp.sin(ang)
    reps = V7X_LANES // HEAD_DIM
    cos_t = jnp.tile(jnp.concatenate([cos, cos], axis=-1), (1, reps))
    sin_t = jnp.tile(jnp.concatenate([-sin, sin], axis=-1), (1, reps))
    return jnp.stack([cos_t, jnp.ones_like(cos_t)]), jnp.stack([sin_t, jnp.zeros_like(sin_t)])


def _qkv_kernel(x_ref, gain_ref, wq_ref, wkv_ref, cos_ref, sin_ref, cos_kv_ref, sin_kv_ref, q_ref, kv_ref,
                pair_ref=None):
    tm = x_ref.shape[0]
    lane = lax.broadcasted_iota(jnp.int32, (tm, V7X_LANES), 1)
    first_half = (lane % HEAD_DIM) < (HEAD_DIM // 2)
    low_half = lane < HEAD_DIM

    def rope(yc, cos, sin):
        rot = jnp.where(
            first_half,
            pltpu.roll(yc, V7X_LANES - HEAD_DIM // 2, 1),
            pltpu.roll(yc, HEAD_DIM // 2, 1),
        )
        return yc * cos + rot * sin

    h = _rms(x_ref[...], gain_ref[...]).astype(BF16)
    yq = jnp.dot(h, wq_ref[...], preferred_element_type=F32)
    cos, sin = cos_ref[...], sin_ref[...]
    for c in range(_Q_TN // V7X_LANES):
        q_ref[:, _lane_block(c)] = (rope(yq[:, _lane_block(c)], cos, sin) * _ATTN_SCALE).astype(q_ref.dtype)

    ykv = jnp.dot(h, wkv_ref[...], preferred_element_type=F32)
    cos_kv, sin_kv = cos_kv_ref[...], sin_kv_ref[...]
    heads_per_block = V7X_LANES // HEAD_DIM
    for c in range(KV_WIDTH // V7X_LANES):
        r = rope(ykv[:, _lane_block(c)], cos_kv, sin_kv)
        kv_ref[:, _lane_block(c)] = r
        if pair_ref is not None:
            swapped = pltpu.roll(r, HEAD_DIM, 1)
            pair_ref[:, _lane_block(heads_per_block * c)] = jnp.where(low_half, r, swapped).astype(BF16)
            pair_ref[:, _lane_block(heads_per_block * c + 1)] = jnp.where(low_half, swapped, r).astype(BF16)


def _qkv(x, gain, gain_layer, w, w_layer, cos_t, sin_t, *, tm, q_dtype, seq_len=None):
    m = x.shape[0]
    n_tab = cos_t.shape[1] // tm
    paired_kv = seq_len is not None
    pair_width = N_KV_HEADS * V7X_LANES
    vmem = (
        2 * _nbytes((tm, D_MODEL), F32)
        + _nbytes((tm, D_MODEL), BF16)
        + 2 * _nbytes((D_MODEL, _Q_TN + KV_WIDTH), BF16)
        + 8 * _nbytes((tm, _Q_TN + KV_WIDTH), F32)
        + 4 * _nbytes((tm, pair_width), BF16)
    )
    out_specs = [pl.BlockSpec((tm, _Q_TN), lambda j, i: (i, j))]
    out_shape = [jax.ShapeDtypeStruct((m, Q_WIDTH), q_dtype)]
    if paired_kv:
        tiles_per_seq = seq_len // tm
        out_specs.append(pl.BlockSpec((None, tm, KV_WIDTH), lambda j, i: (i // tiles_per_seq, 0, j)))
        out_shape.append(jax.ShapeDtypeStruct((m // seq_len, tm, 2 * KV_WIDTH), F32))
        out_specs.append(pl.BlockSpec((tm, pair_width), lambda j, i: (i, j)))
        out_shape.append(jax.ShapeDtypeStruct((m, 2 * pair_width), BF16))
    else:
        out_specs.append(pl.BlockSpec((tm, KV_WIDTH), lambda j, i: (i, j)))
        out_shape.append(jax.ShapeDtypeStruct((m, 2 * KV_WIDTH), F32))
    rot_spec = pl.BlockSpec((None, tm, V7X_LANES), lambda j, i: (0, i % n_tab, 0))
    rot_or_id_spec = pl.BlockSpec((None, tm, V7X_LANES), lambda j, i: (j, i % n_tab, 0))
    return pl.pallas_call(
        _qkv_kernel,
        grid=(_QKV_STEPS, m // tm),
        in_specs=[
            pl.BlockSpec((tm, D_MODEL), lambda j, i: (i, 0)),
            _row_param_spec(gain_layer, D_MODEL),
            pl.BlockSpec((None, D_MODEL, _Q_TN), lambda j, i: (w_layer, 0, j)),
            pl.BlockSpec((None, D_MODEL, KV_WIDTH), lambda j, i: (w_layer, 0, Q_WIDTH // KV_WIDTH + j)),
            rot_spec, rot_spec, rot_or_id_spec, rot_or_id_spec,
        ],
        out_specs=out_specs,
        out_shape=out_shape,
        compiler_params=_params(("arbitrary", "arbitrary"), vmem),
        name="qkv_rope",
    )(x, _row_param(gain), w, w, cos_t, sin_t, cos_t, sin_t)


def _linear_residual_kernel(o_ref, w_ref, x_ref, out_ref, w_scr):
    @pl.when(pl.program_id(0) == 0)
    def _():
        w_scr[...] = w_ref[...].astype(BF16)

    out_ref[...] = x_ref[...] + jnp.dot(o_ref[...].astype(BF16), w_scr[...], preferred_element_type=F32)


def _linear_residual(o, w, layer, x, *, tm):
    m, k = o.shape
    vmem = (
        2 * _nbytes((tm, k), o.dtype)
        + 2 * _nbytes((k, D_MODEL), w.dtype)
        + _nbytes((k, D_MODEL), BF16)
        + 6 * _nbytes((tm, D_MODEL), F32)
    )
    return pl.pallas_call(
        _linear_residual_kernel,
        grid=(m // tm,),
        in_specs=[
            pl.BlockSpec((tm, k), lambda i: (i, 0)),
            pl.BlockSpec((None, k, D_MODEL), lambda i: (layer, 0, 0)),
            pl.BlockSpec((tm, D_MODEL), lambda i: (i, 0)),
        ],
        out_specs=pl.BlockSpec((tm, D_MODEL), lambda i: (i, 0)),
        out_shape=jax.ShapeDtypeStruct((m, D_MODEL), F32),
        scratch_shapes=[pltpu.VMEM((k, D_MODEL), BF16)],
        compiler_params=_params(("arbitrary",), vmem),
        name="linear_residual",
    )(o, w, x)


def _head_cols(h):
    return slice(h * HEAD_DIM, (h + 1) * HEAD_DIM)


def _group_sinks(sinks_ref, kvh, g_of_row):
    sink = jnp.full(g_of_row.shape, sinks_ref[kvh * GROUP], F32)
    for g in range(1, GROUP):
        sink = jnp.where(g_of_row == g, sinks_ref[kvh * GROUP + g], sink)
    return sink


def _sink_softmax_weights(s, sink):
    m = jnp.maximum(jnp.max(s, axis=-1, keepdims=True), sink)
    e = jnp.exp(s - m)
    denom = jnp.sum(e, axis=-1, keepdims=True) + jnp.exp(sink - m)
    return e, 1.0 / denom


def _swa_prompt_kernel(sinks_ref, q_ref, k_prev_ref, k_cur_ref, v_prev_ref, v_cur_ref, o_ref):
    n = pl.program_id(1)
    blk = WINDOW
    rows = GROUP * blk
    qi = lax.broadcasted_iota(jnp.int32, (blk, 2 * blk), 0)
    kj = lax.broadcasted_iota(jnp.int32, (blk, 2 * blk), 1)
    d = qi + blk - kj
    mask = ((d >= 0) & (d < WINDOW) & ((kj >= blk) | (n > 0))) | (kj == 0)
    bias = jnp.where(mask, 0.0, -jnp.inf)[None]
    low_half = lax.broadcasted_iota(jnp.int32, (blk, V7X_LANES), 1) < HEAD_DIM
    g_of_row = lax.broadcasted_iota(jnp.int32, (rows, V7X_LANES), 0) // blk
    col = lax.broadcasted_iota(jnp.int32, (rows, V7X_LANES), 1)
    q_onehot = jnp.where((col == g_of_row) | (col == g_of_row + GROUP), 1.0, 0.0).astype(BF16)
    keep_low = jnp.where(low_half, 1.0, 0.0).astype(BF16)
    keep_high = jnp.where(low_half, 0.0, 1.0).astype(BF16)
    head_rows = 2 * V7X_SUBLANES
    is_slot0 = lax.broadcasted_iota(jnp.int32, (head_rows, V7X_LANES), 0) == 0
    kcol = lax.broadcasted_iota(jnp.int32, (1, V7X_LANES), 1)
    ones = jnp.ones((2 * blk, V7X_LANES), BF16)
    no_feat = jnp.zeros((2 * blk - head_rows, V7X_LANES), BF16)

    def without_slot0(prev_ref, cur_ref, kvh):
        head = jnp.where(is_slot0, 0.0, prev_ref[:head_rows, _lane_block(kvh)].astype(F32)).astype(BF16)
        return jnp.concatenate([head, prev_ref[head_rows:, _lane_block(kvh)], cur_ref[:, _lane_block(kvh)]], axis=0)

    for kvh in range(N_KV_HEADS):
        sink_row = jnp.zeros((1, V7X_LANES), F32)
        for g in range(GROUP):
            sink_row = jnp.where((kcol == g) | (kcol == g + GROUP), sinks_ref[kvh * GROUP + g], sink_row)
        sink_hi = sink_row.astype(BF16).astype(F32)
        sink_feat = jnp.where(kcol < GROUP, sink_hi, sink_row - sink_hi)
        feat = jnp.concatenate([jnp.where(is_slot0, sink_feat, 0.0).astype(BF16), no_feat], axis=0)
        k_ext = jnp.concatenate([without_slot0(k_prev_ref, k_cur_ref, kvh), feat], axis=1)
        v_ones = jnp.concatenate([without_slot0(v_prev_ref, v_cur_ref, kvh), ones], axis=1)
        q_heads = []
        for pair in range(GROUP // 2):
            q_pair = q_ref[:, _lane_block(kvh * (GROUP // 2) + pair)]
            q_heads += [q_pair * keep_low, q_pair * keep_high]
        q_ext = jnp.concatenate([jnp.concatenate(q_heads, axis=0), q_onehot], axis=1)
        s = lax.dot_general(q_ext, k_ext, (((1,), (1,)), ((), ())), preferred_element_type=F32)
        s = (s.reshape(GROUP, blk, 2 * blk) + bias).reshape(rows, 2 * blk)
        e = jnp.exp(s - jnp.max(s, axis=-1, keepdims=True)).astype(BF16)
        o_sum = jnp.dot(e, v_ones, preferred_element_type=F32)
        o = o_sum[:, :V7X_LANES] * (1.0 / o_sum[:, V7X_LANES:])
        for pair in range(GROUP // 2):
            even = o[(2 * pair) * blk:(2 * pair + 1) * blk, :]
            odd = o[(2 * pair + 1) * blk:(2 * pair + 2) * blk, :]
            o_ref[:, _lane_block(kvh * (GROUP // 2) + pair)] = jnp.where(low_half, even, odd).astype(o_ref.dtype)


def _swa_prompt(q, kv_pair, sinks):
    nb = SEQ // WINDOW
    pair_width = N_KV_HEADS * V7X_LANES

    def cur(part):
        return pl.BlockSpec((WINDOW, pair_width), lambda b, n: (b * nb + n, part))

    def prev(part):
        return pl.BlockSpec((WINDOW, pair_width), lambda b, n: (b * nb + jnp.maximum(n - 1, 0), part))

    vmem = (
        4 * _nbytes((WINDOW, Q_WIDTH), BF16)
        + 8 * _nbytes((WINDOW, pair_width), BF16)
        + 24 * _nbytes((GROUP * WINDOW, 2 * WINDOW), F32)
    )
    return pl.pallas_call(
        _swa_prompt_kernel,
        grid=(BATCH, nb),
        in_specs=[
            pl.BlockSpec(memory_space=pltpu.SMEM),
            pl.BlockSpec((WINDOW, Q_WIDTH), lambda b, n: (b * nb + n, 0)),
            prev(0), cur(0), prev(1), cur(1),
        ],
        out_specs=pl.BlockSpec((WINDOW, Q_WIDTH), lambda b, n: (b * nb + n, 0)),
        out_shape=jax.ShapeDtypeStruct((BATCH * SEQ, Q_WIDTH), BF16),
        compiler_params=_params(("parallel", "arbitrary"), vmem),
        name="swa_prompt",
    )(sinks, q, kv_pair, kv_pair, kv_pair, kv_pair)


def _swa_sample_kernel(sinks_ref, q_ref, kv_ref, ck_ref, cv_ref, o_ref, nk_ref, nv_ref, kall_scr, vall_scr):
    nb = ck_ref.shape[0]
    t_new = DEC_SEQ
    s_len = WINDOW + t_new
    rows = GROUP * t_new
    kall_scr[:, :WINDOW, :] = ck_ref[...]
    vall_scr[:, :WINDOW, :] = cv_ref[...]
    kall_scr[:, WINDOW:, :] = kv_ref[:, :KV_WIDTH].reshape(nb, t_new, KV_WIDTH)
    vall_scr[:, WINDOW:, :] = kv_ref[:, KV_WIDTH:].reshape(nb, t_new, KV_WIDTH)
    nk_ref[...] = kall_scr[:, t_new:, :]
    nv_ref[...] = vall_scr[:, t_new:, :]

    row = lax.broadcasted_iota(jnp.int32, (1, rows, s_len), 1)
    kj = lax.broadcasted_iota(jnp.int32, (1, rows, s_len), 2)
    d = row % t_new + WINDOW - kj
    mask = (d >= 0) & (d < WINDOW)
    g_of_row = lax.broadcasted_iota(jnp.int32, (1, rows, 1), 1) // t_new
    q3 = q_ref[...].reshape(nb, t_new, Q_WIDTH)
    for kvh in range(N_KV_HEADS):
        k = kall_scr[:, :, _head_cols(kvh)].astype(BF16)
        v = vall_scr[:, :, _head_cols(kvh)].astype(BF16)
        q = jnp.concatenate([q3[:, :, _head_cols(kvh * GROUP + g)] for g in range(GROUP)], axis=1).astype(BF16)
        s = jnp.einsum("bqd,bkd->bqk", q, k, preferred_element_type=F32)
        s = jnp.where(mask, s, -jnp.inf)
        e, inv = _sink_softmax_weights(s, _group_sinks(sinks_ref, kvh, g_of_row))
        o = jnp.einsum("bqk,bkd->bqd", e.astype(BF16), v, preferred_element_type=F32) * inv
        for g in range(GROUP):
            o_ref[:, _head_cols(kvh * GROUP + g)] = o[:, g * t_new:(g + 1) * t_new, :].reshape(nb * t_new, HEAD_DIM)


_SWA_SAMPLE_SEQS_PER_STEP = 8


def _swa_sample(q, kv, cache_k, cache_v, sinks):
    nb = _SWA_SAMPLE_SEQS_PER_STEP
    cache_spec = pl.BlockSpec((nb, WINDOW, KV_WIDTH), lambda b: (b, 0, 0))
    all_keys = (nb, WINDOW + DEC_SEQ, KV_WIDTH)
    vmem = 8 * _nbytes((nb, WINDOW, KV_WIDTH), F32) + 2 * _nbytes(all_keys, F32) + 16 * 1024 * 1024
    return pl.pallas_call(
        _swa_sample_kernel,
        grid=(DEC_BATCH // nb,),
        in_specs=[
            pl.BlockSpec(memory_space=pltpu.SMEM),
            pl.BlockSpec((nb * DEC_SEQ, Q_WIDTH), lambda b: (b, 0)),
            pl.BlockSpec((nb * DEC_SEQ, 2 * KV_WIDTH), lambda b: (b, 0)),
            cache_spec,
            cache_spec,
        ],
        out_specs=[pl.BlockSpec((nb * DEC_SEQ, Q_WIDTH), lambda b: (b, 0)), cache_spec, cache_spec],
        out_shape=[
            jax.ShapeDtypeStruct((N_SAMPLE_ROWS, Q_WIDTH), F32),
            jax.ShapeDtypeStruct((DEC_BATCH, WINDOW, KV_WIDTH), F32),
            jax.ShapeDtypeStruct((DEC_BATCH, WINDOW, KV_WIDTH), F32),
        ],
        scratch_shapes=[pltpu.VMEM(all_keys, F32), pltpu.VMEM(all_keys, F32)],
        compiler_params=_params(("parallel",), vmem),
        name="swa_sample",
    )(sinks, q, kv, cache_k, cache_v)


def _cross_kernel(x_ref, gain_ref, wq_ref, mk_ref, mv_ref, wo_ref, out_ref, *, seqs, rows):
    scale = MEM_HEAD_DIM ** -0.5
    x = x_ref[...]
    h = _rms(x, gain_ref[...]).astype(BF16)
    q = jnp.dot(h, wq_ref[...].astype(BF16), preferred_element_type=F32)
    heads = []
    for hd in range(N_MEM_HEADS):
        cs = slice(hd * MEM_HEAD_DIM, (hd + 1) * MEM_HEAD_DIM)
        qh = q[:, cs].reshape(seqs, rows, MEM_HEAD_DIM).astype(BF16)
        if mk_ref.ndim == 4:
            k = mk_ref[:, :, hd, :].astype(BF16)
            v = mv_ref[:, :, hd, :].astype(BF16)
        else:
            k = mk_ref[:, :, cs].astype(BF16)
            v = mv_ref[:, :, cs].astype(BF16)
        s = jnp.einsum("bqd,bkd->bqk", qh, k, preferred_element_type=F32) * scale
        e = jnp.exp(s - jnp.max(s, axis=-1, keepdims=True))
        inv = 1.0 / jnp.sum(e, axis=-1, keepdims=True)
        o = jnp.einsum("bqk,bkd->bqd", e.astype(BF16), v, preferred_element_type=F32) * inv
        heads.append(o.reshape(seqs * rows, MEM_HEAD_DIM))
    o_all = jnp.concatenate(heads, axis=-1).astype(BF16)
    out_ref[...] = x + jnp.dot(o_all, wo_ref[...].astype(BF16), preferred_element_type=F32)


def _cross_block(x, gain, wq, mem_k, mem_v, wo, layer, *, n_seq, seqs, rows):
    seq_len = x.shape[0] // n_seq
    nt = seq_len // rows if seqs == 1 else 1
    tile = seqs * rows
    x_spec = pl.BlockSpec((tile, D_MODEL), lambda g, i: (g * nt + i, 0))
    mem_tail = mem_k.shape[3:]
    mem_spec = pl.BlockSpec((None, seqs, N_MEM) + mem_tail, lambda g, i: (layer, g, 0) + (0,) * len(mem_tail))
    mem_pad = V7X_SUBLANES // N_MEM_HEADS if len(mem_tail) == 2 else 1
    vmem = (
        4 * _nbytes((tile, D_MODEL), F32)
        + 4 * mem_pad * _nbytes((seqs, N_MEM, MEM_WIDTH), F32)
        + 4 * _nbytes((D_MODEL, MEM_WIDTH), F32)
        + 2 * _nbytes((D_MODEL, MEM_WIDTH), BF16)
        + 4 * _nbytes((tile, D_MODEL), F32)
        + 8 * _nbytes((tile, N_MEM), F32)
    )
    return pl.pallas_call(
        functools.partial(_cross_kernel, seqs=seqs, rows=rows),
        grid=(n_seq // seqs, nt),
        in_specs=[
            x_spec,
            _row_param_spec(layer, D_MODEL),
            pl.BlockSpec((None, D_MODEL, MEM_WIDTH), lambda g, i: (layer, 0, 0)),
            mem_spec,
            mem_spec,
            pl.BlockSpec((None, MEM_WIDTH, D_MODEL), lambda g, i: (layer, 0, 0)),
        ],
        out_specs=x_spec,
        out_shape=jax.ShapeDtypeStruct(x.shape, F32),
        compiler_params=_params(("parallel", "arbitrary"), vmem),
        name="cross_block",
    )(x, _row_param(gain), wq, mem_k, mem_v, wo)


def _log_sigmoid(x):
    return jnp.minimum(x, 0.0) - jnp.log1p(jnp.exp(-jnp.abs(x)))


def _one_minus_exp2(y, exp_y):
    return jnp.tanh(-y) * (1.0 + exp_y * exp_y)


def _lru_block_cols(n):
    return slice(n * LRU_BLOCK_SIZE, (n + 1) * LRU_BLOCK_SIZE)


def _lru_gate_block(xc, n, wa_ref, ba_ref, wx_ref, bx_ref, lam_ref):
    cs = _lru_block_cols(n)
    xb = xc.astype(BF16)
    r = jax.nn.sigmoid(jnp.dot(xb, wa_ref[n], preferred_element_type=F32) + ba_ref[:, cs])
    ig = jax.nn.sigmoid(jnp.dot(xb, wx_ref[n], preferred_element_type=F32) + bx_ref[:, cs])
    log_a = r * (LRU_C * _log_sigmoid(lam_ref[:, cs]))
    a = jnp.exp(log_a)
    z = _one_minus_exp2(log_a, a)
    u = jnp.where(z > 0.0, z * lax.rsqrt(z), 0.0) * (ig * xc)
    return a, u


def _lru_gates(xc, wa_ref, ba_ref, wx_ref, bx_ref, lam_ref):
    parts = [_lru_gate_block(xc[:, _lru_block_cols(n)], n, wa_ref, ba_ref, wx_ref, bx_ref, lam_ref)
             for n in range(LRU_BLOCKS)]
    return jnp.concatenate([p[0] for p in parts], axis=-1), jnp.concatenate([p[1] for p in parts], axis=-1)


def _lru_prompt_kernel(xb_ref, gate_ref, x_prev_ref, wc_ref, bc_ref, wa_ref, ba_ref, wx_ref, bx_ref, lam_ref,
                       wo_ref, out_ref, hlast_ref, xext_scr, a_scr, u_scr, h_scr, y_scr, *, tiles_per_seq, n_tiles):
    s = pl.program_id(0)
    tm = xb_ref.shape[0]
    pad = V7X_SUBLANES

    @pl.when(s % tiles_per_seq == 0)
    def _():
        xext_scr[0:pad, :] = jnp.zeros((pad, LRU_WIDTH), F32)
        h_scr[...] = jnp.zeros((1, LRU_WIDTH), F32)

    @pl.when(s == 0)
    def _():
        y_scr[...] = jnp.zeros(y_scr.shape, BF16)

    xext_scr[pad:pad + tm, :] = xb_ref[...]
    for n in range(LRU_BLOCKS):
        cs = _lru_block_cols(n)
        out_ref[:, cs] = x_prev_ref[:, cs] + jnp.dot(y_scr[...], wo_ref[:, cs], preferred_element_type=F32)
        xc = bc_ref[:, cs] + wc_ref[CONV_WIDTH - 1:CONV_WIDTH, cs] * xext_scr[pad:pad + tm, cs]
        for k in range(1, CONV_WIDTH):
            xc = xc + wc_ref[CONV_WIDTH - 1 - k:CONV_WIDTH - k, cs] * xext_scr[pad - k:pad - k + tm, cs]
        a_scr[:, cs], u_scr[:, cs] = _lru_gate_block(xc, n, wa_ref, ba_ref, wx_ref, bx_ref, lam_ref)
    xext_scr[0:pad, :] = xext_scr[tm:tm + pad, :]

    def step(t, h):
        h = a_scr[pl.ds(t, 1), :] * h + u_scr[pl.ds(t, 1), :]
        u_scr[pl.ds(t, 1), :] = h
        return h

    h_scr[...] = lax.fori_loop(0, tm, step, h_scr[...], unroll=8)
    y_scr[...] = (u_scr[...] * gate_ref[...]).astype(BF16)

    @pl.when(s < n_tiles)
    def _():
        hlast_ref[...] = u_scr[tm - pad:tm, :]


def _lru_weight_specs(layer_idx):
    def const3(*_):
        return (layer_idx, 0, 0)

    def const4(*_):
        return (layer_idx, 0, 0, 0)

    gate_w = pl.BlockSpec((None, LRU_BLOCKS, LRU_BLOCK_SIZE, LRU_BLOCK_SIZE), const4)
    vec = _row_param_spec(layer_idx, LRU_WIDTH)
    return [
        pl.BlockSpec((None, CONV_WIDTH, LRU_WIDTH), const3),
        vec,
        gate_w,
        vec,
        gate_w,
        vec,
        vec,
        pl.BlockSpec((None, LRU_WIDTH, D_MODEL), const3),
    ]


def _lru_weight_args(w_conv, b_conv, w_gate_a, b_gate_a, w_gate_x, b_gate_x, lam, w_out):
    return (w_conv, _row_param(b_conv), w_gate_a, _row_param(b_gate_a), w_gate_x, _row_param(b_gate_x),
            _row_param(lam), w_out)


def _lru_prompt(xb, gate, x, lw, j, *, tm):
    tiles_per_seq = SEQ // tm
    n_tiles = BATCH * tiles_per_seq
    cur_spec = pl.BlockSpec((tm, LRU_WIDTH), lambda s: (jnp.minimum(s, n_tiles - 1), 0))
    prev_spec = pl.BlockSpec((tm, LRU_WIDTH), lambda s: (jnp.maximum(s - 1, 0), 0))
    vmem = (
        8 * _nbytes((tm, LRU_WIDTH), F32)
        + 3 * _nbytes((tm + V7X_SUBLANES, LRU_WIDTH), F32)
        + 2 * _nbytes((LRU_WIDTH, D_MODEL), BF16)
        + 4 * _nbytes((LRU_BLOCKS, LRU_BLOCK_SIZE, LRU_BLOCK_SIZE), BF16)
        + 10 * _nbytes((tm, LRU_WIDTH), F32)
    )
    return pl.pallas_call(
        functools.partial(_lru_prompt_kernel, tiles_per_seq=tiles_per_seq, n_tiles=n_tiles),
        grid=(n_tiles + 1,),
        in_specs=[cur_spec, cur_spec, prev_spec] + _lru_weight_specs(j),
        out_specs=[
            prev_spec,
            pl.BlockSpec((None, V7X_SUBLANES, LRU_WIDTH),
                         lambda s: (jnp.minimum(s, n_tiles - 1) // tiles_per_seq, 0, 0)),
        ],
        out_shape=[
            jax.ShapeDtypeStruct((BATCH * SEQ, D_MODEL), F32),
            jax.ShapeDtypeStruct((BATCH, V7X_SUBLANES, LRU_WIDTH), F32),
        ],
        scratch_shapes=[
            pltpu.VMEM((tm + V7X_SUBLANES, LRU_WIDTH), F32),
            pltpu.VMEM((tm, LRU_WIDTH), F32),
            pltpu.VMEM((tm, LRU_WIDTH), F32),
            pltpu.VMEM((1, LRU_WIDTH), F32),
            pltpu.VMEM((tm, LRU_WIDTH), BF16),
        ],
        compiler_params=_params(("arbitrary",), vmem),
        name="lru_prompt",
    )(xb, gate, x, *lw)


def _lru_sample_kernel(xb_ref, gate_ref, x_ref, cbuf_ref, h0_ref, wc_ref, bc_ref, wa_ref, ba_ref, wx_ref, bx_ref,
                       lam_ref, wo_ref, out_ref, hs_ref):
    rows = xb_ref.shape[0]
    t = lax.broadcasted_iota(jnp.int32, (rows, LRU_WIDTH), 0) % DEC_SEQ
    xb = xb_ref[...]
    cbuf = cbuf_ref[...]
    xc = bc_ref[...] + wc_ref[CONV_WIDTH - 1:CONV_WIDTH, :] * xb
    for k in range(1, CONV_WIDTH):
        shift_c = CONV_WIDTH - 1 - k
        from_buf = cbuf if shift_c == 0 else pltpu.roll(cbuf, rows - shift_c, 0)
        xk = jnp.where(t >= k, pltpu.roll(xb, k, 0), from_buf)
        xc = xc + wc_ref[CONV_WIDTH - 1 - k:CONV_WIDTH - k, :] * xk

    a, u = _lru_gates(xc, wa_ref, ba_ref, wx_ref, bx_ref, lam_ref)
    k = 1
    while k < DEC_SEQ:
        m = t >= k
        u = jnp.where(m, a * pltpu.roll(u, k, 0) + u, u)
        a = jnp.where(m, a * pltpu.roll(a, k, 0), a)
        k *= 2
    hs = u + a * h0_ref[...]
    hs_ref[...] = hs
    y = (hs * gate_ref[...]).astype(BF16)
    out_ref[...] = x_ref[...] + jnp.dot(y, wo_ref[...], preferred_element_type=F32)


def _lru_sample(xb, gate, x, cbuf_rows, h0_rows, lw, j):
    rows = N_SAMPLE_ROWS
    row_spec0 = pl.BlockSpec((rows, LRU_WIDTH), lambda i: (0, 0))
    vmem = 40 * _nbytes((rows, LRU_WIDTH), F32) + 2 * _nbytes((LRU_WIDTH, D_MODEL), BF16)
    return pl.pallas_call(
        _lru_sample_kernel,
        grid=(1,),
        in_specs=[row_spec0] * 5 + _lru_weight_specs(j),
        out_specs=[row_spec0, row_spec0],
        out_shape=[
            jax.ShapeDtypeStruct((rows, D_MODEL), F32),
            jax.ShapeDtypeStruct((rows, LRU_WIDTH), F32),
        ],
        compiler_params=_params(("arbitrary",), vmem),
        name="lru_sample",
    )(xb, gate, x, cbuf_rows, h0_rows, *lw)


_FFN_TM = 1024
_FFN_TF = 512
_FFN_ROW_CHUNK = 512


def kernel(x_prompt, x_sample, cache_swa_k, cache_swa_v, state_lru_h, state_lru_conv, cache_mem_k, cache_mem_v,
           mem_prompt, ln_ffn1, ffn1_w_in, ffn1_w_out, ln_mix, swa_w_qkv, swa_w_o, swa_sinks, lru_w_in,
           lru_w_conv, lru_b_conv, lru_w_gate_a, lru_b_gate_a, lru_w_gate_x, lru_b_gate_x, lru_lambda,
           lru_w_out, ln_cross, ln_mem, cross_w_q, cross_w_kv, cross_w_o, ln_ffn2, ffn2_w_in, ffn2_w_out,
           ln_final):
    bf = lambda w: w.astype(BF16)
    ffn_seq = []
    for layer in range(DEPTH):
        ffn_seq += [(ffn1_w_in, ffn1_w_out, layer), (ffn2_w_in, ffn2_w_out, layer)]
    ffn_w = (bf(ffn1_w_in[0]), bf(ffn1_w_out[0]))

    def ffn(x, x_extra, gain, layer, final_gain=None):
        nonlocal ffn_w
        nxt = ffn_seq[1] if len(ffn_seq) > 1 else None
        out = _ffn(x, x_extra, gain, layer, *ffn_w, tm=_FFN_TM, tf=_FFN_TF, row_chunk=_FFN_ROW_CHUNK,
                   final_gain=final_gain, cast_next=nxt)
        del ffn_seq[0]
        if nxt is not None:
            ffn_w = out[2:]
        return out[0], out[1]

    swa_w_qkv = bf(swa_w_qkv)
    lru_w_in, lru_w_out = bf(lru_w_in), bf(lru_w_out)
    lru_w_gate_a, lru_w_gate_x = bf(lru_w_gate_a), bf(lru_w_gate_x)
    cross_w_kv = bf(cross_w_kv)

    xp = x_prompt.reshape(N_PROMPT_ROWS, D_MODEL)
    xs = x_sample.reshape(N_SAMPLE_ROWS, D_MODEL)

    mem_rows = mem_prompt.reshape(BATCH * N_MEM, D_MODEL)
    mkv = [
        _norm_linear(mem_rows, ln_mem, layer, cross_w_kv, layer, tm=BATCH * N_MEM, tn=MEM_WIDTH, out_dtype=F32)
        for layer in range(DEPTH)
    ]
    mem_k_prompt = jnp.stack([m[:, :MEM_WIDTH] for m in mkv]).reshape(DEPTH, BATCH, N_MEM, MEM_WIDTH)
    mem_v_prompt = jnp.stack([m[:, MEM_WIDTH:] for m in mkv]).reshape(DEPTH, BATCH, N_MEM, MEM_WIDTH)

    cos_p, sin_p = _rope_tables(jnp.arange(SEQ, dtype=jnp.int32))
    cos_s, sin_s = _rope_tables(PAST_LEN + jnp.arange(N_SAMPLE_ROWS, dtype=jnp.int32) % DEC_SEQ)

    swa_k_p, swa_v_p, swa_k_s, swa_v_s = [], [], [], []
    lru_h_p, lru_conv_p, lru_h_s, lru_conv_s = [], [], [], []

    for layer in range(DEPTH):
        j = layer // 2
        xp, xs = ffn(xp, xs, ln_ffn1, layer)

        if layer % 2 == 0:
            qkv_tm = 1024
            q_p, kv_last, kv_pair = _qkv(xp, ln_mix, layer, swa_w_qkv, j, cos_p, sin_p, tm=qkv_tm, q_dtype=BF16,
                                         seq_len=SEQ)
            o_p = _swa_prompt(q_p, kv_pair, swa_sinks[j])
            xp = _linear_residual(o_p, swa_w_o, j, xp, tm=512)
            kv_tail = kv_last[:, qkv_tm - WINDOW:, :]
            swa_k_p.append(kv_tail[..., :KV_WIDTH].reshape(BATCH, WINDOW, N_KV_HEADS, HEAD_DIM))
            swa_v_p.append(kv_tail[..., KV_WIDTH:].reshape(BATCH, WINDOW, N_KV_HEADS, HEAD_DIM))

            q_s, kv_s = _qkv(xs, ln_mix, layer, swa_w_qkv, j, cos_s, sin_s, tm=N_SAMPLE_ROWS, q_dtype=F32)
            o_s, nk, nv = _swa_sample(
                q_s, kv_s,
                cache_swa_k[j].reshape(DEC_BATCH, WINDOW, KV_WIDTH),
                cache_swa_v[j].reshape(DEC_BATCH, WINDOW, KV_WIDTH),
                swa_sinks[j],
            )
            xs = _linear_residual(o_s, swa_w_o, j, xs, tm=N_SAMPLE_ROWS)
            swa_k_s.append(nk.reshape(DEC_BATCH, WINDOW, N_KV_HEADS, HEAD_DIM))
            swa_v_s.append(nv.reshape(DEC_BATCH, WINDOW, N_KV_HEADS, HEAD_DIM))
        else:
            lw = _lru_weight_args(lru_w_conv, lru_b_conv, lru_w_gate_a, lru_b_gate_a, lru_w_gate_x, lru_b_gate_x,
                                  lru_lambda, lru_w_out)
            xb_p, gate_p = _lru_in(xp, ln_mix, layer, lru_w_in, j, tm=512, tn=LRU_WIDTH)
            xp, h_last = _lru_prompt(xb_p, gate_p, xp, lw, j, tm=256)
            lru_h_p.append(h_last[:, V7X_SUBLANES - 1, :])
            lru_conv_p.append(xb_p.reshape(BATCH, SEQ, LRU_WIDTH)[:, SEQ - (CONV_WIDTH - 1):, :])

            xb_s, gate_s = _lru_in(xs, ln_mix, layer, lru_w_in, j, tm=N_SAMPLE_ROWS, tn=LRU_WIDTH)
            cbuf_rows = jnp.pad(
                state_lru_conv[j], ((0, 0), (0, DEC_SEQ - (CONV_WIDTH - 1)), (0, 0))
            ).reshape(N_SAMPLE_ROWS, LRU_WIDTH)
            h0_rows = jnp.repeat(state_lru_h[j], DEC_SEQ, axis=0)
            xs, hs_s = _lru_sample(xb_s, gate_s, xs, cbuf_rows, h0_rows, lw, j)
            lru_h_s.append(hs_s.reshape(DEC_BATCH, DEC_SEQ, LRU_WIDTH)[:, DEC_SEQ - 1, :])
            xp_tail = jnp.concatenate(
                [state_lru_conv[j], xb_s.reshape(DEC_BATCH, DEC_SEQ, LRU_WIDTH)], axis=1)
            lru_conv_s.append(xp_tail[:, -(CONV_WIDTH - 1):, :])

        xp = _cross_block(xp, ln_cross, cross_w_q, mem_k_prompt, mem_v_prompt, cross_w_o, layer,
                          n_seq=BATCH, seqs=1, rows=1024)
        xs = _cross_block(xs, ln_cross, cross_w_q, cache_mem_k, cache_mem_v, cross_w_o, layer,
                          n_seq=DEC_BATCH, seqs=8, rows=DEC_SEQ)

        fin = ln_final if layer == DEPTH - 1 else None
        xp, xs = ffn(xp, xs, ln_ffn2, layer, final_gain=fin)

    return (
        xp.reshape(BATCH, SEQ, D_MODEL),
        xs.reshape(DEC_BATCH, DEC_SEQ, D_MODEL),
        jnp.stack(swa_k_p), jnp.stack(swa_v_p), jnp.stack(swa_k_s), jnp.stack(swa_v_s),
        jnp.stack(lru_h_p), jnp.stack(lru_conv_p), jnp.stack(lru_h_s), jnp.stack(lru_conv_s),
        mem_k_prompt.reshape(DEPTH, BATCH, N_MEM, N_MEM_HEADS, MEM_HEAD_DIM),
        mem_v_prompt.reshape(DEPTH, BATCH, N_MEM, N_MEM_HEADS, MEM_HEAD_DIM),
    )
```
